```python
import jax, jax.numpy as jnp
from jax import lax
import numpy as np

D_MODEL = 1024
BATCH = 8
SEQ = 4096
DEPTH = 4

N_HEADS = 16
HEAD_DIM = D_MODEL // N_HEADS
D_FF = 2816
CHUNK = 64
LEFT_CHUNKS = 8
BAND = (LEFT_CHUNKS + 1) * CHUNK
REL_CLIP = 256
N_REL = 2 * REL_CLIP + 1
Q_BLOCK = 128
N_A = DEPTH // 2
N_B = DEPTH - N_A
EPS = 1e-6
NEG_INF = -1e30
ATTN_SCALE = HEAD_DIM ** -0.5
FFN_RES_WEIGHT = 0.5

kernel_name = "yoco_chunked_relpos_fox_macaron"


def rms_norm(x, g):
    xf = x.astype(jnp.float32)
    y = xf * lax.rsqrt(jnp.mean(xf * xf, axis=-1, keepdims=True) + EPS)
    return (y * g.astype(jnp.float32)).astype(x.dtype)


def swiglu(x, w_gate, w_up, w_down):
    return (jax.nn.silu(x @ w_gate) * (x @ w_up)) @ w_down


def chunked_relpos_attention(hn, w_qkv, w_o, rel_bias):
    b, s, _ = hn.shape
    nc = s // CHUNK
    q, k, v = jnp.split(hn @ w_qkv, 3, axis=-1)
    q = q.reshape(b, nc, CHUNK, N_HEADS, HEAD_DIM)
    k = k.reshape(b, nc, CHUNK, N_HEADS, HEAD_DIM)
    v = v.reshape(b, nc, CHUNK, N_HEADS, HEAD_DIM)

    def gather_band(t):
        tp = jnp.pad(t, ((0, 0), (LEFT_CHUNKS, 0), (0, 0), (0, 0), (0, 0)))
        return jnp.concatenate([tp[:, j:j + nc] for j in range(LEFT_CHUNKS + 1)], axis=2)

    kb, vb = gather_band(k), gather_band(v)
    qi = jnp.arange(CHUNK)[:, None]
    kj = jnp.arange(BAND)[None, :]
    rel = LEFT_CHUNKS * CHUNK + qi - kj
    rel_idx = jnp.clip(rel, -REL_CLIP, REL_CLIP) + REL_CLIP
    bias = rel_bias[:, rel_idx].astype(jnp.float32)
    key_chunk = jnp.arange(nc)[:, None] - LEFT_CHUNKS + jnp.arange(BAND)[None, :] // CHUNK
    valid = key_chunk >= 0

    logits = jnp.einsum('bcqhd,bckhd->bhcqk', q, kb).astype(jnp.float32) * ATTN_SCALE
    logits = jnp.where(valid[None, None, :, None, :], logits + bias[:, None], NEG_INF)
    p = jax.nn.softmax(logits, axis=-1).astype(vb.dtype)
    o = jnp.einsum('bhcqk,bckhd->bcqhd', p, vb).reshape(b, s, D_MODEL)
    return o @ w_o


def shared_kv_forget(h, kv_norm, w_kvf, b_f):
    b, s, _ = h.shape
    kvf = rms_norm(h, kv_norm) @ w_kvf
    k = kvf[..., :D_MODEL].reshape(b, s, N_HEADS, HEAD_DIM)
    v = kvf[..., D_MODEL:2 * D_MODEL].reshape(b, s, N_HEADS, HEAD_DIM)
    log_f = jax.nn.log_sigmoid(kvf[..., 2 * D_MODEL:].astype(jnp.float32) + b_f.astype(jnp.float32))
    cum_log_f = jnp.cumsum(log_f, axis=1).transpose(0, 2, 1)
    return k, v, cum_log_f


def forgetting_attention(hn, w_q, w_o, k, v, cum_log_f):
    b, s, _ = hn.shape
    q = (hn @ w_q).reshape(b, s, N_HEADS, HEAD_DIM)
    outs = []
    for blk in range(s // Q_BLOCK):
        q0, q1 = blk * Q_BLOCK, (blk + 1) * Q_BLOCK
        logits = jnp.einsum('bqhd,bkhd->bhqk', q[:, q0:q1], k[:, :q1]).astype(jnp.float32) * ATTN_SCALE
        decay = cum_log_f[:, :, q0:q1, None] - cum_log_f[:, :, None, :q1]
        causal = (q0 + jnp.arange(Q_BLOCK))[:, None] >= jnp.arange(q1)[None, :]
        logits = jnp.where(causal, logits + decay, NEG_INF)
        p = jax.nn.softmax(logits, axis=-1).astype(v.dtype)
        outs.append(jnp.einsum('bhqk,bkhd->bqhd', p, v[:, :q1]))
    o = jnp.concatenate(outs, axis=1).reshape(b, s, D_MODEL)
    return o @ w_o


def setup_inputs(seed: int = 0) -> dict:
    key = jax.random.key(seed)
    ks = jax.random.split(key, 16)
    f32 = jnp.float32
    nrm = lambda k, shape, fan_in: jax.random.normal(k, shape, f32) * (fan_in ** -0.5)
    return {
        "x": jax.random.normal(ks[0], (BATCH, SEQ, D_MODEL), f32),
        "ffn_norm": 1.0 + 0.05 * jax.random.normal(ks[1], (DEPTH, 2, D_MODEL), f32),
        "ffn_w_gate": nrm(ks[2], (DEPTH, 2, D_MODEL, D_FF), D_MODEL),
        "ffn_w_up": nrm(ks[3], (DEPTH, 2, D_MODEL, D_FF), D_MODEL),
        "ffn_w_down": nrm(ks[4], (DEPTH, 2, D_FF, D_MODEL), D_FF),
        "mix_norm": 1.0 + 0.05 * jax.random.normal(ks[5], (DEPTH, D_MODEL), f32),
        "a_w_qkv": nrm(ks[6], (N_A, D_MODEL, 3 * D_MODEL), D_MODEL),
        "a_w_o": nrm(ks[7], (N_A, D_MODEL, D_MODEL), D_MODEL),
        "a_rel_bias": 0.5 * jax.random.normal(ks[8], (N_A, N_HEADS, N_REL), f32),
        "kv_norm": 1.0 + 0.05 * jax.random.normal(ks[9], (D_MODEL,), f32),
        "b_w_kvf": nrm(ks[10], (D_MODEL, 2 * D_MODEL + N_HEADS), D_MODEL),
        "b_f_bias": 2.0 + 0.5 * jax.random.normal(ks[11], (N_HEADS,), f32),
        "b_w_q": nrm(ks[12], (N_B, D_MODEL, D_MODEL), D_MODEL),
        "b_w_o": nrm(ks[13], (N_B, D_MODEL, D_MODEL), D_MODEL),
        "final_norm": 1.0 + 0.05 * jax.random.normal(ks[14], (D_MODEL,), f32),
    }


def reference(x, ffn_norm, ffn_w_gate, ffn_w_up, ffn_w_down, mix_norm, a_w_qkv, a_w_o,
              a_rel_bias, kv_norm, b_w_kvf, b_f_bias, b_w_q, b_w_o, final_norm):
    def half_ffn(h, layer, pos):
        hn = rms_norm(h, ffn_norm[layer, pos])
        return h + FFN_RES_WEIGHT * swiglu(hn, ffn_w_gate[layer, pos], ffn_w_up[layer, pos],
                                           ffn_w_down[layer, pos])

    h = x
    for layer in range(N_A):
        h = half_ffn(h, layer, 0)
        h = h + chunked_relpos_attention(rms_norm(h, mix_norm[layer]), a_w_qkv[layer],
                                         a_w_o[layer], a_rel_bias[layer])
        h = half_ffn(h, layer, 1)

    k_sh, v_sh, cum_log_f = shared_kv_forget(h, kv_norm, b_w_kvf, b_f_bias)

    for lb in range(N_B):
        layer = N_A + lb
        h = half_ffn(h, layer, 0)
        h = h + forgetting_attention(rms_norm(h, mix_norm[layer]), b_w_q[lb], b_w_o[lb],
                                     k_sh, v_sh, cum_log_f)
        h = half_ffn(h, layer, 1)

    return rms_norm(h, final_norm)
```

```python
import functools

import jax
import jax.numpy as jnp
from jax import lax
from jax.experimental import pallas as pl
from jax.experimental.pallas import tpu as pltpu

D_MODEL = 1024
N_HEADS = 16
HEAD_DIM = 64
N_PAIRS = N_HEADS // 2
PAIR_W = 2 * HEAD_DIM
D_FF = 2816
CHUNK = 64
LEFT_CHUNKS = 8
REL_CLIP = 256
EPS = 1e-6
NEG_INF = -1e30
ATTN_SCALE = HEAD_DIM ** -0.5
FFN_RES_WEIGHT = 0.5

TM = 512
FF_CHUNK = 256
QA = 256
A_KBLOCKS = LEFT_CHUNKS * CHUNK // QA + 1
QB = 512
EXT_W = 128
EXT_PER_HEAD = 6
VMEM_LIMIT = 56 * 1024 * 1024

F32 = jnp.float32
BF16 = jnp.bfloat16


def _rms_norm(x, g):
    y = x * lax.rsqrt(jnp.mean(x * x, axis=-1, keepdims=True) + EPS)
    return y * g


def _resident(block_shape, index_map):
    return pl.BlockSpec(block_shape, index_map, pipeline_mode=pl.Buffered(1))


def _params(*semantics):
    return pltpu.CompilerParams(dimension_semantics=semantics, vmem_limit_bytes=VMEM_LIMIT)


def _ffn_kernel(x_ref, g_ref, wg_ref, wu_ref, wd_ref, gf_ref, o_ref, a_ref, *, final):
    x = x_ref[...]
    hn = _rms_norm(x, g_ref[...]).astype(BF16)
    for c in range(D_FF // FF_CHUNK):
        sl = slice(c * FF_CHUNK, (c + 1) * FF_CHUNK)
        gate = jnp.dot(hn, wg_ref[:, sl], preferred_element_type=F32)
        up = jnp.dot(hn, wu_ref[:, sl], preferred_element_type=F32)
        a_ref[:, sl] = (gate * jax.nn.sigmoid(gate) * up).astype(BF16)
    y = jnp.dot(a_ref[...], wd_ref[...], preferred_element_type=F32)
    out = x + FFN_RES_WEIGHT * y
    if final:
        out = _rms_norm(out, gf_ref[...])
    o_ref[...] = out


def _ffn(h, g, wg, wu, wd, gf, final):
    t = h.shape[0]
    return pl.pallas_call(
        functools.partial(_ffn_kernel, final=final),
        out_shape=jax.ShapeDtypeStruct((t, D_MODEL), F32),
        grid=(t // TM,),
        in_specs=[
            pl.BlockSpec((TM, D_MODEL), lambda i: (i, 0)),
            _resident((1, D_MODEL), lambda i: (0, 0)),
            _resident((D_MODEL, D_FF), lambda i: (0, 0)),
            _resident((D_MODEL, D_FF), lambda i: (0, 0)),
            _resident((D_FF, D_MODEL), lambda i: (0, 0)),
            _resident((1, D_MODEL), lambda i: (0, 0)),
        ],
        out_specs=pl.BlockSpec((TM, D_MODEL), lambda i: (i, 0)),
        scratch_shapes=[pltpu.VMEM((TM, D_FF), BF16)],
        compiler_params=_params("parallel"),
        name="ffn_final" if final else "ffn",
    )(h, g, wg, wu, wd, gf)


def _proj_kernel(x_ref, g_ref, w_ref, *o_refs, scale_first):
    hn = _rms_norm(x_ref[...], g_ref[...]).astype(BF16)
    for n, o_ref in enumerate(o_refs):
        y = jnp.dot(hn, w_ref[:, n * D_MODEL:(n + 1) * D_MODEL], preferred_element_type=F32)
        if scale_first and n == 0:
            y = y * ATTN_SCALE
        for p in range(N_PAIRS):
            o_ref[p] = y[:, p * PAIR_W:(p + 1) * PAIR_W].astype(BF16)


def _proj(h, g, w, batch, seq, scale_first, name):
    n_out = w.shape[1] // D_MODEL
    ns = seq // TM
    out_sds = jax.ShapeDtypeStruct((batch, N_PAIRS, seq, PAIR_W), BF16)
    out_spec = pl.BlockSpec((None, N_PAIRS, TM, PAIR_W), lambda b, i: (b, 0, i, 0))
    return pl.pallas_call(
        functools.partial(_proj_kernel, scale_first=scale_first),
        out_shape=[out_sds] * n_out,
        grid=(batch, ns),
        in_specs=[
            pl.BlockSpec((TM, D_MODEL), lambda b, i: (b * ns + i, 0)),
            _resident((1, D_MODEL), lambda b, i: (0, 0)),
            _resident((D_MODEL, n_out * D_MODEL), lambda b, i: (0, 0)),
        ],
        out_specs=[out_spec] * n_out,
        compiler_params=_params("parallel", "parallel"),
        name=name,
    )(h, g, w)


def _oproj_kernel(o_ref, w_ref, h_ref, out_ref):
    o = jnp.concatenate([o_ref[p] for p in range(N_PAIRS)], axis=1)
    out_ref[...] = h_ref[...] + jnp.dot(o, w_ref[...], preferred_element_type=F32)


def _oproj(o, w, h, batch, seq):
    ns = seq // TM
    return pl.pallas_call(
        _oproj_kernel,
        out_shape=jax.ShapeDtypeStruct(h.shape, F32),
        grid=(batch, ns),
        in_specs=[
            pl.BlockSpec((None, N_PAIRS, TM, PAIR_W), lambda b, i: (b, 0, i, 0)),
            _resident((D_MODEL, D_MODEL), lambda b, i: (0, 0)),
            pl.BlockSpec((TM, D_MODEL), lambda b, i: (b * ns + i, 0)),
        ],
        out_specs=pl.BlockSpec((TM, D_MODEL), lambda b, i: (b * ns + i, 0)),
        compiler_params=_params("parallel", "parallel"),
        name="oproj",
    )(o, w, h)


def _head_lane_mask(shape, head_in_pair):
    lane = lax.broadcasted_iota(jnp.int32, shape, 1)
    return (lane >= HEAD_DIM * head_in_pair) & (lane < HEAD_DIM * (head_in_pair + 1))


def _dot_nt(a, b):
    return lax.dot_general(a, b, (((1,), (1,)), ((), ())), preferred_element_type=F32)


def _attn_a_kernel(q_ref, *refs):
    k_refs = refs[:A_KBLOCKS]
    v_refs = refs[A_KBLOCKS:2 * A_KBLOCKS]
    bias_ref, o_ref = refs[2 * A_KBLOCKS:]
    i = pl.program_id(2)
    q = q_ref[...]
    ks = [r[...] for r in k_refs]
    vs = [r[...] for r in v_refs]
    pens = [jnp.where(i - (A_KBLOCKS - 1) + j >= 0, 0.0, NEG_INF).astype(F32)
            for j in range(A_KBLOCKS)]
    outs = []
    for hh in range(2):
        qm = jnp.where(_head_lane_mask(q.shape, hh), q, jnp.zeros_like(q))
        ss = [_dot_nt(qm, ks[j]) + bias_ref[hh, j] for j in range(A_KBLOCKS)]
        m = functools.reduce(
            jnp.maximum, [ss[j].max(axis=1, keepdims=True) + pens[j] for j in range(A_KBLOCKS)])
        l = jnp.zeros((QA, 1), F32)
        acc = jnp.zeros((QA, PAIR_W), F32)
        for j in range(A_KBLOCKS):
            pr = jnp.exp(ss[j] - (m - pens[j]))
            l = l + pr.sum(axis=1, keepdims=True)
            acc = acc + jnp.dot(pr.astype(BF16), vs[j], preferred_element_type=F32)
        outs.append(acc / l)
    o_ref[...] = jnp.where(_head_lane_mask(outs[0].shape, 0), outs[0], outs[1]).astype(BF16)


def _attn_a(q, k, v, bias):
    batch, _, seq, _ = q.shape
    nq = seq // QA

    def kv_spec(j):
        back = A_KBLOCKS - 1 - j
        return pl.BlockSpec((None, None, QA, PAIR_W),
                            lambda b, p, i: (b, p, jnp.maximum(i - back, 0), 0))

    return pl.pallas_call(
        _attn_a_kernel,
        out_shape=jax.ShapeDtypeStruct(q.shape, BF16),
        grid=(batch, N_PAIRS, nq),
        in_specs=(
            [pl.BlockSpec((None, None, QA, PAIR_W), lambda b, p, i: (b, p, i, 0))]
            + [kv_spec(j) for j in range(A_KBLOCKS)]
            + [kv_spec(j) for j in range(A_KBLOCKS)]
            + [pl.BlockSpec((2, A_KBLOCKS, QA, QA), lambda b, p, i: (p, 0, 0, 0))]
        ),
        out_specs=pl.BlockSpec((None, None, QA, PAIR_W), lambda b, p, i: (b, p, i, 0)),
        compiler_params=_params("parallel", "parallel", "arbitrary"),
        name="attn_a",
    )(q, *([k] * A_KBLOCKS), *([v] * A_KBLOCKS), bias)


def _attn_a_bias_table(rel_bias):
    r = jnp.arange(QA)[:, None]
    c = jnp.arange(QA)[None, :]
    tabs = []
    for j in range(A_KBLOCKS):
        shift = (A_KBLOCKS - 1 - j) * QA
        rel = r - c + shift
        dchunk = r // CHUNK - c // CHUNK + shift // CHUNK
        valid = (dchunk >= 0) & (dchunk <= LEFT_CHUNKS)
        idx = jnp.clip(rel, -REL_CLIP, REL_CLIP) + REL_CLIP
        tabs.append(jnp.where(valid[None], rel_bias[:, idx].astype(F32), NEG_INF))
    return jnp.stack(tabs, axis=1)


def _split3(x):
    hi = x.astype(BF16)
    r1 = x - hi.astype(F32)
    mid = r1.astype(BF16)
    lo = (r1 - mid.astype(F32)).astype(BF16)
    return hi, mid, lo


def _kvf_kernel(x_ref, g_ref, wkv_ref, wf_ref, bf_ref, tri_ref, selq_ref, selk_ref, oneq_ref,
                onek_ref, k_ref, v_ref, qe_ref, ke_ref, carry_ref):
    @pl.when(pl.program_id(1) == 0)
    def _():
        carry_ref[...] = jnp.zeros_like(carry_ref)

    hn = _rms_norm(x_ref[...], g_ref[...]).astype(BF16)
    for n, o_ref in enumerate((k_ref, v_ref)):
        y = jnp.dot(hn, wkv_ref[:, n * D_MODEL:(n + 1) * D_MODEL], preferred_element_type=F32)
        for p in range(N_PAIRS):
            o_ref[p] = y[:, p * PAIR_W:(p + 1) * PAIR_W].astype(BF16)
    z = jnp.dot(hn, wf_ref[...], preferred_element_type=F32) + bf_ref[...]
    log_f = jnp.minimum(z, 0.0) - jnp.log1p(jnp.exp(-jnp.abs(z)))
    tri = tri_ref[...]
    cum = sum(jnp.dot(tri, piece, preferred_element_type=F32) for piece in _split3(log_f))
    cum = cum + carry_ref[...]
    carry_ref[...] = cum[TM - 1:TM, :]
    pieces = jnp.concatenate(_split3(cum), axis=1)
    qe_ref[...] = (jnp.dot(pieces, selq_ref[...], preferred_element_type=F32)
                   + oneq_ref[...]).astype(BF16)
    ke_ref[...] = (jnp.dot(pieces, selk_ref[...], preferred_element_type=F32)
                   + onek_ref[...]).astype(BF16)


def _kvf(h, g, wkv, wf, bf, batch, seq):
    ns = seq // TM
    tri = (jnp.arange(TM)[:, None] >= jnp.arange(TM)[None, :]).astype(BF16)
    half = EXT_PER_HEAD // 2
    src = jnp.arange(3 * EXT_W)[:, None]
    dst = jnp.arange(EXT_W)[None, :]
    src_head, src_piece = src % EXT_W, src // EXT_W
    dst_head, dst_slot = dst // EXT_PER_HEAD, dst % EXT_PER_HEAD
    same_head = (src_head == dst_head) & (src_head < N_HEADS)
    sel_q = (same_head & (dst_slot == half + src_piece)).astype(BF16)
    sel_k = -(same_head & (dst_slot == src_piece)).astype(BF16)
    live = jnp.arange(EXT_W) < N_HEADS * EXT_PER_HEAD
    slot = jnp.arange(EXT_W) % EXT_PER_HEAD
    one_q = (live & (slot < half)).astype(F32).reshape(1, EXT_W)
    one_k = (live & (slot >= half)).astype(F32).reshape(1, EXT_W)

    kv_sds = jax.ShapeDtypeStruct((batch, N_PAIRS, seq, PAIR_W), BF16)
    kv_spec = pl.BlockSpec((None, N_PAIRS, TM, PAIR_W), lambda b, i: (b, 0, i, 0))
    ext_sds = jax.ShapeDtypeStruct((batch, seq, EXT_W), BF16)
    ext_spec = pl.BlockSpec((None, TM, EXT_W), lambda b, i: (b, i, 0))
    return pl.pallas_call(
        _kvf_kernel,
        out_shape=[kv_sds, kv_sds, ext_sds, ext_sds],
        grid=(batch, ns),
        in_specs=[
            pl.BlockSpec((TM, D_MODEL), lambda b, i: (b * ns + i, 0)),
            _resident((1, D_MODEL), lambda b, i: (0, 0)),
            _resident((D_MODEL, 2 * D_MODEL), lambda b, i: (0, 0)),
            _resident((D_MODEL, EXT_W), lambda b, i: (0, 0)),
            _resident((1, EXT_W), lambda b, i: (0, 0)),
            _resident((TM, TM), lambda b, i: (0, 0)),
            _resident((3 * EXT_W, EXT_W), lambda b, i: (0, 0)),
            _resident((3 * EXT_W, EXT_W), lambda b, i: (0, 0)),
            _resident((1, EXT_W), lambda b, i: (0, 0)),
            _resident((1, EXT_W), lambda b, i: (0, 0)),
        ],
        out_specs=[kv_spec, kv_spec, ext_spec, ext_spec],
        scratch_shapes=[pltpu.VMEM((1, EXT_W), F32)],
        compiler_params=_params("parallel", "arbitrary"),
        name="kvf",
    )(h, g, wkv, wf, bf, tri, sel_q, sel_k, one_q, one_k)


def _fox_kernel(q_ref, k_ref, v_ref, qe_ref, ke_ref, o_ref):
    p = pl.program_id(1)
    i = pl.program_id(2)
    q = q_ref[...]
    qe = qe_ref[...]
    ext_lane = lax.broadcasted_iota(jnp.int32, qe.shape, 1)
    row = lax.broadcasted_iota(jnp.int32, (QB, QB), 0)
    col = lax.broadcasted_iota(jnp.int32, (QB, QB), 1)
    outs = []
    for hh in range(2):
        head = 2 * p + hh
        qm = jnp.where(_head_lane_mask(q.shape, hh), q, jnp.zeros_like(q))
        own = (ext_lane >= EXT_PER_HEAD * head) & (ext_lane < EXT_PER_HEAD * (head + 1))
        qa = jnp.concatenate([qm, jnp.where(own, qe, jnp.zeros_like(qe))], axis=1)

        def scores(off):
            ka = jnp.concatenate([k_ref[pl.ds(off, QB), :], ke_ref[pl.ds(off, QB), :]], axis=1)
            return _dot_nt(qa, ka)

        def update(carry, s, off):
            m, l, acc = carry
            m_new = jnp.maximum(m, s.max(axis=1, keepdims=True))
            alpha = jnp.exp(m - m_new)
            pr = jnp.exp(s - m_new)
            l = alpha * l + pr.sum(axis=1, keepdims=True)
            pv = jnp.dot(pr.astype(BF16), v_ref[pl.ds(off, QB), :], preferred_element_type=F32)
            return m_new, l, alpha * acc + pv

        def body(j, carry):
            off = pl.multiple_of(j * QB, QB)
            return update(carry, scores(off), off)

        init = (jnp.full((QB, 1), NEG_INF, F32), jnp.zeros((QB, 1), F32),
                jnp.zeros((QB, PAIR_W), F32))
        carry = lax.fori_loop(0, i, body, init)
        off = pl.multiple_of(i * QB, QB)
        s_diag = jnp.where(row >= col, scores(off), NEG_INF)
        _, l, acc = update(carry, s_diag, off)
        outs.append(acc / l)
    o_ref[...] = jnp.where(_head_lane_mask(outs[0].shape, 0), outs[0], outs[1]).astype(BF16)


def _fox(q, k, v, q_ext, k_ext):
    batch, _, seq, _ = q.shape
    nq = seq // QB
    return pl.pallas_call(
        _fox_kernel,
        out_shape=jax.ShapeDtypeStruct(q.shape, BF16),
        grid=(batch, N_PAIRS, nq),
        in_specs=[
            pl.BlockSpec((None, None, QB, PAIR_W), lambda b, p, i: (b, p, i, 0)),
            pl.BlockSpec((None, None, seq, PAIR_W), lambda b, p, i: (b, p, 0, 0)),
            pl.BlockSpec((None, None, seq, PAIR_W), lambda b, p, i: (b, p, 0, 0)),
            pl.BlockSpec((None, QB, EXT_W), lambda b, p, i: (b, i, 0)),
            pl.BlockSpec((None, seq, EXT_W), lambda b, p, i: (b, 0, 0)),
        ],
        out_specs=pl.BlockSpec((None, None, QB, PAIR_W), lambda b, p, i: (b, p, i, 0)),
        compiler_params=_params("parallel", "parallel", "arbitrary"),
        name="fox",
    )(q, k, v, q_ext, k_ext)


def kernel(x, ffn_norm, ffn_w_gate, ffn_w_up, ffn_w_down, mix_norm, a_w_qkv, a_w_o, a_rel_bias,
           kv_norm, b_w_kvf, b_f_bias, b_w_q, b_w_o, final_norm):
    batch, seq, _ = x.shape
    depth = ffn_norm.shape[0]
    n_a = a_w_qkv.shape[0]
    n_b = b_w_q.shape[0]
    assert seq % TM == 0 and seq % QA == 0 and seq % QB == 0 and n_a + n_b == depth

    wg = ffn_w_gate.astype(BF16)
    wu = ffn_w_up.astype(BF16)
    wd = ffn_w_down.astype(BF16)
    gf = final_norm.reshape(1, D_MODEL)

    def half_ffn(h, layer, pos, final=False):
        return _ffn(h, ffn_norm[layer, pos].reshape(1, D_MODEL), wg[layer, pos], wu[layer, pos],
                    wd[layer, pos], gf, final)

    h = x.reshape(batch * seq, D_MODEL)
    for layer in range(n_a):
        h = half_ffn(h, layer, 0)
        q, k, v = _proj(h, mix_norm[layer].reshape(1, D_MODEL), a_w_qkv[layer].astype(BF16),
                        batch, seq, True, "qkv_proj")
        o = _attn_a(q, k, v, _attn_a_bias_table(a_rel_bias[layer]))
        h = _oproj(o, a_w_o[layer].astype(BF16), h, batch, seq)
        h = half_ffn(h, layer, 1)

    wkv = b_w_kvf[:, :2 * D_MODEL].astype(BF16)
    wf = jnp.pad(b_w_kvf[:, 2 * D_MODEL:], ((0, 0), (0, EXT_W - N_HEADS))).astype(BF16)
    bf = jnp.pad(b_f_bias, (0, EXT_W - N_HEADS)).reshape(1, EXT_W).astype(F32)
    k_sh, v_sh, q_ext, k_ext = _kvf(h, kv_norm.reshape(1, D_MODEL), wkv, wf, bf, batch, seq)

    for lb in range(n_b):
        layer = n_a + lb
        h = half_ffn(h, layer, 0)
        (q,) = _proj(h, mix_norm[layer].reshape(1, D_MODEL), b_w_q[lb].astype(BF16),
                     batch, seq, True, "q_proj")
        o = _fox(q, k_sh, v_sh, q_ext, k_ext)
        h = _oproj(o, b_w_o[lb].astype(BF16), h, batch, seq)
        h = half_ffn(h, layer, 1, final=(layer == depth - 1))

    return h.reshape(batch, seq, D_MODEL)
```

```python
import functools

import jax
import jax.numpy as jnp
from jax import lax
from jax.experimental import pallas as pl
from jax.experimental.pallas import tpu as pltpu

D_MODEL = 1024
N_HEADS = 16
HEAD_DIM = 64
N_PAIRS = N_HEADS // 2
PAIR_W = 2 * HEAD_DIM
D_FF = 2816
CHUNK = 64
LEFT_CHUNKS = 8
REL_CLIP = 256
EPS = 1e-6
NEG_INF = -1e30
ATTN_SCALE = HEAD_DIM ** -0.5
LOG2E = 1.4426950408889634
Q_SCALE = ATTN_SCALE * LOG2E
FFN_RES_WEIGHT = 0.5

TM = 512
FF_CHUNK = 256
QA = 256
A_KBLOCKS = LEFT_CHUNKS * CHUNK // QA + 1
A_PAIRS = 4
QB = 512
FOX_PAIRS = 2
EXT_W = 128
EXT_PER_HEAD = 6
LANES = 128
STRIP = 32
VMEM_LIMIT = 56 * 1024 * 1024

F32 = jnp.float32
BF16 = jnp.bfloat16


def _rms_norm(x, g):
    y = x * lax.rsqrt(jnp.mean(x * x, axis=-1, keepdims=True) + EPS)
    return y * g


def _resident(block_shape, index_map):
    return pl.BlockSpec(block_shape, index_map, pipeline_mode=pl.Buffered(1))


def _params(*semantics):
    return pltpu.CompilerParams(dimension_semantics=semantics, vmem_limit_bytes=VMEM_LIMIT)


def _ffn_kernel(x_ref, g_ref, wg_ref, wu_ref, wd_ref, gf_ref, o_ref, a_ref, *, final):
    x = x_ref[...]
    hn = _rms_norm(x, g_ref[...]).astype(BF16)
    for c in range(D_FF // FF_CHUNK):
        sl = slice(c * FF_CHUNK, (c + 1) * FF_CHUNK)
        gate = jnp.dot(hn, wg_ref[:, sl], preferred_element_type=F32)
        up = jnp.dot(hn, wu_ref[:, sl], preferred_element_type=F32)
        a_ref[:, sl] = (gate * jax.nn.sigmoid(gate) * up).astype(BF16)
    y = jnp.dot(a_ref[...], wd_ref[...], preferred_element_type=F32)
    out = x + FFN_RES_WEIGHT * y
    if final:
        out = _rms_norm(out, gf_ref[...])
    o_ref[...] = out


def _ffn(h, g, wg, wu, wd, gf, final):
    t = h.shape[0]
    return pl.pallas_call(
        functools.partial(_ffn_kernel, final=final),
        out_shape=jax.ShapeDtypeStruct((t, D_MODEL), F32),
        grid=(t // TM,),
        in_specs=[
            pl.BlockSpec((TM, D_MODEL), lambda i: (i, 0)),
            _resident((1, D_MODEL), lambda i: (0, 0)),
            _resident((D_MODEL, D_FF), lambda i: (0, 0)),
            _resident((D_MODEL, D_FF), lambda i: (0, 0)),
            _resident((D_FF, D_MODEL), lambda i: (0, 0)),
            _resident((1, D_MODEL), lambda i: (0, 0)),
        ],
        out_specs=pl.BlockSpec((TM, D_MODEL), lambda i: (i, 0)),
        scratch_shapes=[pltpu.VMEM((TM, D_FF), BF16)],
        compiler_params=_params("parallel"),
        name="ffn_final" if final else "ffn",
    )(h, g, wg, wu, wd, gf)


def _proj_kernel(x_ref, g_ref, w_ref, *o_refs, scale_first):
    hn = _rms_norm(x_ref[...], g_ref[...]).astype(BF16)
    for n, o_ref in enumerate(o_refs):
        y = jnp.dot(hn, w_ref[:, n * D_MODEL:(n + 1) * D_MODEL], preferred_element_type=F32)
        if scale_first and n == 0:
            y = y * Q_SCALE
        for p in range(N_PAIRS):
            o_ref[p] = y[:, p * PAIR_W:(p + 1) * PAIR_W].astype(BF16)


def _proj(h, g, w, batch, seq, scale_first, name):
    n_out = w.shape[1] // D_MODEL
    ns = seq // TM
    out_sds = jax.ShapeDtypeStruct((batch, N_PAIRS, seq, PAIR_W), BF16)
    out_spec = pl.BlockSpec((None, N_PAIRS, TM, PAIR_W), lambda b, i: (b, 0, i, 0))
    return pl.pallas_call(
        functools.partial(_proj_kernel, scale_first=scale_first),
        out_shape=[out_sds] * n_out,
        grid=(batch, ns),
        in_specs=[
            pl.BlockSpec((TM, D_MODEL), lambda b, i: (b * ns + i, 0)),
            _resident((1, D_MODEL), lambda b, i: (0, 0)),
            _resident((D_MODEL, n_out * D_MODEL), lambda b, i: (0, 0)),
        ],
        out_specs=[out_spec] * n_out,
        compiler_params=_params("parallel", "parallel"),
        name=name,
    )(h, g, w)


def _oproj_kernel(o_ref, w_ref, h_ref, out_ref):
    o = jnp.concatenate([o_ref[p] for p in range(N_PAIRS)], axis=1)
    out_ref[...] = h_ref[...] + jnp.dot(o, w_ref[...], preferred_element_type=F32)


def _oproj(o, w, h, batch, seq):
    ns = seq // TM
    return pl.pallas_call(
        _oproj_kernel,
        out_shape=jax.ShapeDtypeStruct(h.shape, F32),
        grid=(batch, ns),
        in_specs=[
            pl.BlockSpec((None, N_PAIRS, TM, PAIR_W), lambda b, i: (b, 0, i, 0)),
            _resident((D_MODEL, D_MODEL), lambda b, i: (0, 0)),
            pl.BlockSpec((TM, D_MODEL), lambda b, i: (b * ns + i, 0)),
        ],
        out_specs=pl.BlockSpec((TM, D_MODEL), lambda b, i: (b * ns + i, 0)),
        compiler_params=_params("parallel", "parallel"),
        name="oproj",
    )(o, w, h)


def _head_lane_mask(shape, head_in_pair):
    lane = lax.broadcasted_iota(jnp.int32, shape, 1)
    return (lane >= HEAD_DIM * head_in_pair) & (lane < HEAD_DIM * (head_in_pair + 1))


def _dot_nt(a, b):
    return lax.dot_general(a, b, (((1,), (1,)), ((), ())), preferred_element_type=F32)


def _lane_chunks(ref, lead, rows, width):
    return [ref[(*lead, rows, slice(c * LANES, (c + 1) * LANES))] for c in range(width // LANES)]


def _attn_a_kernel(q_ref, *refs):
    k_refs = refs[:A_KBLOCKS]
    v_refs = refs[A_KBLOCKS:2 * A_KBLOCKS]
    bias_ref, o_ref, s_ref, p_ref, m_ref, l_ref = refs[2 * A_KBLOCKS:]
    i = pl.program_id(2)
    pens = [jnp.where(i - (A_KBLOCKS - 1) + j >= 0, 0.0, NEG_INF).astype(F32)
            for j in range(A_KBLOCKS)]
    heads = [(pp, hh) for pp in range(A_PAIRS) for hh in range(2)]
    strips = [slice(r, r + STRIP) for r in range(0, QA, STRIP)]
    for pp, hh in heads:
        q = q_ref[pp]
        qm = jnp.where(_head_lane_mask(q.shape, hh), q, jnp.zeros_like(q))
        for j in range(A_KBLOCKS):
            s_ref[pp, hh, j] = _dot_nt(qm, k_refs[j][pp])
    for pp, hh in heads:
        for rows in strips:
            blk_max = []
            for j in range(A_KBLOCKS):
                t = [s + b for s, b in zip(_lane_chunks(s_ref, (pp, hh, j), rows, QA),
                                           _lane_chunks(bias_ref, (2 * pp + hh, j), rows, QA))]
                for c, tc in enumerate(t):
                    s_ref[pp, hh, j, rows, c * LANES:(c + 1) * LANES] = tc
                blk_max.append(functools.reduce(jnp.maximum, t) + pens[j])
            m = functools.reduce(jnp.maximum, blk_max).max(axis=1, keepdims=True)
            m_ref[pp, hh, rows, :] = jnp.broadcast_to(m, (STRIP, LANES))
    for pp, hh in heads:
        for rows in strips:
            m = m_ref[pp, hh, rows, :]
            tot = jnp.zeros((STRIP, LANES), F32)
            for j in range(A_KBLOCKS):
                shift = m - pens[j]
                for c, tc in enumerate(_lane_chunks(s_ref, (pp, hh, j), rows, QA)):
                    pr = jnp.exp2(tc - shift)
                    tot = tot + pr
                    p_ref[pp, hh, j, rows, c * LANES:(c + 1) * LANES] = pr.astype(BF16)
            l_ref[pp, hh, rows, :] = jnp.broadcast_to(
                tot.sum(axis=1, keepdims=True), (STRIP, LANES))
    for pp in range(A_PAIRS):
        outs = []
        for hh in range(2):
            acc = sum(jnp.dot(p_ref[pp, hh, j], v_refs[j][pp], preferred_element_type=F32)
                      for j in range(A_KBLOCKS))
            outs.append(acc / l_ref[pp, hh])
        o_ref[pp] = jnp.where(_head_lane_mask(outs[0].shape, 0), outs[0], outs[1]).astype(BF16)


def _attn_a(q, k, v, bias):
    batch, _, seq, _ = q.shape
    nq = seq // QA

    def kv_spec(j):
        back = A_KBLOCKS - 1 - j
        return pl.BlockSpec((None, A_PAIRS, QA, PAIR_W),
                            lambda b, p, i: (b, p, jnp.maximum(i - back, 0), 0))

    return pl.pallas_call(
        _attn_a_kernel,
        out_shape=jax.ShapeDtypeStruct(q.shape, BF16),
        grid=(batch, N_PAIRS // A_PAIRS, nq),
        in_specs=(
            [pl.BlockSpec((None, A_PAIRS, QA, PAIR_W), lambda b, p, i: (b, p, i, 0))]
            + [kv_spec(j) for j in range(A_KBLOCKS)]
            + [kv_spec(j) for j in range(A_KBLOCKS)]
            + [pl.BlockSpec((2 * A_PAIRS, A_KBLOCKS, QA, QA), lambda b, p, i: (p, 0, 0, 0))]
        ),
        out_specs=pl.BlockSpec((None, A_PAIRS, QA, PAIR_W), lambda b, p, i: (b, p, i, 0)),
        scratch_shapes=[
            pltpu.VMEM((A_PAIRS, 2, A_KBLOCKS, QA, QA), F32),
            pltpu.VMEM((A_PAIRS, 2, A_KBLOCKS, QA, QA), BF16),
            pltpu.VMEM((A_PAIRS, 2, QA, LANES), F32),
            pltpu.VMEM((A_PAIRS, 2, QA, LANES), F32),
        ],
        compiler_params=_params("parallel", "parallel", "arbitrary"),
        name="attn_a",
    )(q, *([k] * A_KBLOCKS), *([v] * A_KBLOCKS), bias)


def _attn_a_bias_table(rel_bias):
    n_heads = rel_bias.shape[0]
    back = (A_KBLOCKS - 1) * QA
    keys = A_KBLOCKS * QA
    d = back + (QA - 1) - jnp.arange(keys + QA - 1)
    f = jnp.take(rel_bias.astype(F32), jnp.clip(d, -REL_CLIP, REL_CLIP) + REL_CLIP, axis=1)
    period = keys + QA
    g = jnp.pad(f, ((0, 0), (0, period - f.shape[1])))
    skew = jnp.tile(g, (1, QA))[:, :QA * (period - 1)].reshape(n_heads, QA, period - 1)
    toeplitz = skew[:, :, QA - 1:QA - 1 + keys]
    r = jnp.arange(QA)[:, None]
    x = jnp.arange(keys)[None, :]
    dchunk = (r + back) // CHUNK - x // CHUNK
    valid = (dchunk >= 0) & (dchunk <= LEFT_CHUNKS)
    table = jnp.where(valid[None], toeplitz * LOG2E, NEG_INF)
    return table.reshape(n_heads, QA, A_KBLOCKS, QA).transpose(0, 2, 1, 3)


def _split3(x):
    hi = x.astype(BF16)
    r1 = x - hi.astype(F32)
    mid = r1.astype(BF16)
    lo = (r1 - mid.astype(F32)).astype(BF16)
    return hi, mid, lo


def _kvf_kernel(x_ref, g_ref, wkv_ref, wf_ref, bf_ref, tri_ref, selq_ref, selk_ref, oneq_ref,
                onek_ref, k_ref, v_ref, qe_ref, ke_ref, carry_ref):
    @pl.when(pl.program_id(1) == 0)
    def _():
        carry_ref[...] = jnp.zeros_like(carry_ref)

    hn = _rms_norm(x_ref[...], g_ref[...]).astype(BF16)
    for n, o_ref in enumerate((k_ref, v_ref)):
        y = jnp.dot(hn, wkv_ref[:, n * D_MODEL:(n + 1) * D_MODEL], preferred_element_type=F32)
        for p in range(N_PAIRS):
            o_ref[p] = y[:, p * PAIR_W:(p + 1) * PAIR_W].astype(BF16)
    z = jnp.dot(hn, wf_ref[...], preferred_element_type=F32) + bf_ref[...]
    log_f = jnp.minimum(z, 0.0) - jnp.log1p(jnp.exp(-jnp.abs(z)))
    tri = tri_ref[...]
    cum = sum(jnp.dot(tri, piece, preferred_element_type=F32) for piece in _split3(log_f))
    cum = cum + carry_ref[...]
    carry_ref[...] = cum[TM - 1:TM, :]
    pieces = jnp.concatenate(_split3(cum * LOG2E), axis=1)
    qe_ref[...] = (jnp.dot(pieces, selq_ref[...], preferred_element_type=F32)
                   + oneq_ref[...]).astype(BF16)
    ke_ref[...] = (jnp.dot(pieces, selk_ref[...], preferred_element_type=F32)
                   + onek_ref[...]).astype(BF16)


def _kvf(h, g, wkv, wf, bf, batch, seq):
    ns = seq // TM
    tri = (jnp.arange(TM)[:, None] >= jnp.arange(TM)[None, :]).astype(BF16)
    half = EXT_PER_HEAD // 2
    src = jnp.arange(3 * EXT_W)[:, None]
    dst = jnp.arange(EXT_W)[None, :]
    src_head, src_piece = src % EXT_W, src // EXT_W
    dst_head, dst_slot = dst // EXT_PER_HEAD, dst % EXT_PER_HEAD
    same_head = (src_head == dst_head) & (src_head < N_HEADS)
    sel_q = (same_head & (dst_slot == half + src_piece)).astype(BF16)
    sel_k = -(same_head & (dst_slot == src_piece)).astype(BF16)
    live = jnp.arange(EXT_W) < N_HEADS * EXT_PER_HEAD
    slot = jnp.arange(EXT_W) % EXT_PER_HEAD
    one_q = (live & (slot < half)).astype(F32).reshape(1, EXT_W)
    one_k = (live & (slot >= half)).astype(F32).reshape(1, EXT_W)

    kv_sds = jax.ShapeDtypeStruct((batch, N_PAIRS, seq, PAIR_W), BF16)
    kv_spec = pl.BlockSpec((None, N_PAIRS, TM, PAIR_W), lambda b, i: (b, 0, i, 0))
    ext_sds = jax.ShapeDtypeStruct((batch, seq, EXT_W), BF16)
    ext_spec = pl.BlockSpec((None, TM, EXT_W), lambda b, i: (b, i, 0))
    return pl.pallas_call(
        _kvf_kernel,
        out_shape=[kv_sds, kv_sds, ext_sds, ext_sds],
        grid=(batch, ns),
        in_specs=[
            pl.BlockSpec((TM, D_MODEL), lambda b, i: (b * ns + i, 0)),
            _resident((1, D_MODEL), lambda b, i: (0, 0)),
            _resident((D_MODEL, 2 * D_MODEL), lambda b, i: (0, 0)),
            _resident((D_MODEL, EXT_W), lambda b, i: (0, 0)),
            _resident((1, EXT_W), lambda b, i: (0, 0)),
            _resident((TM, TM), lambda b, i: (0, 0)),
            _resident((3 * EXT_W, EXT_W), lambda b, i: (0, 0)),
            _resident((3 * EXT_W, EXT_W), lambda b, i: (0, 0)),
            _resident((1, EXT_W), lambda b, i: (0, 0)),
            _resident((1, EXT_W), lambda b, i: (0, 0)),
        ],
        out_specs=[kv_spec, kv_spec, ext_spec, ext_spec],
        scratch_shapes=[pltpu.VMEM((1, EXT_W), F32)],
        compiler_params=_params("parallel", "arbitrary"),
        name="kvf",
    )(h, g, wkv, wf, bf, tri, sel_q, sel_k, one_q, one_k)


def _fox_kernel(q_ref, k_ref, v_ref, qe_ref, ke_ref, o_ref,
                qa_ref, s_ref, p_ref, m_ref, l_ref, alpha_ref, acc_ref):
    group = pl.program_id(1)
    i = pl.program_id(2)
    heads = [(pp, hh) for pp in range(FOX_PAIRS) for hh in range(2)]
    qe = qe_ref[...]
    ext_lane = lax.broadcasted_iota(jnp.int32, qe.shape, 1)
    for slot, (pp, hh) in enumerate(heads):
        q = q_ref[pp]
        head = 2 * (group * FOX_PAIRS + pp) + hh
        qm = jnp.where(_head_lane_mask(q.shape, hh), q, jnp.zeros_like(q))
        own = (ext_lane >= EXT_PER_HEAD * head) & (ext_lane < EXT_PER_HEAD * (head + 1))
        qa_ref[slot] = jnp.concatenate([qm, jnp.where(own, qe, jnp.zeros_like(qe))], axis=1)
    m_ref[...] = jnp.full(m_ref.shape, NEG_INF, F32)
    l_ref[...] = jnp.zeros(l_ref.shape, F32)
    acc_ref[...] = jnp.zeros(acc_ref.shape, F32)

    def key_block(off, diagonal):
        k_ext = ke_ref[pl.ds(off, QB), :]
        for slot, (pp, hh) in enumerate(heads):
            ka = jnp.concatenate([k_ref[pp, pl.ds(off, QB), :], k_ext], axis=1)
            s_ref[slot] = _dot_nt(qa_ref[slot], ka)
        n_chunks = QB // LANES

        def live_chunks(r):
            if not diagonal:
                return list(range(n_chunks))
            return [c for c in range(n_chunks) if c * LANES <= r + STRIP - 1]

        for slot in range(len(heads)):
            for r in range(0, QB, STRIP):
                rows = slice(r, r + STRIP)
                t = _lane_chunks(s_ref, (slot,), rows, QB)
                live = live_chunks(r)
                if diagonal:
                    row_id = r + lax.broadcasted_iota(jnp.int32, (STRIP, LANES), 0)
                    col_id = lax.broadcasted_iota(jnp.int32, (STRIP, LANES), 1)
                    for c in live:
                        if (c + 1) * LANES - 1 > r:
                            t[c] = jnp.where(row_id >= col_id + c * LANES, t[c], NEG_INF)
                            s_ref[slot, rows, c * LANES:(c + 1) * LANES] = t[c]
                m_old = m_ref[slot, rows, :]
                blk_max = functools.reduce(jnp.maximum, [t[c] for c in live])
                m_new = jnp.maximum(m_old, blk_max.max(axis=1, keepdims=True))
                m_ref[slot, rows, :] = m_new
                alpha_ref[slot, rows, :] = jnp.exp2(m_old - m_new)
        for slot in range(len(heads)):
            for r in range(0, QB, STRIP):
                rows = slice(r, r + STRIP)
                t = _lane_chunks(s_ref, (slot,), rows, QB)
                live = live_chunks(r)
                m_new = m_ref[slot, rows, :]
                tot = jnp.zeros((STRIP, LANES), F32)
                for c in range(n_chunks):
                    cols = slice(c * LANES, (c + 1) * LANES)
                    if c in live:
                        pr = jnp.exp2(t[c] - m_new)
                        tot = tot + pr
                        p_ref[slot, rows, cols] = pr.astype(BF16)
                    else:
                        p_ref[slot, rows, cols] = jnp.zeros((STRIP, LANES), BF16)
                l_ref[slot, rows, :] = (alpha_ref[slot, rows, :] * l_ref[slot, rows, :]
                                        + tot.sum(axis=1, keepdims=True))
        for slot, (pp, hh) in enumerate(heads):
            acc_ref[slot] = alpha_ref[slot] * acc_ref[slot] + jnp.dot(
                p_ref[slot], v_ref[pp, pl.ds(off, QB), :], preferred_element_type=F32)

    def body(j, carry):
        key_block(pl.multiple_of(j * QB, QB), False)
        return carry

    lax.fori_loop(0, i, body, 0)
    key_block(pl.multiple_of(i * QB, QB), True)
    for pp in range(FOX_PAIRS):
        outs = [acc_ref[2 * pp + hh] / l_ref[2 * pp + hh] for hh in range(2)]
        o_ref[pp] = jnp.where(_head_lane_mask(outs[0].shape, 0), outs[0], outs[1]).astype(BF16)


def _fox(q, k, v, q_ext, k_ext):
    batch, _, seq, _ = q.shape
    nq = seq // QB
    n_heads = 2 * FOX_PAIRS
    return pl.pallas_call(
        _fox_kernel,
        out_shape=jax.ShapeDtypeStruct(q.shape, BF16),
        grid=(batch, N_PAIRS // FOX_PAIRS, nq),
        in_specs=[
            pl.BlockSpec((None, FOX_PAIRS, QB, PAIR_W), lambda b, p, i: (b, p, i, 0)),
            pl.BlockSpec((None, FOX_PAIRS, seq, PAIR_W), lambda b, p, i: (b, p, 0, 0)),
            pl.BlockSpec((None, FOX_PAIRS, seq, PAIR_W), lambda b, p, i: (b, p, 0, 0)),
            pl.BlockSpec((None, QB, EXT_W), lambda b, p, i: (b, i, 0)),
            pl.BlockSpec((None, seq, EXT_W), lambda b, p, i: (b, 0, 0)),
        ],
        out_specs=pl.BlockSpec((None, FOX_PAIRS, QB, PAIR_W), lambda b, p, i: (b, p, i, 0)),
        scratch_shapes=[
            pltpu.VMEM((n_heads, QB, PAIR_W + EXT_W), BF16),
            pltpu.VMEM((n_heads, QB, QB), F32),
            pltpu.VMEM((n_heads, QB, QB), BF16),
            pltpu.VMEM((n_heads, QB, LANES), F32),
            pltpu.VMEM((n_heads, QB, LANES), F32),
            pltpu.VMEM((n_heads, QB, LANES), F32),
            pltpu.VMEM((n_heads, QB, PAIR_W), F32),
        ],
        compiler_params=_params("parallel", "parallel", "arbitrary"),
        name="fox",
    )(q, k, v, q_ext, k_ext)


def kernel(x, ffn_norm, ffn_w_gate, ffn_w_up, ffn_w_down, mix_norm, a_w_qkv, a_w_o, a_rel_bias,
           kv_norm, b_w_kvf, b_f_bias, b_w_q, b_w_o, final_norm):
    batch, seq, _ = x.shape
    depth = ffn_norm.shape[0]
    n_a = a_w_qkv.shape[0]
    n_b = b_w_q.shape[0]
    assert seq % TM == 0 and seq % QA == 0 and seq % QB == 0 and n_a + n_b == depth

    wg = ffn_w_gate.astype(BF16)
    wu = ffn_w_up.astype(BF16)
    wd = ffn_w_down.astype(BF16)
    gf = final_norm.reshape(1, D_MODEL)

    def half_ffn(h, layer, pos, final=False):
        return _ffn(h, ffn_norm[layer, pos].reshape(1, D_MODEL), wg[layer, pos], wu[layer, pos],
                    wd[layer, pos], gf, final)

    h = x.reshape(batch * seq, D_MODEL)
    for layer in range(n_a):
        h = half_ffn(h, layer, 0)
        q, k, v = _proj(h, mix_norm[layer].reshape(1, D_MODEL), a_w_qkv[layer].astype(BF16),
                        batch, seq, True, "qkv_proj")
        o = _attn_a(q, k, v, _attn_a_bias_table(a_rel_bias[layer]))
        h = _oproj(o, a_w_o[layer].astype(BF16), h, batch, seq)
        h = half_ffn(h, layer, 1)

    wkv = b_w_kvf[:, :2 * D_MODEL].astype(BF16)
    wf = jnp.pad(b_w_kvf[:, 2 * D_MODEL:], ((0, 0), (0, EXT_W - N_HEADS))).astype(BF16)
    bf = jnp.pad(b_f_bias, (0, EXT_W - N_HEADS)).reshape(1, EXT_W).astype(F32)
    k_sh, v_sh, q_ext, k_ext = _kvf(h, kv_norm.reshape(1, D_MODEL), wkv, wf, bf, batch, seq)

    for lb in range(n_b):
        layer = n_a + lb
        h = half_ffn(h, layer, 0)
        (q,) = _proj(h, mix_norm[layer].reshape(1, D_MODEL), b_w_q[lb].astype(BF16),
                     batch, seq, True, "q_proj")
        o = _fox(q, k_sh, v_sh, q_ext, k_ext)
        h = _oproj(o, b_w_o[lb].astype(BF16), h, batch, seq)
        h = half_ffn(h, layer, 1, final=(layer == depth - 1))

    return h.reshape(batch, seq, D_MODEL)
```

```python
import functools

import jax
import jax.numpy as jnp
from jax import lax
from jax.experimental import pallas as pl
from jax.experimental.pallas import tpu as pltpu

D_MODEL = 1024
N_HEADS = 16
HEAD_DIM = 64
N_PAIRS = N_HEADS // 2
PAIR_W = 2 * HEAD_DIM
D_FF = 2816
CHUNK = 64
LEFT_CHUNKS = 8
REL_CLIP = 256
EPS = 1e-6
NEG_INF = -1e30
ATTN_SCALE = HEAD_DIM ** -0.5
LOG2E = 1.4426950408889634
Q_SCALE = ATTN_SCALE * LOG2E
FFN_RES_WEIGHT = 0.5

TM = 512
FF_CHUNK = 256
QA = 256
A_KBLOCKS = LEFT_CHUNKS * CHUNK // QA + 1
A_PAIRS = 4
QB = 512
FOX_PAIRS = 4
EXT_W = 128
EXT_PER_HEAD = 6
LANES = 128
STRIP = 32
VMEM_LIMIT = 56 * 1024 * 1024

F32 = jnp.float32
BF16 = jnp.bfloat16


def _rms_norm(x, g):
    y = x * lax.rsqrt(jnp.mean(x * x, axis=-1, keepdims=True) + EPS)
    return y * g


def _resident(block_shape, index_map):
    return pl.BlockSpec(block_shape, index_map, pipeline_mode=pl.Buffered(1))


def _params(*semantics):
    return pltpu.CompilerParams(dimension_semantics=semantics, vmem_limit_bytes=VMEM_LIMIT)


def _ffn_kernel(x_ref, g_ref, wg_ref, wu_ref, wd_ref, gf_ref, o_ref, a_ref, *, final):
    x = x_ref[...]
    hn = _rms_norm(x, g_ref[...]).astype(BF16)
    for c in range(D_FF // FF_CHUNK):
        sl = slice(c * FF_CHUNK, (c + 1) * FF_CHUNK)
        gate = jnp.dot(hn, wg_ref[:, sl], preferred_element_type=F32)
        up = jnp.dot(hn, wu_ref[:, sl], preferred_element_type=F32)
        a_ref[:, sl] = (gate * jax.nn.sigmoid(gate) * up).astype(BF16)
    y = jnp.dot(a_ref[...], wd_ref[...], preferred_element_type=F32)
    out = x + FFN_RES_WEIGHT * y
    if final:
        out = _rms_norm(out, gf_ref[...])
    o_ref[...] = out


def _ffn(h, g, wg, wu, wd, gf, layer, pos, final):
    t = h.shape[0]
    return pl.pallas_call(
        functools.partial(_ffn_kernel, final=final),
        out_shape=jax.ShapeDtypeStruct((t, D_MODEL), F32),
        grid=(t // TM,),
        in_specs=[
            pl.BlockSpec((TM, D_MODEL), lambda i: (i, 0)),
            _resident((1, D_MODEL), lambda i: (0, 0)),
            _resident((None, None, D_MODEL, D_FF), lambda i: (layer, pos, 0, 0)),
            _resident((None, None, D_MODEL, D_FF), lambda i: (layer, pos, 0, 0)),
            _resident((None, None, D_FF, D_MODEL), lambda i: (layer, pos, 0, 0)),
            _resident((1, D_MODEL), lambda i: (0, 0)),
        ],
        out_specs=pl.BlockSpec((TM, D_MODEL), lambda i: (i, 0)),
        scratch_shapes=[pltpu.VMEM((TM, D_FF), BF16)],
        compiler_params=_params("parallel"),
        name="ffn_final" if final else "ffn",
    )(h, g, wg, wu, wd, gf)


def _proj_kernel(x_ref, g_ref, w_ref, *o_refs, scale_first):
    hn = _rms_norm(x_ref[...], g_ref[...]).astype(BF16)
    for n, o_ref in enumerate(o_refs):
        y = jnp.dot(hn, w_ref[:, n * D_MODEL:(n + 1) * D_MODEL], preferred_element_type=F32)
        if scale_first and n == 0:
            y = y * Q_SCALE
        for p in range(N_PAIRS):
            o_ref[p] = y[:, p * PAIR_W:(p + 1) * PAIR_W].astype(BF16)


def _proj(h, g, w, batch, seq, scale_first, name):
    n_out = w.shape[1] // D_MODEL
    ns = seq // TM
    out_sds = jax.ShapeDtypeStruct((batch, N_PAIRS, seq, PAIR_W), BF16)
    out_spec = pl.BlockSpec((None, N_PAIRS, TM, PAIR_W), lambda b, i: (b, 0, i, 0))
    return pl.pallas_call(
        functools.partial(_proj_kernel, scale_first=scale_first),
        out_shape=[out_sds] * n_out,
        grid=(batch, ns),
        in_specs=[
            pl.BlockSpec((TM, D_MODEL), lambda b, i: (b * ns + i, 0)),
            _resident((1, D_MODEL), lambda b, i: (0, 0)),
            _resident((D_MODEL, n_out * D_MODEL), lambda b, i: (0, 0)),
        ],
        out_specs=[out_spec] * n_out,
        compiler_params=_params("parallel", "parallel"),
        name=name,
    )(h, g, w)


def _oproj_kernel(o_ref, w_ref, h_ref, out_ref):
    o = jnp.concatenate([o_ref[p] for p in range(N_PAIRS)], axis=1)
    out_ref[...] = h_ref[...] + jnp.dot(o, w_ref[...], preferred_element_type=F32)


def _oproj(o, w, h, batch, seq):
    ns = seq // TM
    return pl.pallas_call(
        _oproj_kernel,
        out_shape=jax.ShapeDtypeStruct(h.shape, F32),
        grid=(batch, ns),
        in_specs=[
            pl.BlockSpec((None, N_PAIRS, TM, PAIR_W), lambda b, i: (b, 0, i, 0)),
            _resident((D_MODEL, D_MODEL), lambda b, i: (0, 0)),
            pl.BlockSpec((TM, D_MODEL), lambda b, i: (b * ns + i, 0)),
        ],
        out_specs=pl.BlockSpec((TM, D_MODEL), lambda b, i: (b * ns + i, 0)),
        compiler_params=_params("parallel", "parallel"),
        name="oproj",
    )(o, w, h)


def _head_lane_mask(shape, head_in_pair):
    lane = lax.broadcasted_iota(jnp.int32, shape, 1)
    return (lane >= HEAD_DIM * head_in_pair) & (lane < HEAD_DIM * (head_in_pair + 1))


def _dot_nt(a, b):
    return lax.dot_general(a, b, (((1,), (1,)), ((), ())), preferred_element_type=F32)


def _lane_chunks(ref, lead, rows, width):
    return [ref[(*lead, rows, slice(c * LANES, (c + 1) * LANES))] for c in range(width // LANES)]


def _attn_a_kernel(q_ref, *refs):
    k_refs = refs[:A_KBLOCKS]
    v_refs = refs[A_KBLOCKS:2 * A_KBLOCKS]
    bias_ref, o_ref, s_ref, p_ref, m_ref, l_ref = refs[2 * A_KBLOCKS:]
    i = pl.program_id(2)
    pens = [jnp.where(i - (A_KBLOCKS - 1) + j >= 0, 0.0, NEG_INF).astype(F32)
            for j in range(A_KBLOCKS)]
    heads = [(pp, hh) for pp in range(A_PAIRS) for hh in range(2)]
    strips = [slice(r, r + STRIP) for r in range(0, QA, STRIP)]
    for pp, hh in heads:
        q = q_ref[pp]
        qm = jnp.where(_head_lane_mask(q.shape, hh), q, jnp.zeros_like(q))
        for j in range(A_KBLOCKS):
            s_ref[pp, hh, j] = _dot_nt(qm, k_refs[j][pp])
    for pp, hh in heads:
        for rows in strips:
            blk_max = []
            for j in range(A_KBLOCKS):
                t = [s + b for s, b in zip(_lane_chunks(s_ref, (pp, hh, j), rows, QA),
                                           _lane_chunks(bias_ref, (2 * pp + hh, j), rows, QA))]
                for c, tc in enumerate(t):
                    s_ref[pp, hh, j, rows, c * LANES:(c + 1) * LANES] = tc
                blk_max.append(functools.reduce(jnp.maximum, t) + pens[j])
            m = functools.reduce(jnp.maximum, blk_max).max(axis=1, keepdims=True)
            m_ref[pp, hh, rows, :] = jnp.broadcast_to(m, (STRIP, LANES))
    for pp, hh in heads:
        for rows in strips:
            m = m_ref[pp, hh, rows, :]
            tot = jnp.zeros((STRIP, LANES), F32)
            for j in range(A_KBLOCKS):
                shift = m - pens[j]
                for c, tc in enumerate(_lane_chunks(s_ref, (pp, hh, j), rows, QA)):
                    pr = jnp.exp2(tc - shift)
                    tot = tot + pr
                    p_ref[pp, hh, j, rows, c * LANES:(c + 1) * LANES] = pr.astype(BF16)
            l_ref[pp, hh, rows, :] = jnp.broadcast_to(
                tot.sum(axis=1, keepdims=True), (STRIP, LANES))
    for pp in range(A_PAIRS):
        outs = []
        for hh in range(2):
            acc = sum(jnp.dot(p_ref[pp, hh, j], v_refs[j][pp], preferred_element_type=F32)
                      for j in range(A_KBLOCKS))
            outs.append(acc / l_ref[pp, hh])
        o_ref[pp] = jnp.where(_head_lane_mask(outs[0].shape, 0), outs[0], outs[1]).astype(BF16)


def _attn_a(q, k, v, bias):
    batch, _, seq, _ = q.shape
    nq = seq // QA

    def kv_spec(j):
        back = A_KBLOCKS - 1 - j
        return pl.BlockSpec((None, A_PAIRS, QA, PAIR_W),
                            lambda b, p, i: (b, p, jnp.maximum(i - back, 0), 0))

    return pl.pallas_call(
        _attn_a_kernel,
        out_shape=jax.ShapeDtypeStruct(q.shape, BF16),
        grid=(batch, N_PAIRS // A_PAIRS, nq),
        in_specs=(
            [pl.BlockSpec((None, A_PAIRS, QA, PAIR_W), lambda b, p, i: (b, p, i, 0))]
            + [kv_spec(j) for j in range(A_KBLOCKS)]
            + [kv_spec(j) for j in range(A_KBLOCKS)]
            + [pl.BlockSpec((2 * A_PAIRS, A_KBLOCKS, QA, QA), lambda b, p, i: (p, 0, 0, 0))]
        ),
        out_specs=pl.BlockSpec((None, A_PAIRS, QA, PAIR_W), lambda b, p, i: (b, p, i, 0)),
        scratch_shapes=[
            pltpu.VMEM((A_PAIRS, 2, A_KBLOCKS, QA, QA), F32),
            pltpu.VMEM((A_PAIRS, 2, A_KBLOCKS, QA, QA), BF16),
            pltpu.VMEM((A_PAIRS, 2, QA, LANES), F32),
            pltpu.VMEM((A_PAIRS, 2, QA, LANES), F32),
        ],
        compiler_params=_params("parallel", "parallel", "arbitrary"),
        name="attn_a",
    )(q, *([k] * A_KBLOCKS), *([v] * A_KBLOCKS), bias)


def _attn_a_bias_table(rel_bias):
    n_heads = rel_bias.shape[0]
    back = (A_KBLOCKS - 1) * QA
    keys = A_KBLOCKS * QA
    d = back + (QA - 1) - jnp.arange(keys + QA - 1)
    f = jnp.take(rel_bias.astype(F32), jnp.clip(d, -REL_CLIP, REL_CLIP) + REL_CLIP, axis=1)
    period = keys + QA
    g = jnp.pad(f, ((0, 0), (0, period - f.shape[1])))
    skew = jnp.tile(g, (1, QA))[:, :QA * (period - 1)].reshape(n_heads, QA, period - 1)
    toeplitz = skew[:, :, QA - 1:QA - 1 + keys]
    r = jnp.arange(QA)[:, None]
    x = jnp.arange(keys)[None, :]
    dchunk = (r + back) // CHUNK - x // CHUNK
    valid = (dchunk >= 0) & (dchunk <= LEFT_CHUNKS)
    table = jnp.where(valid[None], toeplitz * LOG2E, NEG_INF)
    return table.reshape(n_heads, QA, A_KBLOCKS, QA).transpose(0, 2, 1, 3)


def _split3(x):
    hi = x.astype(BF16)
    r1 = x - hi.astype(F32)
    mid = r1.astype(BF16)
    lo = (r1 - mid.astype(F32)).astype(BF16)
    return hi, mid, lo


def _kvf_kernel(x_ref, g_ref, wkv_ref, wf_ref, bf_ref, tri_ref, selq_ref, selk_ref, oneq_ref,
                onek_ref, k_ref, v_ref, qe_ref, ke_ref, carry_ref):
    @pl.when(pl.program_id(1) == 0)
    def _():
        carry_ref[...] = jnp.zeros_like(carry_ref)

    hn = _rms_norm(x_ref[...], g_ref[...]).astype(BF16)
    for n, o_ref in enumerate((k_ref, v_ref)):
        y = jnp.dot(hn, wkv_ref[:, n * D_MODEL:(n + 1) * D_MODEL], preferred_element_type=F32)
        for p in range(N_PAIRS):
            o_ref[p] = y[:, p * PAIR_W:(p + 1) * PAIR_W].astype(BF16)
    z = jnp.dot(hn, wf_ref[...], preferred_element_type=F32) + bf_ref[...]
    log_f = jnp.minimum(z, 0.0) - jnp.log1p(jnp.exp(-jnp.abs(z)))
    tri = tri_ref[...]
    cum = sum(jnp.dot(tri, piece, preferred_element_type=F32) for piece in _split3(log_f))
    cum = cum + carry_ref[...]
    carry_ref[...] = cum[TM - 1:TM, :]
    pieces = jnp.concatenate(_split3(cum * LOG2E), axis=1)
    qe_ref[...] = (jnp.dot(pieces, selq_ref[...], preferred_element_type=F32)
                   + oneq_ref[...]).astype(BF16)
    ke_ref[...] = (jnp.dot(pieces, selk_ref[...], preferred_element_type=F32)
                   + onek_ref[...]).astype(BF16)


def _kvf(h, g, wkv, wf, bf, batch, seq):
    ns = seq // TM
    tri = (jnp.arange(TM)[:, None] >= jnp.arange(TM)[None, :]).astype(BF16)
    half = EXT_PER_HEAD // 2
    src = jnp.arange(3 * EXT_W)[:, None]
    dst = jnp.arange(EXT_W)[None, :]
    src_head, src_piece = src % EXT_W, src // EXT_W
    dst_head, dst_slot = dst // EXT_PER_HEAD, dst % EXT_PER_HEAD
    same_head = (src_head == dst_head) & (src_head < N_HEADS)
    sel_q = (same_head & (dst_slot == half + src_piece)).astype(BF16)
    sel_k = -(same_head & (dst_slot == src_piece)).astype(BF16)
    live = jnp.arange(EXT_W) < N_HEADS * EXT_PER_HEAD
    slot = jnp.arange(EXT_W) % EXT_PER_HEAD
    one_q = (live & (slot < half)).astype(F32).reshape(1, EXT_W)
    one_k = (live & (slot >= half)).astype(F32).reshape(1, EXT_W)

    kv_sds = jax.ShapeDtypeStruct((batch, N_PAIRS, seq, PAIR_W), BF16)
    kv_spec = pl.BlockSpec((None, N_PAIRS, TM, PAIR_W), lambda b, i: (b, 0, i, 0))
    ext_sds = jax.ShapeDtypeStruct((batch, seq, EXT_W), BF16)
    ext_spec = pl.BlockSpec((None, TM, EXT_W), lambda b, i: (b, i, 0))
    return pl.pallas_call(
        _kvf_kernel,
        out_shape=[kv_sds, kv_sds, ext_sds, ext_sds],
        grid=(batch, ns),
        in_specs=[
            pl.BlockSpec((TM, D_MODEL), lambda b, i: (b * ns + i, 0)),
            _resident((1, D_MODEL), lambda b, i: (0, 0)),
            _resident((D_MODEL, 2 * D_MODEL), lambda b, i: (0, 0)),
            _resident((D_MODEL, EXT_W), lambda b, i: (0, 0)),
            _resident((1, EXT_W), lambda b, i: (0, 0)),
            _resident((TM, TM), lambda b, i: (0, 0)),
            _resident((3 * EXT_W, EXT_W), lambda b, i: (0, 0)),
            _resident((3 * EXT_W, EXT_W), lambda b, i: (0, 0)),
            _resident((1, EXT_W), lambda b, i: (0, 0)),
            _resident((1, EXT_W), lambda b, i: (0, 0)),
        ],
        out_specs=[kv_spec, kv_spec, ext_spec, ext_spec],
        scratch_shapes=[pltpu.VMEM((1, EXT_W), F32)],
        compiler_params=_params("parallel", "arbitrary"),
        name="kvf",
    )(h, g, wkv, wf, bf, tri, sel_q, sel_k, one_q, one_k)


def _fox_kernel(q_ref, k_ref, v_ref, qe_ref, ke_ref, o_ref,
                qa_ref, s_ref, p_ref, m_ref, l_ref, alpha_ref, acc_ref):
    group = pl.program_id(1)
    i = pl.program_id(2)
    heads = [(pp, hh) for pp in range(FOX_PAIRS) for hh in range(2)]
    qe = qe_ref[...]
    ext_lane = lax.broadcasted_iota(jnp.int32, qe.shape, 1)
    for slot, (pp, hh) in enumerate(heads):
        q = q_ref[pp]
        head = 2 * (group * FOX_PAIRS + pp) + hh
        qm = jnp.where(_head_lane_mask(q.shape, hh), q, jnp.zeros_like(q))
        own = (ext_lane >= EXT_PER_HEAD * head) & (ext_lane < EXT_PER_HEAD * (head + 1))
        qa_ref[slot] = jnp.concatenate([qm, jnp.where(own, qe, jnp.zeros_like(qe))], axis=1)

    half = QB // 2

    def key_block(off, diagonal):
        k_ext = ke_ref[pl.ds(off, QB), :]
        for slot, (pp, hh) in enumerate(heads):
            ka = jnp.concatenate([k_ref[pp, pl.ds(off, QB), :], k_ext], axis=1)
            if diagonal:
                s_ref[slot, :half, :half] = _dot_nt(qa_ref[slot, :half, :], ka[:half])
                s_ref[slot, half:, :] = _dot_nt(qa_ref[slot, half:, :], ka)
            else:
                s_ref[slot] = _dot_nt(qa_ref[slot], ka)
        n_chunks = QB // LANES

        def live_chunks(r):
            if not diagonal:
                return list(range(n_chunks))
            return [c for c in range(n_chunks) if c * LANES <= r + STRIP - 1]

        for slot in range(len(heads)):
            for r in range(0, QB, STRIP):
                rows = slice(r, r + STRIP)
                live = live_chunks(r)
                t = {c: s_ref[slot, rows, c * LANES:(c + 1) * LANES] for c in live}
                if diagonal:
                    row_id = r + lax.broadcasted_iota(jnp.int32, (STRIP, LANES), 0)
                    col_id = lax.broadcasted_iota(jnp.int32, (STRIP, LANES), 1)
                    for c in live:
                        if (c + 1) * LANES - 1 > r:
                            t[c] = jnp.where(row_id >= col_id + c * LANES, t[c], NEG_INF)
                            s_ref[slot, rows, c * LANES:(c + 1) * LANES] = t[c]
                blk_max = functools.reduce(jnp.maximum, [t[c] for c in live])
                blk_max = blk_max.max(axis=1, keepdims=True)
                if diagonal:
                    m_ref[slot, rows, :] = jnp.broadcast_to(blk_max, (STRIP, LANES))
                else:
                    m_old = m_ref[slot, rows, :]
                    m_new = jnp.maximum(m_old, blk_max)
                    m_ref[slot, rows, :] = m_new
                    alpha_ref[slot, rows, :] = jnp.exp2(m_old - m_new)
        for slot in range(len(heads)):
            for r in range(0, QB, STRIP):
                rows = slice(r, r + STRIP)
                live = live_chunks(r)
                t = {c: s_ref[slot, rows, c * LANES:(c + 1) * LANES] for c in live}
                m_new = m_ref[slot, rows, :]
                tot = jnp.zeros((STRIP, LANES), F32)
                for c in range(n_chunks):
                    cols = slice(c * LANES, (c + 1) * LANES)
                    if c in live:
                        pr = jnp.exp2(t[c] - m_new)
                        tot = tot + pr
                        p_ref[slot, rows, cols] = pr.astype(BF16)
                    elif r >= half or c * LANES < half:
                        p_ref[slot, rows, cols] = jnp.zeros((STRIP, LANES), BF16)
                row_sum = tot.sum(axis=1, keepdims=True)
                if diagonal:
                    l_ref[slot, rows, :] = jnp.broadcast_to(row_sum, (STRIP, LANES))
                else:
                    l_ref[slot, rows, :] = (alpha_ref[slot, rows, :] * l_ref[slot, rows, :]
                                            + row_sum)
        for slot, (pp, hh) in enumerate(heads):
            v_blk = v_ref[pp, pl.ds(off, QB), :]
            if diagonal:
                acc_ref[slot, :half] = jnp.dot(
                    p_ref[slot, :half, :half], v_blk[:half], preferred_element_type=F32)
                acc_ref[slot, half:] = jnp.dot(
                    p_ref[slot, half:, :], v_blk, preferred_element_type=F32)
            else:
                acc_ref[slot] = alpha_ref[slot] * acc_ref[slot] + jnp.dot(
                    p_ref[slot], v_blk, preferred_element_type=F32)

    def body(j, carry):
        key_block(pl.multiple_of(j * QB, QB), False)
        return carry

    key_block(pl.multiple_of(i * QB, QB), True)
    lax.fori_loop(0, i, body, 0)
    for pp in range(FOX_PAIRS):
        outs = [acc_ref[2 * pp + hh] / l_ref[2 * pp + hh] for hh in range(2)]
        o_ref[pp] = jnp.where(_head_lane_mask(outs[0].shape, 0), outs[0], outs[1]).astype(BF16)


def _fox(q, k, v, q_ext, k_ext):
    batch, _, seq, _ = q.shape
    nq = seq // QB
    n_heads = 2 * FOX_PAIRS
    return pl.pallas_call(
        _fox_kernel,
        out_shape=jax.ShapeDtypeStruct(q.shape, BF16),
        grid=(batch, N_PAIRS // FOX_PAIRS, nq),
        in_specs=[
            pl.BlockSpec((None, FOX_PAIRS, QB, PAIR_W), lambda b, p, i: (b, p, i, 0)),
            pl.BlockSpec((None, FOX_PAIRS, seq, PAIR_W), lambda b, p, i: (b, p, 0, 0)),
            pl.BlockSpec((None, FOX_PAIRS, seq, PAIR_W), lambda b, p, i: (b, p, 0, 0)),
            pl.BlockSpec((None, QB, EXT_W), lambda b, p, i: (b, i, 0)),
            pl.BlockSpec((None, seq, EXT_W), lambda b, p, i: (b, 0, 0)),
        ],
        out_specs=pl.BlockSpec((None, FOX_PAIRS, QB, PAIR_W), lambda b, p, i: (b, p, i, 0)),
        scratch_shapes=[
            pltpu.VMEM((n_heads, QB, PAIR_W + EXT_W), BF16),
            pltpu.VMEM((n_heads, QB, QB), F32),
            pltpu.VMEM((n_heads, QB, QB), BF16),
            pltpu.VMEM((n_heads, QB, LANES), F32),
            pltpu.VMEM((n_heads, QB, LANES), F32),
            pltpu.VMEM((n_heads, QB, LANES), F32),
            pltpu.VMEM((n_heads, QB, PAIR_W), F32),
        ],
        compiler_params=_params("parallel", "parallel", "arbitrary"),
        name="fox",
    )(q, k, v, q_ext, k_ext)


def kernel(x, ffn_norm, ffn_w_gate, ffn_w_up, ffn_w_down, mix_norm, a_w_qkv, a_w_o, a_rel_bias,
           kv_norm, b_w_kvf, b_f_bias, b_w_q, b_w_o, final_norm):
    batch, seq, _ = x.shape
    depth = ffn_norm.shape[0]
    n_a = a_w_qkv.shape[0]
    n_b = b_w_q.shape[0]
    assert seq % TM == 0 and seq % QA == 0 and seq % QB == 0 and n_a + n_b == depth

    wg = ffn_w_gate.astype(BF16)
    wu = ffn_w_up.astype(BF16)
    wd = ffn_w_down.astype(BF16)
    gf = final_norm.reshape(1, D_MODEL)

    def half_ffn(h, layer, pos, final=False):
        return _ffn(h, ffn_norm[layer, pos].reshape(1, D_MODEL), wg, wu, wd, gf, layer, pos, final)

    h = x.reshape(batch * seq, D_MODEL)
    for layer in range(n_a):
        h = half_ffn(h, layer, 0)
        q, k, v = _proj(h, mix_norm[layer].reshape(1, D_MODEL), a_w_qkv[layer].astype(BF16),
                        batch, seq, True, "qkv_proj")
        o = _attn_a(q, k, v, _attn_a_bias_table(a_rel_bias[layer]))
        h = _oproj(o, a_w_o[layer].astype(BF16), h, batch, seq)
        h = half_ffn(h, layer, 1)

    wkv = b_w_kvf[:, :2 * D_MODEL].astype(BF16)
    wf = jnp.pad(b_w_kvf[:, 2 * D_MODEL:], ((0, 0), (0, EXT_W - N_HEADS))).astype(BF16)
    bf = jnp.pad(b_f_bias, (0, EXT_W - N_HEADS)).reshape(1, EXT_W).astype(F32)
    k_sh, v_sh, q_ext, k_ext = _kvf(h, kv_norm.reshape(1, D_MODEL), wkv, wf, bf, batch, seq)

    for lb in range(n_b):
        layer = n_a + lb
        h = half_ffn(h, layer, 0)
        (q,) = _proj(h, mix_norm[layer].reshape(1, D_MODEL), b_w_q[lb].astype(BF16),
                     batch, seq, True, "q_proj")
        o = _fox(q, k_sh, v_sh, q_ext, k_ext)
        h = _oproj(o, b_w_o[lb].astype(BF16), h, batch, seq)
        h = half_ffn(h, layer, 1, final=(layer == depth - 1))

    return h.reshape(batch, seq, D_MODEL)
```

```python
import functools

import jax
import jax.numpy as jnp
from jax import lax
from jax.experimental import pallas as pl
from jax.experimental.pallas import tpu as pltpu

D_MODEL = 1024
N_HEADS = 16
HEAD_DIM = 64
N_PAIRS = N_HEADS // 2
PAIR_W = 2 * HEAD_DIM
D_FF = 2816
CHUNK = 64
LEFT_CHUNKS = 8
REL_CLIP = 256
EPS = 1e-6
NEG_INF = -1e30
ATTN_SCALE = HEAD_DIM ** -0.5
LOG2E = 1.4426950408889634
Q_SCALE = ATTN_SCALE * LOG2E
FFN_RES_WEIGHT = 0.5

TM = 512
FF_CHUNK = 256
QA = 256
A_KBLOCKS = LEFT_CHUNKS * CHUNK // QA + 1
A_PAIRS = 4
QB = 512
FOX_PAIRS = 4
EXT_W = 128
EXT_PER_HEAD = 6
LANES = 128
MXU_TILE = 256
STRIP = 32
KSTRIP = 128
VMEM_LIMIT = 56 * 1024 * 1024

F32 = jnp.float32
BF16 = jnp.bfloat16


def _rms_norm(x, g):
    y = x * lax.rsqrt(jnp.mean(x * x, axis=-1, keepdims=True) + EPS)
    return y * g


def _resident(block_shape, index_map):
    return pl.BlockSpec(block_shape, index_map, pipeline_mode=pl.Buffered(1))


def _params(*semantics):
    return pltpu.CompilerParams(dimension_semantics=semantics, vmem_limit_bytes=VMEM_LIMIT)


def _ffn_kernel(x_ref, g_ref, wg_ref, wu_ref, wd_ref, gf_ref, o_ref, a_ref, *, final):
    x = x_ref[...]
    hn = _rms_norm(x, g_ref[...]).astype(BF16)
    for c in range(D_FF // FF_CHUNK):
        sl = slice(c * FF_CHUNK, (c + 1) * FF_CHUNK)
        gate = jnp.dot(hn, wg_ref[:, sl], preferred_element_type=F32)
        up = jnp.dot(hn, wu_ref[:, sl], preferred_element_type=F32)
        a_ref[:, sl] = (gate * jax.nn.sigmoid(gate) * up).astype(BF16)
    y = jnp.dot(a_ref[...], wd_ref[...], preferred_element_type=F32)
    out = x + FFN_RES_WEIGHT * y
    if final:
        out = _rms_norm(out, gf_ref[...])
    o_ref[...] = out


def _ffn(h, g, wg, wu, wd, gf, layer, pos, final):
    t = h.shape[0]
    return pl.pallas_call(
        functools.partial(_ffn_kernel, final=final),
        out_shape=jax.ShapeDtypeStruct((t, D_MODEL), F32),
        grid=(t // TM,),
        in_specs=[
            pl.BlockSpec((TM, D_MODEL), lambda i: (i, 0)),
            _resident((1, D_MODEL), lambda i: (0, 0)),
            _resident((None, None, D_MODEL, D_FF), lambda i: (layer, pos, 0, 0)),
            _resident((None, None, D_MODEL, D_FF), lambda i: (layer, pos, 0, 0)),
            _resident((None, None, D_FF, D_MODEL), lambda i: (layer, pos, 0, 0)),
            _resident((1, D_MODEL), lambda i: (0, 0)),
        ],
        out_specs=pl.BlockSpec((TM, D_MODEL), lambda i: (i, 0)),
        scratch_shapes=[pltpu.VMEM((TM, D_FF), BF16)],
        compiler_params=_params("parallel"),
        name="ffn_final" if final else "ffn",
    )(h, g, wg, wu, wd, gf)


def _proj_kernel(x_ref, g_ref, w_ref, *o_refs, scale_first):
    hn = _rms_norm(x_ref[...], g_ref[...]).astype(BF16)
    for n, o_ref in enumerate(o_refs):
        y = jnp.dot(hn, w_ref[:, n * D_MODEL:(n + 1) * D_MODEL], preferred_element_type=F32)
        if scale_first and n == 0:
            y = y * Q_SCALE
        for p in range(N_PAIRS):
            o_ref[p] = y[:, p * PAIR_W:(p + 1) * PAIR_W].astype(BF16)


def _proj(h, g, w, batch, seq, scale_first, name):
    n_out = w.shape[1] // D_MODEL
    ns = seq // TM
    out_sds = jax.ShapeDtypeStruct((batch, N_PAIRS, seq, PAIR_W), BF16)
    out_spec = pl.BlockSpec((None, N_PAIRS, TM, PAIR_W), lambda b, i: (b, 0, i, 0))
    return pl.pallas_call(
        functools.partial(_proj_kernel, scale_first=scale_first),
        out_shape=[out_sds] * n_out,
        grid=(batch, ns),
        in_specs=[
            pl.BlockSpec((TM, D_MODEL), lambda b, i: (b * ns + i, 0)),
            _resident((1, D_MODEL), lambda b, i: (0, 0)),
            _resident((D_MODEL, n_out * D_MODEL), lambda b, i: (0, 0)),
        ],
        out_specs=[out_spec] * n_out,
        compiler_params=_params("parallel", "parallel"),
        name=name,
    )(h, g, w)


def _oproj_kernel(o_ref, w_ref, h_ref, out_ref):
    o = jnp.concatenate([o_ref[p] for p in range(N_PAIRS)], axis=1)
    out_ref[...] = h_ref[...] + jnp.dot(o, w_ref[...], preferred_element_type=F32)


def _oproj(o, w, h, batch, seq):
    ns = seq // TM
    return pl.pallas_call(
        _oproj_kernel,
        out_shape=jax.ShapeDtypeStruct(h.shape, F32),
        grid=(batch, ns),
        in_specs=[
            pl.BlockSpec((None, N_PAIRS, TM, PAIR_W), lambda b, i: (b, 0, i, 0)),
            _resident((D_MODEL, D_MODEL), lambda b, i: (0, 0)),
            pl.BlockSpec((TM, D_MODEL), lambda b, i: (b * ns + i, 0)),
        ],
        out_specs=pl.BlockSpec((TM, D_MODEL), lambda b, i: (b * ns + i, 0)),
        compiler_params=_params("parallel", "parallel"),
        name="oproj",
    )(o, w, h)


def _head_lane_mask(shape, head_in_pair):
    lane = lax.broadcasted_iota(jnp.int32, shape, 1)
    return (lane >= HEAD_DIM * head_in_pair) & (lane < HEAD_DIM * (head_in_pair + 1))


def _dot_nt(a, b):
    return lax.dot_general(a, b, (((1,), (1,)), ((), ())), preferred_element_type=F32)


def _lane_chunks(ref, lead, rows, width):
    return [ref[(*lead, rows, slice(c * LANES, (c + 1) * LANES))] for c in range(width // LANES)]


def _attn_a_kernel(q_ref, *refs):
    k_refs = refs[:A_KBLOCKS]
    v_refs = refs[A_KBLOCKS:2 * A_KBLOCKS]
    bias_ref, o_ref, s_ref, p_ref, m_ref, l_ref = refs[2 * A_KBLOCKS:]
    i = pl.program_id(2)
    pens = [jnp.where(i - (A_KBLOCKS - 1) + j >= 0, 0.0, NEG_INF).astype(F32)
            for j in range(A_KBLOCKS)]
    heads = [(pp, hh) for pp in range(A_PAIRS) for hh in range(2)]
    strips = [slice(r, r + STRIP) for r in range(0, QA, STRIP)]
    for pp, hh in heads:
        q = q_ref[pp]
        qm = jnp.where(_head_lane_mask(q.shape, hh), q, jnp.zeros_like(q))
        for j in range(A_KBLOCKS):
            s_ref[pp, hh, j] = _dot_nt(qm, k_refs[j][pp])
    for pp, hh in heads:
        for rows in strips:
            blk_max = []
            for j in range(A_KBLOCKS):
                t = [s + b for s, b in zip(_lane_chunks(s_ref, (pp, hh, j), rows, QA),
                                           _lane_chunks(bias_ref, (2 * pp + hh, j), rows, QA))]
                for c, tc in enumerate(t):
                    s_ref[pp, hh, j, rows, c * LANES:(c + 1) * LANES] = tc
                blk_max.append(functools.reduce(jnp.maximum, t) + pens[j])
            m = functools.reduce(jnp.maximum, blk_max).max(axis=1, keepdims=True)
            m_ref[pp, hh, rows, :] = jnp.broadcast_to(m, (STRIP, LANES))
    for pp, hh in heads:
        for rows in strips:
            m = m_ref[pp, hh, rows, :]
            tot = jnp.zeros((STRIP, LANES), F32)
            for j in range(A_KBLOCKS):
                shift = m - pens[j]
                for c, tc in enumerate(_lane_chunks(s_ref, (pp, hh, j), rows, QA)):
                    pr = jnp.exp2(tc - shift)
                    tot = tot + pr
                    p_ref[pp, hh, j, rows, c * LANES:(c + 1) * LANES] = pr.astype(BF16)
            l_ref[pp, hh, rows, :] = jnp.broadcast_to(
                tot.sum(axis=1, keepdims=True), (STRIP, LANES))
    for pp in range(A_PAIRS):
        outs = []
        for hh in range(2):
            acc = sum(jnp.dot(p_ref[pp, hh, j], v_refs[j][pp], preferred_element_type=F32)
                      for j in range(A_KBLOCKS))
            outs.append(acc / l_ref[pp, hh])
        o_ref[pp] = jnp.where(_head_lane_mask(outs[0].shape, 0), outs[0], outs[1]).astype(BF16)


def _attn_a(q, k, v, bias):
    batch, _, seq, _ = q.shape
    nq = seq // QA

    def kv_spec(j):
        back = A_KBLOCKS - 1 - j
        return pl.BlockSpec((None, A_PAIRS, QA, PAIR_W),
                            lambda b, p, i: (b, p, jnp.maximum(i - back, 0), 0))

    return pl.pallas_call(
        _attn_a_kernel,
        out_shape=jax.ShapeDtypeStruct(q.shape, BF16),
        grid=(batch, N_PAIRS // A_PAIRS, nq),
        in_specs=(
            [pl.BlockSpec((None, A_PAIRS, QA, PAIR_W), lambda b, p, i: (b, p, i, 0))]
            + [kv_spec(j) for j in range(A_KBLOCKS)]
            + [kv_spec(j) for j in range(A_KBLOCKS)]
            + [pl.BlockSpec((2 * A_PAIRS, A_KBLOCKS, QA, QA), lambda b, p, i: (p, 0, 0, 0))]
        ),
        out_specs=pl.BlockSpec((None, A_PAIRS, QA, PAIR_W), lambda b, p, i: (b, p, i, 0)),
        scratch_shapes=[
            pltpu.VMEM((A_PAIRS, 2, A_KBLOCKS, QA, QA), F32),
            pltpu.VMEM((A_PAIRS, 2, A_KBLOCKS, QA, QA), BF16),
            pltpu.VMEM((A_PAIRS, 2, QA, LANES), F32),
            pltpu.VMEM((A_PAIRS, 2, QA, LANES), F32),
        ],
        compiler_params=_params("parallel", "parallel", "arbitrary"),
        name="attn_a",
    )(q, *([k] * A_KBLOCKS), *([v] * A_KBLOCKS), bias)


def _attn_a_bias_table(rel_bias):
    n_heads = rel_bias.shape[0]
    back = (A_KBLOCKS - 1) * QA
    keys = A_KBLOCKS * QA
    d = back + (QA - 1) - jnp.arange(keys + QA - 1)
    f = jnp.take(rel_bias.astype(F32), jnp.clip(d, -REL_CLIP, REL_CLIP) + REL_CLIP, axis=1)
    period = keys + QA
    g = jnp.pad(f, ((0, 0), (0, period - f.shape[1])))
    skew = jnp.tile(g, (1, QA))[:, :QA * (period - 1)].reshape(n_heads, QA, period - 1)
    toeplitz = skew[:, :, QA - 1:QA - 1 + keys]
    r = jnp.arange(QA)[:, None]
    x = jnp.arange(keys)[None, :]
    dchunk = (r + back) // CHUNK - x // CHUNK
    valid = (dchunk >= 0) & (dchunk <= LEFT_CHUNKS)
    table = jnp.where(valid[None], toeplitz * LOG2E, NEG_INF)
    return table.reshape(n_heads, QA, A_KBLOCKS, QA).transpose(0, 2, 1, 3)


def _split3(x):
    hi = x.astype(BF16)
    r1 = x - hi.astype(F32)
    mid = r1.astype(BF16)
    lo = (r1 - mid.astype(F32)).astype(BF16)
    return hi, mid, lo


def _kvf_kernel(x_ref, g_ref, wkv_ref, wf_ref, bf_ref, tri_ref, selq_ref, selk_ref, oneq_ref,
                onek_ref, k_ref, vt_ref, qe_ref, ke_ref, carry_ref):
    @pl.when(pl.program_id(1) == 0)
    def _():
        carry_ref[...] = jnp.zeros_like(carry_ref)

    hn = _rms_norm(x_ref[...], g_ref[...]).astype(BF16)
    yk = jnp.dot(hn, wkv_ref[:, :D_MODEL], preferred_element_type=F32)
    yv = jnp.dot(hn, wkv_ref[:, D_MODEL:], preferred_element_type=F32)
    for p in range(N_PAIRS):
        k_ref[p] = yk[:, p * PAIR_W:(p + 1) * PAIR_W].astype(BF16)
        vt_ref[p] = yv[:, p * PAIR_W:(p + 1) * PAIR_W].T.astype(BF16)
    z = jnp.dot(hn, wf_ref[...], preferred_element_type=F32) + bf_ref[...]
    log_f = jnp.minimum(z, 0.0) - jnp.log1p(jnp.exp(-jnp.abs(z)))
    tri = tri_ref[...]
    cum = sum(jnp.dot(tri, piece, preferred_element_type=F32) for piece in _split3(log_f))
    cum = cum + carry_ref[...]
    carry_ref[...] = cum[TM - 1:TM, :]
    pieces = jnp.concatenate(_split3(cum * LOG2E), axis=1)
    qe_ref[...] = (jnp.dot(pieces, selq_ref[...], preferred_element_type=F32)
                   + oneq_ref[...]).astype(BF16)
    ke_ref[...] = (jnp.dot(pieces, selk_ref[...], preferred_element_type=F32)
                   + onek_ref[...]).astype(BF16)


def _kvf(h, g, wkv, wf, bf, batch, seq):
    ns = seq // TM
    tri = (jnp.arange(TM)[:, None] >= jnp.arange(TM)[None, :]).astype(BF16)
    half = EXT_PER_HEAD // 2
    src = jnp.arange(3 * EXT_W)[:, None]
    dst = jnp.arange(EXT_W)[None, :]
    src_head, src_piece = src % EXT_W, src // EXT_W
    dst_head, dst_slot = dst // EXT_PER_HEAD, dst % EXT_PER_HEAD
    same_head = (src_head == dst_head) & (src_head < N_HEADS)
    sel_q = (same_head & (dst_slot == half + src_piece)).astype(BF16)
    sel_k = -(same_head & (dst_slot == src_piece)).astype(BF16)
    live = jnp.arange(EXT_W) < N_HEADS * EXT_PER_HEAD
    slot = jnp.arange(EXT_W) % EXT_PER_HEAD
    one_q = (live & (slot < half)).astype(F32).reshape(1, EXT_W)
    one_k = (live & (slot >= half)).astype(F32).reshape(1, EXT_W)

    k_sds = jax.ShapeDtypeStruct((batch, N_PAIRS, seq, PAIR_W), BF16)
    k_spec = pl.BlockSpec((None, N_PAIRS, TM, PAIR_W), lambda b, i: (b, 0, i, 0))
    vt_sds = jax.ShapeDtypeStruct((batch, N_PAIRS, ns, PAIR_W, TM), BF16)
    vt_spec = pl.BlockSpec((None, N_PAIRS, None, PAIR_W, TM), lambda b, i: (b, 0, i, 0, 0))
    ext_sds = jax.ShapeDtypeStruct((batch, seq, EXT_W), BF16)
    ext_spec = pl.BlockSpec((None, TM, EXT_W), lambda b, i: (b, i, 0))
    return pl.pallas_call(
        _kvf_kernel,
        out_shape=[k_sds, vt_sds, ext_sds, ext_sds],
        grid=(batch, ns),
        in_specs=[
            pl.BlockSpec((TM, D_MODEL), lambda b, i: (b * ns + i, 0)),
            _resident((1, D_MODEL), lambda b, i: (0, 0)),
            _resident((D_MODEL, 2 * D_MODEL), lambda b, i: (0, 0)),
            _resident((D_MODEL, EXT_W), lambda b, i: (0, 0)),
            _resident((1, EXT_W), lambda b, i: (0, 0)),
            _resident((TM, TM), lambda b, i: (0, 0)),
            _resident((3 * EXT_W, EXT_W), lambda b, i: (0, 0)),
            _resident((3 * EXT_W, EXT_W), lambda b, i: (0, 0)),
            _resident((1, EXT_W), lambda b, i: (0, 0)),
            _resident((1, EXT_W), lambda b, i: (0, 0)),
        ],
        out_specs=[k_spec, vt_spec, ext_spec, ext_spec],
        scratch_shapes=[pltpu.VMEM((1, EXT_W), F32)],
        compiler_params=_params("parallel", "arbitrary"),
        name="kvf",
    )(h, g, wkv, wf, bf, tri, sel_q, sel_k, one_q, one_k)


def _fox_kernel(q_ref, k_ref, vt_ref, qe_ref, ke_ref, o_ref,
                qa_ref, s_ref, m_ref, l_ref, alpha_ref, acc_ref):
    group = pl.program_id(1)
    i = pl.program_id(2)
    heads = [(pp, hh) for pp in range(FOX_PAIRS) for hh in range(2)]
    qe = qe_ref[...]
    ext_lane = lax.broadcasted_iota(jnp.int32, qe.shape, 1)
    for slot, (pp, hh) in enumerate(heads):
        q = q_ref[pp]
        head = 2 * (group * FOX_PAIRS + pp) + hh
        qm = jnp.where(_head_lane_mask(q.shape, hh), q, jnp.zeros_like(q))
        own = (ext_lane >= EXT_PER_HEAD * head) & (ext_lane < EXT_PER_HEAD * (head + 1))
        qa_ref[slot] = jnp.concatenate([qm, jnp.where(own, qe, jnp.zeros_like(qe))], axis=1)

    half = QB // 2

    n_chunks = QB // LANES

    def key_block(blk, diagonal):
        off = pl.multiple_of(blk * QB, QB)
        k_ext = ke_ref[pl.ds(off, QB), :]
        for slot, (pp, hh) in enumerate(heads):
            ka = jnp.concatenate([k_ref[pp, pl.ds(off, QB), :], k_ext], axis=1)
            if diagonal:
                s_ref[slot, :half, :half] = _dot_nt(ka[:half], qa_ref[slot, :half, :])
                s_ref[slot, :, half:] = _dot_nt(ka, qa_ref[slot, half:, :])
            else:
                s_ref[slot] = _dot_nt(ka, qa_ref[slot])

        def live_keys(c):
            return (c + 1) * LANES if diagonal else QB

        for slot in range(len(heads)):
            for c in range(n_chunks):
                cols = slice(c * LANES, (c + 1) * LANES)
                parts = []
                n_full = c * LANES if diagonal else QB
                if n_full:
                    parts.append(s_ref[slot, :n_full, cols].max(axis=0, keepdims=True))
                if diagonal:
                    tile = slice(c * LANES, (c + 1) * LANES)
                    key_id = lax.broadcasted_iota(jnp.int32, (LANES, LANES), 0)
                    qry_id = lax.broadcasted_iota(jnp.int32, (LANES, LANES), 1)
                    x = jnp.where(key_id <= qry_id, s_ref[slot, tile, cols], NEG_INF)
                    s_ref[slot, tile, cols] = x
                    parts.append(x.max(axis=0, keepdims=True))
                blk_max = functools.reduce(jnp.maximum, parts)
                if diagonal:
                    m_ref[slot, :, cols] = blk_max
                else:
                    m_old = m_ref[slot, :, cols]
                    m_new = jnp.maximum(m_old, blk_max)
                    m_ref[slot, :, cols] = m_new
                    alpha_ref[slot, :, cols] = jnp.exp2(m_old - m_new)
        chunks_per_tile = MXU_TILE // LANES
        for slot, (pp, hh) in enumerate(heads):
            vt = vt_ref[pp, blk, hh * HEAD_DIM:(hh + 1) * HEAD_DIM, :]
            for nt in range(QB // MXU_TILE):
                tile_cols = slice(nt * MXU_TILE, (nt + 1) * MXU_TILE)
                chunk_ids = range(nt * chunks_per_tile, (nt + 1) * chunks_per_tile)
                tots = {c: jnp.zeros((8, LANES), F32) for c in chunk_ids}
                n_key_tiles = (nt + 1) if diagonal else QB // MXU_TILE
                pv = None
                for kt in range(n_key_tiles):
                    columns = []
                    for c in chunk_ids:
                        cols = slice(c * LANES, (c + 1) * LANES)
                        m_new = m_ref[slot, :, cols]
                        pieces = []
                        for r in range(kt * MXU_TILE, (kt + 1) * MXU_TILE, KSTRIP):
                            if r < live_keys(c):
                                pr = jnp.exp2(s_ref[slot, r:r + KSTRIP, cols] - m_new)
                                tots[c] = tots[c] + pr.reshape(KSTRIP // 8, 8, LANES).sum(axis=0)
                                pieces.append(pr.astype(BF16))
                            else:
                                pieces.append(jnp.zeros((KSTRIP, LANES), BF16))
                        columns.append(jnp.concatenate(pieces, axis=0))
                    p_tile = jnp.concatenate(columns, axis=1)
                    part = jnp.dot(vt[:, kt * MXU_TILE:(kt + 1) * MXU_TILE], p_tile,
                                   preferred_element_type=F32)
                    pv = part if pv is None else pv + part
                col_sum = jnp.concatenate(
                    [tots[c].sum(axis=0, keepdims=True) for c in chunk_ids], axis=1)
                if diagonal:
                    l_ref[slot, :, tile_cols] = col_sum
                    acc_ref[slot, :, tile_cols] = pv
                else:
                    alpha = alpha_ref[slot, :, tile_cols]
                    l_ref[slot, :, tile_cols] = alpha * l_ref[slot, :, tile_cols] + col_sum
                    acc_ref[slot, :, tile_cols] = alpha * acc_ref[slot, :, tile_cols] + pv

    def body(j, carry):
        key_block(j, False)
        return carry

    key_block(i, True)
    lax.fori_loop(0, i, body, 0)
    for pp in range(FOX_PAIRS):
        o_t = jnp.concatenate(
            [acc_ref[2 * pp + hh] / l_ref[2 * pp + hh] for hh in range(2)], axis=0)
        o_ref[pp] = o_t.T.astype(BF16)


def _fox(q, k, vt, q_ext, k_ext):
    batch, _, seq, _ = q.shape
    nq = seq // QB
    n_heads = 2 * FOX_PAIRS
    assert vt.shape == (batch, N_PAIRS, nq, PAIR_W, QB)
    return pl.pallas_call(
        _fox_kernel,
        out_shape=jax.ShapeDtypeStruct(q.shape, BF16),
        grid=(batch, N_PAIRS // FOX_PAIRS, nq),
        in_specs=[
            pl.BlockSpec((None, FOX_PAIRS, QB, PAIR_W), lambda b, p, i: (b, p, i, 0)),
            pl.BlockSpec((None, FOX_PAIRS, seq, PAIR_W), lambda b, p, i: (b, p, 0, 0)),
            pl.BlockSpec((None, FOX_PAIRS, nq, PAIR_W, QB), lambda b, p, i: (b, p, 0, 0, 0)),
            pl.BlockSpec((None, QB, EXT_W), lambda b, p, i: (b, i, 0)),
            pl.BlockSpec((None, seq, EXT_W), lambda b, p, i: (b, 0, 0)),
        ],
        out_specs=pl.BlockSpec((None, FOX_PAIRS, QB, PAIR_W), lambda b, p, i: (b, p, i, 0)),
        scratch_shapes=[
            pltpu.VMEM((n_heads, QB, PAIR_W + EXT_W), BF16),
            pltpu.VMEM((n_heads, QB, QB), F32),
            pltpu.VMEM((n_heads, 1, QB), F32),
            pltpu.VMEM((n_heads, 1, QB), F32),
            pltpu.VMEM((n_heads, 1, QB), F32),
            pltpu.VMEM((n_heads, HEAD_DIM, QB), F32),
        ],
        compiler_params=_params("parallel", "parallel", "arbitrary"),
        name="fox",
    )(q, k, vt, q_ext, k_ext)


def kernel(x, ffn_norm, ffn_w_gate, ffn_w_up, ffn_w_down, mix_norm, a_w_qkv, a_w_o, a_rel_bias,
           kv_norm, b_w_kvf, b_f_bias, b_w_q, b_w_o, final_norm):
    batch, seq, _ = x.shape
    depth = ffn_norm.shape[0]
    n_a = a_w_qkv.shape[0]
    n_b = b_w_q.shape[0]
    assert seq % TM == 0 and seq % QA == 0 and seq % QB == 0 and n_a + n_b == depth

    wg = ffn_w_gate.astype(BF16)
    wu = ffn_w_up.astype(BF16)
    wd = ffn_w_down.astype(BF16)
    gf = final_norm.reshape(1, D_MODEL)

    def half_ffn(h, layer, pos, final=False):
        return _ffn(h, ffn_norm[layer, pos].reshape(1, D_MODEL), wg, wu, wd, gf, layer, pos, final)

    h = x.reshape(batch * seq, D_MODEL)
    for layer in range(n_a):
        h = half_ffn(h, layer, 0)
        q, k, v = _proj(h, mix_norm[layer].reshape(1, D_MODEL), a_w_qkv[layer].astype(BF16),
                        batch, seq, True, "qkv_proj")
        o = _attn_a(q, k, v, _attn_a_bias_table(a_rel_bias[layer]))
        h = _oproj(o, a_w_o[layer].astype(BF16), h, batch, seq)
        h = half_ffn(h, layer, 1)

    wkv = b_w_kvf[:, :2 * D_MODEL].astype(BF16)
    wf = jnp.pad(b_w_kvf[:, 2 * D_MODEL:], ((0, 0), (0, EXT_W - N_HEADS))).astype(BF16)
    bf = jnp.pad(b_f_bias, (0, EXT_W - N_HEADS)).reshape(1, EXT_W).astype(F32)
    k_sh, v_sh, q_ext, k_ext = _kvf(h, kv_norm.reshape(1, D_MODEL), wkv, wf, bf, batch, seq)

    for lb in range(n_b):
        layer = n_a + lb
        h = half_ffn(h, layer, 0)
        (q,) = _proj(h, mix_norm[layer].reshape(1, D_MODEL), b_w_q[lb].astype(BF16),
                     batch, seq, True, "q_proj")
        o = _fox(q, k_sh, v_sh, q_ext, k_ext)
        h = _oproj(o, b_w_o[lb].astype(BF16), h, batch, seq)
        h = half_ffn(h, layer, 1, final=(layer == depth - 1))

    return h.reshape(batch, seq, D_MODEL)
```

```python
import functools

import jax
import jax.numpy as jnp
from jax import lax
from jax.experimental import pallas as pl
from jax.experimental.pallas import tpu as pltpu

D_MODEL = 1024
N_HEADS = 16
HEAD_DIM = 64
N_PAIRS = N_HEADS // 2
PAIR_W = 2 * HEAD_DIM
D_FF = 2816
CHUNK = 64
LEFT_CHUNKS = 8
REL_CLIP = 256
EPS = 1e-6
NEG_INF = -1e30
ATTN_SCALE = HEAD_DIM ** -0.5
LOG2E = 1.4426950408889634
Q_SCALE = ATTN_SCALE * LOG2E
FFN_RES_WEIGHT = 0.5

TM = 512
FF_CHUNK = 256
QA = 256
A_KBLOCKS = LEFT_CHUNKS * CHUNK // QA + 1
A_PAIRS = N_PAIRS
QB = 512
FOX_PAIRS = N_PAIRS
EXT_W = 128
EXT_PER_HEAD = 6
LANES = 128
MXU_TILE = 256
KSTRIP = 128
ONES_ROWS = 16
VMEM_LIMIT = 56 * 1024 * 1024

F32 = jnp.float32
BF16 = jnp.bfloat16


def _rms_norm(x, g):
    y = x * lax.rsqrt(jnp.mean(x * x, axis=-1, keepdims=True) + EPS)
    return y * g


def _resident(block_shape, index_map):
    return pl.BlockSpec(block_shape, index_map, pipeline_mode=pl.Buffered(1))


def _params(*semantics):
    return pltpu.CompilerParams(dimension_semantics=semantics, vmem_limit_bytes=VMEM_LIMIT)


def _ffn_kernel(x_ref, g_ref, wg_ref, wu_ref, wd_ref, gf_ref, o_ref, a_ref, *, final):
    x = x_ref[...]
    hn = _rms_norm(x, g_ref[...]).astype(BF16)
    for c in range(D_FF // FF_CHUNK):
        sl = slice(c * FF_CHUNK, (c + 1) * FF_CHUNK)
        gate = jnp.dot(hn, wg_ref[:, sl], preferred_element_type=F32)
        up = jnp.dot(hn, wu_ref[:, sl], preferred_element_type=F32)
        a_ref[:, sl] = (gate * jax.nn.sigmoid(gate) * up).astype(BF16)
    y = jnp.dot(a_ref[...], wd_ref[...], preferred_element_type=F32)
    out = x + FFN_RES_WEIGHT * y
    if final:
        out = _rms_norm(out, gf_ref[...])
    o_ref[...] = out


def _ffn(h, g, wg, wu, wd, gf, layer, pos, final):
    t = h.shape[0]
    return pl.pallas_call(
        functools.partial(_ffn_kernel, final=final),
        out_shape=jax.ShapeDtypeStruct((t, D_MODEL), F32),
        grid=(t // TM,),
        in_specs=[
            pl.BlockSpec((TM, D_MODEL), lambda i: (i, 0)),
            _resident((1, D_MODEL), lambda i: (0, 0)),
            _resident((None, None, D_MODEL, D_FF), lambda i: (layer, pos, 0, 0)),
            _resident((None, None, D_MODEL, D_FF), lambda i: (layer, pos, 0, 0)),
            _resident((None, None, D_FF, D_MODEL), lambda i: (layer, pos, 0, 0)),
            _resident((1, D_MODEL), lambda i: (0, 0)),
        ],
        out_specs=pl.BlockSpec((TM, D_MODEL), lambda i: (i, 0)),
        scratch_shapes=[pltpu.VMEM((TM, D_FF), BF16)],
        compiler_params=_params("parallel"),
        name="ffn_final" if final else "ffn",
    )(h, g, wg, wu, wd, gf)


def _proj_kernel(x_ref, g_ref, w_ref, *o_refs, scale_first, transpose_last):
    hn = _rms_norm(x_ref[...], g_ref[...]).astype(BF16)
    for n, o_ref in enumerate(o_refs):
        y = jnp.dot(hn, w_ref[:, n * D_MODEL:(n + 1) * D_MODEL], preferred_element_type=F32)
        if scale_first and n == 0:
            y = y * Q_SCALE
        for p in range(N_PAIRS):
            y_pair = y[:, p * PAIR_W:(p + 1) * PAIR_W]
            if transpose_last and n == len(o_refs) - 1:
                y_pair = y_pair.T
            o_ref[p] = y_pair.astype(BF16)


def _proj(h, g, w, batch, seq, scale_first, name, transpose_last=False):
    n_out = w.shape[1] // D_MODEL
    ns = seq // TM
    out_sds = [jax.ShapeDtypeStruct((batch, N_PAIRS, seq, PAIR_W), BF16)] * n_out
    out_specs = [pl.BlockSpec((None, N_PAIRS, TM, PAIR_W), lambda b, i: (b, 0, i, 0))] * n_out
    if transpose_last:
        out_sds[-1] = jax.ShapeDtypeStruct((batch, N_PAIRS, PAIR_W, seq), BF16)
        out_specs[-1] = pl.BlockSpec((None, N_PAIRS, PAIR_W, TM), lambda b, i: (b, 0, 0, i))
    return pl.pallas_call(
        functools.partial(_proj_kernel, scale_first=scale_first, transpose_last=transpose_last),
        out_shape=out_sds,
        grid=(batch, ns),
        in_specs=[
            pl.BlockSpec((TM, D_MODEL), lambda b, i: (b * ns + i, 0)),
            _resident((1, D_MODEL), lambda b, i: (0, 0)),
            _resident((D_MODEL, n_out * D_MODEL), lambda b, i: (0, 0)),
        ],
        out_specs=out_specs,
        compiler_params=_params("parallel", "parallel"),
        name=name,
    )(h, g, w)


def _oproj_kernel(o_ref, w_ref, h_ref, out_ref):
    o = jnp.concatenate([o_ref[p] for p in range(N_PAIRS)], axis=1)
    out_ref[...] = h_ref[...] + jnp.dot(o, w_ref[...], preferred_element_type=F32)


def _oproj(o, w, h, batch, seq):
    ns = seq // TM
    return pl.pallas_call(
        _oproj_kernel,
        out_shape=jax.ShapeDtypeStruct(h.shape, F32),
        grid=(batch, ns),
        in_specs=[
            pl.BlockSpec((None, N_PAIRS, TM, PAIR_W), lambda b, i: (b, 0, i, 0)),
            _resident((D_MODEL, D_MODEL), lambda b, i: (0, 0)),
            pl.BlockSpec((TM, D_MODEL), lambda b, i: (b * ns + i, 0)),
        ],
        out_specs=pl.BlockSpec((TM, D_MODEL), lambda b, i: (b * ns + i, 0)),
        compiler_params=_params("parallel", "parallel"),
        name="oproj",
    )(o, w, h)


def _head_lane_mask(shape, head_in_pair):
    lane = lax.broadcasted_iota(jnp.int32, shape, 1)
    return (lane >= HEAD_DIM * head_in_pair) & (lane < HEAD_DIM * (head_in_pair + 1))


def _dot_nt(a, b):
    return lax.dot_general(a, b, (((1,), (1,)), ((), ())), preferred_element_type=F32)


def _attn_a_kernel(q_ref, *refs):
    k_refs = refs[:A_KBLOCKS]
    vt_refs = refs[A_KBLOCKS:2 * A_KBLOCKS]
    bias_ref, zero_ref, o_ref = refs[2 * A_KBLOCKS:2 * A_KBLOCKS + 3]
    s_refs = refs[2 * A_KBLOCKS + 3:-1]
    m_ref = refs[-1]
    dz = zero_ref[0]
    i = pl.program_id(2)
    pens = [jnp.where(i - (A_KBLOCKS - 1) + j >= 0, 0.0, NEG_INF).astype(F32)
            for j in range(A_KBLOCKS)]
    heads = [(pp, hh) for pp in range(A_PAIRS) for hh in range(2)]
    chunks = [slice(c * LANES, (c + 1) * LANES) for c in range(QA // LANES)]
    for (pp, hh), s_ref in zip(heads, s_refs):
        q = q_ref[pp]
        qm = jnp.where(_head_lane_mask(q.shape, hh), q, jnp.zeros_like(q))
        for j in range(A_KBLOCKS):
            s_ref[j + dz] = _dot_nt(k_refs[j][pp], qm)
    for (pp, hh), s_ref in zip(heads, s_refs):
        for cols in chunks:
            blk_max = []
            for j in range(A_KBLOCKS):
                x = s_ref[j + dz, :, cols] + bias_ref[2 * pp + hh, j, :, cols]
                s_ref[j + dz, :, cols] = x
                blk_max.append(x.max(axis=0, keepdims=True) + pens[j])
            m_ref[pp, hh, :, cols] = functools.reduce(jnp.maximum, blk_max)
    ones = jnp.ones((ONES_ROWS, QA), BF16)
    for pp in range(A_PAIRS):
        outs = []
        for hh in range(2):
            s_ref = s_refs[2 * pp + hh]
            pv = None
            for j in range(A_KBLOCKS):
                columns = []
                for cols in chunks:
                    shift = m_ref[pp, hh, :, cols] - pens[j]
                    pieces = [jnp.exp2(s_ref[j + dz, r:r + KSTRIP, cols] - shift).astype(BF16)
                              for r in range(0, QA, KSTRIP)]
                    columns.append(jnp.concatenate(pieces, axis=0))
                p_tile = jnp.concatenate(columns, axis=1)
                vt = vt_refs[j][pp, hh * HEAD_DIM:(hh + 1) * HEAD_DIM, :]
                part = jnp.dot(jnp.concatenate([vt, ones], axis=0), p_tile,
                               preferred_element_type=F32)
                pv = part if pv is None else pv + part
            outs.append(pv[:HEAD_DIM] / pv[HEAD_DIM:HEAD_DIM + 1])
        o_ref[pp] = jnp.concatenate(outs, axis=0).T.astype(BF16)


def _attn_a(q, k, vt, bias_t):
    batch, _, seq, _ = q.shape
    nq = seq // QA

    def k_spec(j):
        back = A_KBLOCKS - 1 - j
        return pl.BlockSpec((None, A_PAIRS, QA, PAIR_W),
                            lambda b, p, i: (b, p, jnp.maximum(i - back, 0), 0))

    def vt_spec(j):
        back = A_KBLOCKS - 1 - j
        return pl.BlockSpec((None, A_PAIRS, PAIR_W, QA),
                            lambda b, p, i: (b, p, 0, jnp.maximum(i - back, 0)))

    return pl.pallas_call(
        _attn_a_kernel,
        out_shape=jax.ShapeDtypeStruct(q.shape, BF16),
        grid=(batch, N_PAIRS // A_PAIRS, nq),
        in_specs=(
            [pl.BlockSpec((None, A_PAIRS, QA, PAIR_W), lambda b, p, i: (b, p, i, 0))]
            + [k_spec(j) for j in range(A_KBLOCKS)]
            + [vt_spec(j) for j in range(A_KBLOCKS)]
            + [_resident((N_HEADS, A_KBLOCKS, QA, QA), lambda b, p, i: (0, 0, 0, 0))]
            + [pl.BlockSpec(memory_space=pltpu.SMEM)]
        ),
        out_specs=pl.BlockSpec((None, A_PAIRS, QA, PAIR_W), lambda b, p, i: (b, p, i, 0)),
        scratch_shapes=(
            [pltpu.VMEM((A_KBLOCKS, QA, QA), F32)] * (2 * A_PAIRS)
            + [pltpu.VMEM((A_PAIRS, 2, 1, QA), F32)]
        ),
        compiler_params=_params("parallel", "parallel", "arbitrary"),
        name="attn_a",
    )(q, *([k] * A_KBLOCKS), *([vt] * A_KBLOCKS), bias_t, jnp.zeros((1,), jnp.int32))


def _attn_a_bias_table(rel_bias):
    n_heads = rel_bias.shape[0]
    back = (A_KBLOCKS - 1) * QA
    keys = A_KBLOCKS * QA
    d = back + (QA - 1) - jnp.arange(keys + QA - 1)
    f = jnp.take(rel_bias.astype(F32), jnp.clip(d, -REL_CLIP, REL_CLIP) + REL_CLIP, axis=1)
    period = keys + QA
    g = jnp.pad(f, ((0, 0), (0, period - f.shape[1])))
    skew = jnp.tile(g, (1, QA))[:, :QA * (period - 1)].reshape(n_heads, QA, period - 1)
    toeplitz = skew[:, :, QA - 1:QA - 1 + keys]
    r = jnp.arange(QA)[:, None]
    x = jnp.arange(keys)[None, :]
    dchunk = (r + back) // CHUNK - x // CHUNK
    valid = (dchunk >= 0) & (dchunk <= LEFT_CHUNKS)
    table = jnp.where(valid[None], toeplitz * LOG2E, NEG_INF)
    return table.reshape(n_heads, QA, A_KBLOCKS, QA).transpose(0, 2, 3, 1)


def _split3(x):
    hi = x.astype(BF16)
    r1 = x - hi.astype(F32)
    mid = r1.astype(BF16)
    lo = (r1 - mid.astype(F32)).astype(BF16)
    return hi, mid, lo


def _kvf_kernel(x_ref, g_ref, wkv_ref, wf_ref, bf_ref, tri_ref, selq_ref, selk_ref, oneq_ref,
                onek_ref, k_ref, vt_ref, qe_ref, ke_ref, carry_ref):
    @pl.when(pl.program_id(1) == 0)
    def _():
        carry_ref[...] = jnp.zeros_like(carry_ref)

    hn = _rms_norm(x_ref[...], g_ref[...]).astype(BF16)
    yk = jnp.dot(hn, wkv_ref[:, :D_MODEL], preferred_element_type=F32)
    yv = jnp.dot(hn, wkv_ref[:, D_MODEL:], preferred_element_type=F32)
    for p in range(N_PAIRS):
        k_ref[p] = yk[:, p * PAIR_W:(p + 1) * PAIR_W].astype(BF16)
        vt_ref[p] = yv[:, p * PAIR_W:(p + 1) * PAIR_W].T.astype(BF16)
    z = jnp.dot(hn, wf_ref[...], preferred_element_type=F32) + bf_ref[...]
    log_f = jnp.minimum(z, 0.0) - jnp.log1p(jnp.exp(-jnp.abs(z)))
    tri = tri_ref[...]
    cum = sum(jnp.dot(tri, piece, preferred_element_type=F32) for piece in _split3(log_f))
    cum = cum + carry_ref[...]
    carry_ref[...] = cum[TM - 1:TM, :]
    pieces = jnp.concatenate(_split3(cum * LOG2E), axis=1)
    qe_ref[...] = (jnp.dot(pieces, selq_ref[...], preferred_element_type=F32)
                   + oneq_ref[...]).astype(BF16)
    ke_ref[...] = (jnp.dot(pieces, selk_ref[...], preferred_element_type=F32)
                   + onek_ref[...]).astype(BF16)


def _kvf(h, g, wkv, wf, bf, batch, seq):
    ns = seq // TM
    tri = (jnp.arange(TM)[:, None] >= jnp.arange(TM)[None, :]).astype(BF16)
    half = EXT_PER_HEAD // 2
    src = jnp.arange(3 * EXT_W)[:, None]
    dst = jnp.arange(EXT_W)[None, :]
    src_head, src_piece = src % EXT_W, src // EXT_W
    dst_head, dst_slot = dst // EXT_PER_HEAD, dst % EXT_PER_HEAD
    same_head = (src_head == dst_head) & (src_head < N_HEADS)
    sel_q = (same_head & (dst_slot == half + src_piece)).astype(BF16)
    sel_k = -(same_head & (dst_slot == src_piece)).astype(BF16)
    live = jnp.arange(EXT_W) < N_HEADS * EXT_PER_HEAD
    slot = jnp.arange(EXT_W) % EXT_PER_HEAD
    one_q = (live & (slot < half)).astype(F32).reshape(1, EXT_W)
    one_k = (live & (slot >= half)).astype(F32).reshape(1, EXT_W)

    k_sds = jax.ShapeDtypeStruct((batch, N_PAIRS, seq, PAIR_W), BF16)
    k_spec = pl.BlockSpec((None, N_PAIRS, TM, PAIR_W), lambda b, i: (b, 0, i, 0))
    vt_sds = jax.ShapeDtypeStruct((batch, N_PAIRS, ns, PAIR_W, TM), BF16)
    vt_spec = pl.BlockSpec((None, N_PAIRS, None, PAIR_W, TM), lambda b, i: (b, 0, i, 0, 0))
    ext_sds = jax.ShapeDtypeStruct((batch, seq, EXT_W), BF16)
    ext_spec = pl.BlockSpec((None, TM, EXT_W), lambda b, i: (b, i, 0))
    return pl.pallas_call(
        _kvf_kernel,
        out_shape=[k_sds, vt_sds, ext_sds, ext_sds],
        grid=(batch, ns),
        in_specs=[
            pl.BlockSpec((TM, D_MODEL), lambda b, i: (b * ns + i, 0)),
            _resident((1, D_MODEL), lambda b, i: (0, 0)),
            _resident((D_MODEL, 2 * D_MODEL), lambda b, i: (0, 0)),
            _resident((D_MODEL, EXT_W), lambda b, i: (0, 0)),
            _resident((1, EXT_W), lambda b, i: (0, 0)),
            _resident((TM, TM), lambda b, i: (0, 0)),
            _resident((3 * EXT_W, EXT_W), lambda b, i: (0, 0)),
            _resident((3 * EXT_W, EXT_W), lambda b, i: (0, 0)),
            _resident((1, EXT_W), lambda b, i: (0, 0)),
            _resident((1, EXT_W), lambda b, i: (0, 0)),
        ],
        out_specs=[k_spec, vt_spec, ext_spec, ext_spec],
        scratch_shapes=[pltpu.VMEM((1, EXT_W), F32)],
        compiler_params=_params("parallel", "arbitrary"),
        name="kvf",
    )(h, g, wkv, wf, bf, tri, sel_q, sel_k, one_q, one_k)


def _fox_kernel(q_ref, k_ref, vt_ref, qe_ref, ke_ref, zero_ref, o_ref, qa_ref, *scratch):
    s_refs = scratch[:-3]
    m_ref, alpha_ref, acc_ref = scratch[-3:]
    dz = zero_ref[0]
    group = pl.program_id(1)
    i = pl.program_id(2)
    heads = [(pp, hh) for pp in range(FOX_PAIRS) for hh in range(2)]
    qe = qe_ref[...]
    ext_lane = lax.broadcasted_iota(jnp.int32, qe.shape, 1)
    for slot, (pp, hh) in enumerate(heads):
        q = q_ref[pp]
        head = 2 * (group * FOX_PAIRS + pp) + hh
        qm = jnp.where(_head_lane_mask(q.shape, hh), q, jnp.zeros_like(q))
        own = (ext_lane >= EXT_PER_HEAD * head) & (ext_lane < EXT_PER_HEAD * (head + 1))
        qa_ref[slot] = jnp.concatenate([qm, jnp.where(own, qe, jnp.zeros_like(qe))], axis=1)

    half = QB // 2

    n_chunks = QB // LANES

    def key_block(blk, diagonal):
        off = pl.multiple_of(blk * QB, QB)
        k_ext = ke_ref[pl.ds(off, QB), :]
        for slot, (pp, hh) in enumerate(heads):
            ka = jnp.concatenate([k_ref[pp, pl.ds(off, QB), :], k_ext], axis=1)
            if diagonal:
                s_refs[slot][dz, :half, :half] = _dot_nt(ka[:half], qa_ref[slot, :half, :])
                s_refs[slot][dz, :, half:] = _dot_nt(ka, qa_ref[slot, half:, :])
            else:
                s_refs[slot][dz] = _dot_nt(ka, qa_ref[slot])

        def live_keys(c):
            return (c + 1) * LANES if diagonal else QB

        for slot in range(len(heads)):
            for c in range(n_chunks):
                cols = slice(c * LANES, (c + 1) * LANES)
                parts = []
                n_full = c * LANES if diagonal else QB
                if n_full:
                    parts.append(s_refs[slot][dz, :n_full, cols].max(axis=0, keepdims=True))
                if diagonal:
                    tile = slice(c * LANES, (c + 1) * LANES)
                    key_id = lax.broadcasted_iota(jnp.int32, (LANES, LANES), 0)
                    qry_id = lax.broadcasted_iota(jnp.int32, (LANES, LANES), 1)
                    x = jnp.where(key_id <= qry_id, s_refs[slot][dz, tile, cols], NEG_INF)
                    s_refs[slot][dz, tile, cols] = x
                    parts.append(x.max(axis=0, keepdims=True))
                blk_max = functools.reduce(jnp.maximum, parts)
                if diagonal:
                    m_ref[slot, :, cols] = blk_max
                else:
                    m_old = m_ref[slot, :, cols]
                    m_new = jnp.maximum(m_old, blk_max)
                    m_ref[slot, :, cols] = m_new
                    alpha_ref[slot, :, cols] = jnp.exp2(m_old - m_new)
        chunks_per_tile = MXU_TILE // LANES
        ones = jnp.ones((ONES_ROWS, QB), BF16)
        for slot, (pp, hh) in enumerate(heads):
            vt = vt_ref[pp, blk, hh * HEAD_DIM:(hh + 1) * HEAD_DIM, :]
            vt = jnp.concatenate([vt, ones], axis=0)
            for nt in range(QB // MXU_TILE):
                tile_cols = slice(nt * MXU_TILE, (nt + 1) * MXU_TILE)
                n_key_tiles = (nt + 1) if diagonal else QB // MXU_TILE
                pv = None
                for kt in range(n_key_tiles):
                    columns = []
                    for c in range(nt * chunks_per_tile, (nt + 1) * chunks_per_tile):
                        cols = slice(c * LANES, (c + 1) * LANES)
                        m_new = m_ref[slot, :, cols]
                        pieces = []
                        for r in range(kt * MXU_TILE, (kt + 1) * MXU_TILE, KSTRIP):
                            if r < live_keys(c):
                                pr = jnp.exp2(s_refs[slot][dz, r:r + KSTRIP, cols] - m_new)
                                pieces.append(pr.astype(BF16))
                            else:
                                pieces.append(jnp.zeros((KSTRIP, LANES), BF16))
                        columns.append(jnp.concatenate(pieces, axis=0))
                    p_tile = jnp.concatenate(columns, axis=1)
                    part = jnp.dot(vt[:, kt * MXU_TILE:(kt + 1) * MXU_TILE], p_tile,
                                   preferred_element_type=F32)
                    pv = part if pv is None else pv + part
                if diagonal:
                    acc_ref[slot, :, tile_cols] = pv
                else:
                    acc_ref[slot, :, tile_cols] = (
                        alpha_ref[slot, :, tile_cols] * acc_ref[slot, :, tile_cols] + pv)

    def body(j, carry):
        key_block(j, False)
        return carry

    key_block(i, True)
    lax.fori_loop(0, i, body, 0)
    for pp in range(FOX_PAIRS):
        o_t = jnp.concatenate(
            [acc_ref[2 * pp + hh, :HEAD_DIM, :] / acc_ref[2 * pp + hh, HEAD_DIM:HEAD_DIM + 1, :]
             for hh in range(2)], axis=0)
        o_ref[pp] = o_t.T.astype(BF16)


def _fox(q, k, vt, q_ext, k_ext):
    batch, _, seq, _ = q.shape
    nq = seq // QB
    n_heads = 2 * FOX_PAIRS
    assert vt.shape == (batch, N_PAIRS, nq, PAIR_W, QB)
    return pl.pallas_call(
        _fox_kernel,
        out_shape=jax.ShapeDtypeStruct(q.shape, BF16),
        grid=(batch, N_PAIRS // FOX_PAIRS, nq),
        in_specs=[
            pl.BlockSpec((None, FOX_PAIRS, QB, PAIR_W), lambda b, p, i: (b, p, i, 0)),
            _resident((None, FOX_PAIRS, seq, PAIR_W), lambda b, p, i: (b, p, 0, 0)),
            _resident((None, FOX_PAIRS, nq, PAIR_W, QB), lambda b, p, i: (b, p, 0, 0, 0)),
            pl.BlockSpec((None, QB, EXT_W), lambda b, p, i: (b, i, 0)),
            _resident((None, seq, EXT_W), lambda b, p, i: (b, 0, 0)),
            pl.BlockSpec(memory_space=pltpu.SMEM),
        ],
        out_specs=pl.BlockSpec((None, FOX_PAIRS, QB, PAIR_W), lambda b, p, i: (b, p, i, 0)),
        scratch_shapes=(
            [pltpu.VMEM((n_heads, QB, PAIR_W + EXT_W), BF16)]
            + [pltpu.VMEM((1, QB, QB), F32)] * n_heads
            + [pltpu.VMEM((n_heads, 1, QB), F32),
               pltpu.VMEM((n_heads, 1, QB), F32),
               pltpu.VMEM((n_heads, HEAD_DIM + ONES_ROWS, QB), F32)]
        ),
        compiler_params=_params("parallel", "parallel", "arbitrary"),
        name="fox",
    )(q, k, vt, q_ext, k_ext, jnp.zeros((1,), jnp.int32))


def kernel(x, ffn_norm, ffn_w_gate, ffn_w_up, ffn_w_down, mix_norm, a_w_qkv, a_w_o, a_rel_bias,
           kv_norm, b_w_kvf, b_f_bias, b_w_q, b_w_o, final_norm):
    batch, seq, _ = x.shape
    depth = ffn_norm.shape[0]
    n_a = a_w_qkv.shape[0]
    n_b = b_w_q.shape[0]
    assert seq % TM == 0 and seq % QA == 0 and seq % QB == 0 and n_a + n_b == depth

    wg = ffn_w_gate.astype(BF16)
    wu = ffn_w_up.astype(BF16)
    wd = ffn_w_down.astype(BF16)
    gf = final_norm.reshape(1, D_MODEL)

    def half_ffn(h, layer, pos, final=False):
        return _ffn(h, ffn_norm[layer, pos].reshape(1, D_MODEL), wg, wu, wd, gf, layer, pos, final)

    h = x.reshape(batch * seq, D_MODEL)
    for layer in range(n_a):
        h = half_ffn(h, layer, 0)
        q, k, vt = _proj(h, mix_norm[layer].reshape(1, D_MODEL), a_w_qkv[layer].astype(BF16),
                         batch, seq, True, "qkv_proj", transpose_last=True)
        o = _attn_a(q, k, vt, _attn_a_bias_table(a_rel_bias[layer]))
        h = _oproj(o, a_w_o[layer].astype(BF16), h, batch, seq)
        h = half_ffn(h, layer, 1)

    wkv = b_w_kvf[:, :2 * D_MODEL].astype(BF16)
    wf = jnp.pad(b_w_kvf[:, 2 * D_MODEL:], ((0, 0), (0, EXT_W - N_HEADS))).astype(BF16)
    bf = jnp.pad(b_f_bias, (0, EXT_W - N_HEADS)).reshape(1, EXT_W).astype(F32)
    k_sh, v_sh, q_ext, k_ext = _kvf(h, kv_norm.reshape(1, D_MODEL), wkv, wf, bf, batch, seq)

    for lb in range(n_b):
        layer = n_a + lb
        h = half_ffn(h, layer, 0)
        (q,) = _proj(h, mix_norm[layer].reshape(1, D_MODEL), b_w_q[lb].astype(BF16),
                     batch, seq, True, "q_proj")
        o = _fox(q, k_sh, v_sh, q_ext, k_ext)
        h = _oproj(o, b_w_o[lb].astype(BF16), h, batch, seq)
        h = half_ffn(h, layer, 1, final=(layer == depth - 1))

    return h.reshape(batch, seq, D_MODEL)
```

```python
import functools

import jax
import jax.numpy as jnp
from jax import lax
from jax.experimental import pallas as pl
from jax.experimental.pallas import tpu as pltpu

D_MODEL = 1024
N_HEADS = 16
HEAD_DIM = 64
N_PAIRS = N_HEADS // 2
PAIR_W = 2 * HEAD_DIM
D_FF = 2816
CHUNK = 64
LEFT_CHUNKS = 8
REL_CLIP = 256
EPS = 1e-6
NEG_INF = -1e30
ATTN_SCALE = HEAD_DIM ** -0.5
LOG2E = 1.4426950408889634
Q_SCALE = ATTN_SCALE * LOG2E
FFN_RES_WEIGHT = 0.5

TM = 512
FF_CHUNK = 256
QA = 256
A_KBLOCKS = LEFT_CHUNKS * CHUNK // QA + 1
A_PAIRS = N_PAIRS
QB = 512
FOX_PAIRS = N_PAIRS
EXT_W = 128
EXT_PER_HEAD = 6
LANES = 128
MXU_TILE = 256
KSTRIP = 128
ONES_ROWS = 16
VMEM_LIMIT = 56 * 1024 * 1024

F32 = jnp.float32
BF16 = jnp.bfloat16


def _rms_norm(x, g):
    y = x * lax.rsqrt(jnp.mean(x * x, axis=-1, keepdims=True) + EPS)
    return y * g


def _resident(block_shape, index_map):
    return pl.BlockSpec(block_shape, index_map, pipeline_mode=pl.Buffered(1))


def _params(*semantics):
    return pltpu.CompilerParams(dimension_semantics=semantics, vmem_limit_bytes=VMEM_LIMIT)


def _ffn_kernel(*refs, has_mixer, n_proj, transpose_last, final):
    x_ref, g_ref, wg_ref, wu_ref, wd_ref, gf_ref = refs[:6]
    at = 6
    if has_mixer:
        o_ref, wo_ref = refs[at:at + 2]
        at += 2
    if n_proj:
        gm_ref, wp_ref = refs[at:at + 2]
        at += 2
    out_ref = refs[at]
    proj_refs = refs[at + 1:at + 1 + n_proj]
    a_ref = refs[-1]

    x = x_ref[...]
    if has_mixer:
        o = jnp.concatenate([o_ref[p] for p in range(N_PAIRS)], axis=1)
        x = x + jnp.dot(o, wo_ref[...], preferred_element_type=F32)
    hn = _rms_norm(x, g_ref[...]).astype(BF16)
    for c in range(D_FF // FF_CHUNK):
        sl = slice(c * FF_CHUNK, (c + 1) * FF_CHUNK)
        gate = jnp.dot(hn, wg_ref[:, sl], preferred_element_type=F32)
        up = jnp.dot(hn, wu_ref[:, sl], preferred_element_type=F32)
        a_ref[:, sl] = (gate * jax.nn.sigmoid(gate) * up).astype(BF16)
    y = jnp.dot(a_ref[...], wd_ref[...], preferred_element_type=F32)
    out = x + FFN_RES_WEIGHT * y
    if final:
        out = _rms_norm(out, gf_ref[...])
    out_ref[...] = out
    if n_proj:
        hp = _rms_norm(out, gm_ref[...]).astype(BF16)
        for n, p_ref in enumerate(proj_refs):
            y = jnp.dot(hp, wp_ref[:, n * D_MODEL:(n + 1) * D_MODEL], preferred_element_type=F32)
            if n == 0:
                y = y * Q_SCALE
            for p in range(N_PAIRS):
                y_pair = y[:, p * PAIR_W:(p + 1) * PAIR_W]
                if transpose_last and n == n_proj - 1:
                    y_pair = y_pair.T
                p_ref[p] = y_pair.astype(BF16)


def _ffn(h, g, wg, wu, wd, gf, layer, pos, batch, seq, mixer=None, proj=None, final=False):
    ns = seq // TM

    def rows(b, i):
        return (b * ns + i, 0)

    def fixed(b, i):
        return (0, 0)

    def weights(b, i):
        return (layer, pos, 0, 0)

    pair_spec = pl.BlockSpec((None, N_PAIRS, TM, PAIR_W), lambda b, i: (b, 0, i, 0))
    inputs = [h, g, wg, wu, wd, gf]
    in_specs = [
        pl.BlockSpec((TM, D_MODEL), rows),
        _resident((1, D_MODEL), fixed),
        _resident((None, None, D_MODEL, D_FF), weights),
        _resident((None, None, D_MODEL, D_FF), weights),
        _resident((None, None, D_FF, D_MODEL), weights),
        _resident((1, D_MODEL), fixed),
    ]
    if mixer is not None:
        inputs += list(mixer)
        in_specs += [pair_spec, _resident((D_MODEL, D_MODEL), fixed)]
    out_shape = [jax.ShapeDtypeStruct(h.shape, F32)]
    out_specs = [pl.BlockSpec((TM, D_MODEL), rows)]
    n_proj, transpose_last = 0, False
    if proj is not None:
        g_mix, w_proj, transpose_last = proj
        n_proj = w_proj.shape[1] // D_MODEL
        inputs += [g_mix, w_proj]
        in_specs += [_resident((1, D_MODEL), fixed), _resident((D_MODEL, n_proj * D_MODEL), fixed)]
        out_shape += [jax.ShapeDtypeStruct((batch, N_PAIRS, seq, PAIR_W), BF16)] * n_proj
        out_specs += [pair_spec] * n_proj
        if transpose_last:
            out_shape[-1] = jax.ShapeDtypeStruct((batch, N_PAIRS, PAIR_W, seq), BF16)
            out_specs[-1] = pl.BlockSpec((None, N_PAIRS, PAIR_W, TM), lambda b, i: (b, 0, 0, i))
    return pl.pallas_call(
        functools.partial(_ffn_kernel, has_mixer=mixer is not None, n_proj=n_proj,
                          transpose_last=transpose_last, final=final),
        out_shape=out_shape,
        grid=(batch, ns),
        in_specs=in_specs,
        out_specs=out_specs,
        scratch_shapes=[pltpu.VMEM((TM, D_FF), BF16)],
        compiler_params=_params("parallel", "parallel"),
        name="ffn",
    )(*inputs)


def _head_lane_mask(shape, head_in_pair):
    lane = lax.broadcasted_iota(jnp.int32, shape, 1)
    return (lane >= HEAD_DIM * head_in_pair) & (lane < HEAD_DIM * (head_in_pair + 1))


def _dot_nt(a, b):
    return lax.dot_general(a, b, (((1,), (1,)), ((), ())), preferred_element_type=F32)


def _attn_a_kernel(q_ref, *refs):
    k_refs = refs[:A_KBLOCKS]
    vt_refs = refs[A_KBLOCKS:2 * A_KBLOCKS]
    bias_ref, zero_ref, o_ref = refs[2 * A_KBLOCKS:2 * A_KBLOCKS + 3]
    s_refs = refs[2 * A_KBLOCKS + 3:-1]
    m_ref = refs[-1]
    dz = zero_ref[0]
    i = pl.program_id(2)
    pens = [jnp.where(i - (A_KBLOCKS - 1) + j >= 0, 0.0, NEG_INF).astype(F32)
            for j in range(A_KBLOCKS)]
    heads = [(pp, hh) for pp in range(A_PAIRS) for hh in range(2)]
    chunks = [slice(c * LANES, (c + 1) * LANES) for c in range(QA // LANES)]
    for (pp, hh), s_ref in zip(heads, s_refs):
        q = q_ref[pp]
        qm = jnp.where(_head_lane_mask(q.shape, hh), q, jnp.zeros_like(q))
        for j in range(A_KBLOCKS):
            s_ref[j + dz] = _dot_nt(k_refs[j][pp], qm)
    for (pp, hh), s_ref in zip(heads, s_refs):
        for cols in chunks:
            blk_max = []
            for j in range(A_KBLOCKS):
                x = s_ref[j + dz, :, cols] + bias_ref[2 * pp + hh, j, :, cols]
                s_ref[j + dz, :, cols] = x
                blk_max.append(x.max(axis=0, keepdims=True) + pens[j])
            m_ref[pp, hh, :, cols] = functools.reduce(jnp.maximum, blk_max)
    ones = jnp.ones((ONES_ROWS, QA), BF16)
    for pp in range(A_PAIRS):
        outs = []
        for hh in range(2):
            s_ref = s_refs[2 * pp + hh]
            pv = None
            for j in range(A_KBLOCKS):
                columns = []
                for cols in chunks:
                    shift = m_ref[pp, hh, :, cols] - pens[j]
                    pieces = [jnp.exp2(s_ref[j + dz, r:r + KSTRIP, cols] - shift).astype(BF16)
                              for r in range(0, QA, KSTRIP)]
                    columns.append(jnp.concatenate(pieces, axis=0))
                p_tile = jnp.concatenate(columns, axis=1)
                vt = vt_refs[j][pp, hh * HEAD_DIM:(hh + 1) * HEAD_DIM, :]
                part = jnp.dot(jnp.concatenate([vt, ones], axis=0), p_tile,
                               preferred_element_type=F32)
                pv = part if pv is None else pv + part
            outs.append(pv[:HEAD_DIM] / pv[HEAD_DIM:HEAD_DIM + 1])
        o_ref[pp] = jnp.concatenate(outs, axis=0).T.astype(BF16)


def _attn_a(q, k, vt, bias_t):
    batch, _, seq, _ = q.shape
    nq = seq // QA

    def k_spec(j):
        back = A_KBLOCKS - 1 - j
        return pl.BlockSpec((None, A_PAIRS, QA, PAIR_W),
                            lambda b, p, i: (b, p, jnp.maximum(i - back, 0), 0))

    def vt_spec(j):
        back = A_KBLOCKS - 1 - j
        return pl.BlockSpec((None, A_PAIRS, PAIR_W, QA),
                            lambda b, p, i: (b, p, 0, jnp.maximum(i - back, 0)))

    return pl.pallas_call(
        _attn_a_kernel,
        out_shape=jax.ShapeDtypeStruct(q.shape, BF16),
        grid=(batch, N_PAIRS // A_PAIRS, nq),
        in_specs=(
            [pl.BlockSpec((None, A_PAIRS, QA, PAIR_W), lambda b, p, i: (b, p, i, 0))]
            + [k_spec(j) for j in range(A_KBLOCKS)]
            + [vt_spec(j) for j in range(A_KBLOCKS)]
            + [_resident((N_HEADS, A_KBLOCKS, QA, QA), lambda b, p, i: (0, 0, 0, 0))]
            + [pl.BlockSpec(memory_space=pltpu.SMEM)]
        ),
        out_specs=pl.BlockSpec((None, A_PAIRS, QA, PAIR_W), lambda b, p, i: (b, p, i, 0)),
        scratch_shapes=(
            [pltpu.VMEM((A_KBLOCKS, QA, QA), F32)] * (2 * A_PAIRS)
            + [pltpu.VMEM((A_PAIRS, 2, 1, QA), F32)]
        ),
        compiler_params=_params("parallel", "parallel", "arbitrary"),
        name="attn_a",
    )(q, *([k] * A_KBLOCKS), *([vt] * A_KBLOCKS), bias_t, jnp.zeros((1,), jnp.int32))


def _attn_a_bias_table(rel_bias):
    n_heads = rel_bias.shape[0]
    back = (A_KBLOCKS - 1) * QA
    keys = A_KBLOCKS * QA
    d = back + (QA - 1) - jnp.arange(keys + QA - 1)
    f = jnp.take(rel_bias.astype(F32), jnp.clip(d, -REL_CLIP, REL_CLIP) + REL_CLIP, axis=1)
    period = keys + QA
    g = jnp.pad(f, ((0, 0), (0, period - f.shape[1])))
    skew = jnp.tile(g, (1, QA))[:, :QA * (period - 1)].reshape(n_heads, QA, period - 1)
    toeplitz = skew[:, :, QA - 1:QA - 1 + keys]
    r = jnp.arange(QA)[:, None]
    x = jnp.arange(keys)[None, :]
    dchunk = (r + back) // CHUNK - x // CHUNK
    valid = (dchunk >= 0) & (dchunk <= LEFT_CHUNKS)
    table = jnp.where(valid[None], toeplitz * LOG2E, NEG_INF)
    return table.reshape(n_heads, QA, A_KBLOCKS, QA).transpose(0, 2, 3, 1)


def _split3(x):
    hi = x.astype(BF16)
    r1 = x - hi.astype(F32)
    mid = r1.astype(BF16)
    lo = (r1 - mid.astype(F32)).astype(BF16)
    return hi, mid, lo


def _kvf_kernel(x_ref, g_ref, wkv_ref, wf_ref, bf_ref, tri_ref, selq_ref, selk_ref, oneq_ref,
                onek_ref, k_ref, vt_ref, qe_ref, ke_ref, carry_ref):
    @pl.when(pl.program_id(1) == 0)
    def _():
        carry_ref[...] = jnp.zeros_like(carry_ref)

    hn = _rms_norm(x_ref[...], g_ref[...]).astype(BF16)
    yk = jnp.dot(hn, wkv_ref[:, :D_MODEL], preferred_element_type=F32)
    yv = jnp.dot(hn, wkv_ref[:, D_MODEL:], preferred_element_type=F32)
    for p in range(N_PAIRS):
        k_ref[p] = yk[:, p * PAIR_W:(p + 1) * PAIR_W].astype(BF16)
        vt_ref[p] = yv[:, p * PAIR_W:(p + 1) * PAIR_W].T.astype(BF16)
    z = jnp.dot(hn, wf_ref[...], preferred_element_type=F32) + bf_ref[...]
    log_f = jnp.minimum(z, 0.0) - jnp.log1p(jnp.exp(-jnp.abs(z)))
    tri = tri_ref[...]
    cum = sum(jnp.dot(tri, piece, preferred_element_type=F32) for piece in _split3(log_f))
    cum = cum + carry_ref[...]
    carry_ref[...] = cum[TM - 1:TM, :]
    pieces = jnp.concatenate(_split3(cum * LOG2E), axis=1)
    qe_ref[...] = (jnp.dot(pieces, selq_ref[...], preferred_element_type=F32)
                   + oneq_ref[...]).astype(BF16)
    ke_ref[...] = (jnp.dot(pieces, selk_ref[...], preferred_element_type=F32)
                   + onek_ref[...]).astype(BF16)


def _kvf(h, g, wkv, wf, bf, batch, seq):
    ns = seq // TM
    tri = (jnp.arange(TM)[:, None] >= jnp.arange(TM)[None, :]).astype(BF16)
    half = EXT_PER_HEAD // 2
    src = jnp.arange(3 * EXT_W)[:, None]
    dst = jnp.arange(EXT_W)[None, :]
    src_head, src_piece = src % EXT_W, src // EXT_W
    dst_head, dst_slot = dst // EXT_PER_HEAD, dst % EXT_PER_HEAD
    same_head = (src_head == dst_head) & (src_head < N_HEADS)
    sel_q = (same_head & (dst_slot == half + src_piece)).astype(BF16)
    sel_k = -(same_head & (dst_slot == src_piece)).astype(BF16)
    live = jnp.arange(EXT_W) < N_HEADS * EXT_PER_HEAD
    slot = jnp.arange(EXT_W) % EXT_PER_HEAD
    one_q = (live & (slot < half)).astype(F32).reshape(1, EXT_W)
    one_k = (live & (slot >= half)).astype(F32).reshape(1, EXT_W)

    k_sds = jax.ShapeDtypeStruct((batch, N_PAIRS, seq, PAIR_W), BF16)
    k_spec = pl.BlockSpec((None, N_PAIRS, TM, PAIR_W), lambda b, i: (b, 0, i, 0))
    vt_sds = jax.ShapeDtypeStruct((batch, N_PAIRS, ns, PAIR_W, TM), BF16)
    vt_spec = pl.BlockSpec((None, N_PAIRS, None, PAIR_W, TM), lambda b, i: (b, 0, i, 0, 0))
    ext_sds = jax.ShapeDtypeStruct((batch, seq, EXT_W), BF16)
    ext_spec = pl.BlockSpec((None, TM, EXT_W), lambda b, i: (b, i, 0))
    return pl.pallas_call(
        _kvf_kernel,
        out_shape=[k_sds, vt_sds, ext_sds, ext_sds],
        grid=(batch, ns),
        in_specs=[
            pl.BlockSpec((TM, D_MODEL), lambda b, i: (b * ns + i, 0)),
            _resident((1, D_MODEL), lambda b, i: (0, 0)),
            _resident((D_MODEL, 2 * D_MODEL), lambda b, i: (0, 0)),
            _resident((D_MODEL, EXT_W), lambda b, i: (0, 0)),
            _resident((1, EXT_W), lambda b, i: (0, 0)),
            _resident((TM, TM), lambda b, i: (0, 0)),
            _resident((3 * EXT_W, EXT_W), lambda b, i: (0, 0)),
            _resident((3 * EXT_W, EXT_W), lambda b, i: (0, 0)),
            _resident((1, EXT_W), lambda b, i: (0, 0)),
            _resident((1, EXT_W), lambda b, i: (0, 0)),
        ],
        out_specs=[k_spec, vt_spec, ext_spec, ext_spec],
        scratch_shapes=[pltpu.VMEM((1, EXT_W), F32)],
        compiler_params=_params("parallel", "arbitrary"),
        name="kvf",
    )(h, g, wkv, wf, bf, tri, sel_q, sel_k, one_q, one_k)


def _fox_kernel(q_ref, k_ref, vt_ref, qe_ref, ke_ref, zero_ref, o_ref, qa_ref, *scratch):
    s_refs = scratch[:-3]
    m_ref, alpha_ref, acc_ref = scratch[-3:]
    dz = zero_ref[0]
    group = pl.program_id(1)
    i = pl.program_id(2)
    heads = [(pp, hh) for pp in range(FOX_PAIRS) for hh in range(2)]
    qe = qe_ref[...]
    ext_lane = lax.broadcasted_iota(jnp.int32, qe.shape, 1)
    for slot, (pp, hh) in enumerate(heads):
        q = q_ref[pp]
        head = 2 * (group * FOX_PAIRS + pp) + hh
        qm = jnp.where(_head_lane_mask(q.shape, hh), q, jnp.zeros_like(q))
        own = (ext_lane >= EXT_PER_HEAD * head) & (ext_lane < EXT_PER_HEAD * (head + 1))
        qa_ref[slot] = jnp.concatenate([qm, jnp.where(own, qe, jnp.zeros_like(qe))], axis=1)

    half = QB // 2

    n_chunks = QB // LANES

    def key_block(blk, diagonal):
        off = pl.multiple_of(blk * QB, QB)
        k_ext = ke_ref[pl.ds(off, QB), :]
        for slot, (pp, hh) in enumerate(heads):
            ka = jnp.concatenate([k_ref[pp, pl.ds(off, QB), :], k_ext], axis=1)
            if diagonal:
                s_refs[slot][dz, :half, :half] = _dot_nt(ka[:half], qa_ref[slot, :half, :])
                s_refs[slot][dz, :, half:] = _dot_nt(ka, qa_ref[slot, half:, :])
            else:
                s_refs[slot][dz] = _dot_nt(ka, qa_ref[slot])

        def live_keys(c):
            return (c + 1) * LANES if diagonal else QB

        for slot in range(len(heads)):
            for c in range(n_chunks):
                cols = slice(c * LANES, (c + 1) * LANES)
                parts = []
                n_full = c * LANES if diagonal else QB
                if n_full:
                    parts.append(s_refs[slot][dz, :n_full, cols].max(axis=0, keepdims=True))
                if diagonal:
                    tile = slice(c * LANES, (c + 1) * LANES)
                    key_id = lax.broadcasted_iota(jnp.int32, (LANES, LANES), 0)
                    qry_id = lax.broadcasted_iota(jnp.int32, (LANES, LANES), 1)
                    x = jnp.where(key_id <= qry_id, s_refs[slot][dz, tile, cols], NEG_INF)
                    s_refs[slot][dz, tile, cols] = x
                    parts.append(x.max(axis=0, keepdims=True))
                blk_max = functools.reduce(jnp.maximum, parts)
                if diagonal:
                    m_ref[slot, :, cols] = blk_max
                else:
                    m_old = m_ref[slot, :, cols]
                    m_new = jnp.maximum(m_old, blk_max)
                    m_ref[slot, :, cols] = m_new
                    alpha_ref[slot, :, cols] = jnp.exp2(m_old - m_new)
        chunks_per_tile = MXU_TILE // LANES
        ones = jnp.ones((ONES_ROWS, QB), BF16)
        for slot, (pp, hh) in enumerate(heads):
            vt = vt_ref[pp, blk, hh * HEAD_DIM:(hh + 1) * HEAD_DIM, :]
            vt = jnp.concatenate([vt, ones], axis=0)
            for nt in range(QB // MXU_TILE):
                tile_cols = slice(nt * MXU_TILE, (nt + 1) * MXU_TILE)
                n_key_tiles = (nt + 1) if diagonal else QB // MXU_TILE
                pv = None
                for kt in range(n_key_tiles):
                    columns = []
                    for c in range(nt * chunks_per_tile, (nt + 1) * chunks_per_tile):
                        cols = slice(c * LANES, (c + 1) * LANES)
                        m_new = m_ref[slot, :, cols]
                        pieces = []
                        for r in range(kt * MXU_TILE, (kt + 1) * MXU_TILE, KSTRIP):
                            if r < live_keys(c):
                                pr = jnp.exp2(s_refs[slot][dz, r:r + KSTRIP, cols] - m_new)
                                pieces.append(pr.astype(BF16))
                            else:
                                pieces.append(jnp.zeros((KSTRIP, LANES), BF16))
                        columns.append(jnp.concatenate(pieces, axis=0))
                    p_tile = jnp.concatenate(columns, axis=1)
                    part = jnp.dot(vt[:, kt * MXU_TILE:(kt + 1) * MXU_TILE], p_tile,
                                   preferred_element_type=F32)
                    pv = part if pv is None else pv + part
                if diagonal:
                    acc_ref[slot, :, tile_cols] = pv
                else:
                    acc_ref[slot, :, tile_cols] = (
                        alpha_ref[slot, :, tile_cols] * acc_ref[slot, :, tile_cols] + pv)

    def body(j, carry):
        key_block(j, False)
        return carry

    key_block(i, True)
    lax.fori_loop(0, i, body, 0)
    for pp in range(FOX_PAIRS):
        o_t = jnp.concatenate(
            [acc_ref[2 * pp + hh, :HEAD_DIM, :] / acc_ref[2 * pp + hh, HEAD_DIM:HEAD_DIM + 1, :]
             for hh in range(2)], axis=0)
        o_ref[pp] = o_t.T.astype(BF16)


def _fox(q, k, vt, q_ext, k_ext):
    batch, _, seq, _ = q.shape
    nq = seq // QB
    n_heads = 2 * FOX_PAIRS
    assert vt.shape == (batch, N_PAIRS, nq, PAIR_W, QB)
    return pl.pallas_call(
        _fox_kernel,
        out_shape=jax.ShapeDtypeStruct(q.shape, BF16),
        grid=(batch, N_PAIRS // FOX_PAIRS, nq),
        in_specs=[
            pl.BlockSpec((None, FOX_PAIRS, QB, PAIR_W), lambda b, p, i: (b, p, i, 0)),
            _resident((None, FOX_PAIRS, seq, PAIR_W), lambda b, p, i: (b, p, 0, 0)),
            _resident((None, FOX_PAIRS, nq, PAIR_W, QB), lambda b, p, i: (b, p, 0, 0, 0)),
            pl.BlockSpec((None, QB, EXT_W), lambda b, p, i: (b, i, 0)),
            _resident((None, seq, EXT_W), lambda b, p, i: (b, 0, 0)),
            pl.BlockSpec(memory_space=pltpu.SMEM),
        ],
        out_specs=pl.BlockSpec((None, FOX_PAIRS, QB, PAIR_W), lambda b, p, i: (b, p, i, 0)),
        scratch_shapes=(
            [pltpu.VMEM((n_heads, QB, PAIR_W + EXT_W), BF16)]
            + [pltpu.VMEM((1, QB, QB), F32)] * n_heads
            + [pltpu.VMEM((n_heads, 1, QB), F32),
               pltpu.VMEM((n_heads, 1, QB), F32),
               pltpu.VMEM((n_heads, HEAD_DIM + ONES_ROWS, QB), F32)]
        ),
        compiler_params=_params("parallel", "parallel", "arbitrary"),
        name="fox",
    )(q, k, vt, q_ext, k_ext, jnp.zeros((1,), jnp.int32))


def kernel(x, ffn_norm, ffn_w_gate, ffn_w_up, ffn_w_down, mix_norm, a_w_qkv, a_w_o, a_rel_bias,
           kv_norm, b_w_kvf, b_f_bias, b_w_q, b_w_o, final_norm):
    batch, seq, _ = x.shape
    depth = ffn_norm.shape[0]
    n_a = a_w_qkv.shape[0]
    n_b = b_w_q.shape[0]
    assert seq % TM == 0 and seq % QA == 0 and seq % QB == 0 and n_a + n_b == depth

    wg = ffn_w_gate.astype(BF16)
    wu = ffn_w_up.astype(BF16)
    wd = ffn_w_down.astype(BF16)
    gf = final_norm.reshape(1, D_MODEL)

    def half_ffn(h, layer, pos, **fused):
        return _ffn(h, ffn_norm[layer, pos].reshape(1, D_MODEL), wg, wu, wd, gf, layer, pos,
                    batch, seq, **fused)

    h = x.reshape(batch * seq, D_MODEL)
    for layer in range(n_a):
        g_mix = mix_norm[layer].reshape(1, D_MODEL)
        h, q, k, vt = half_ffn(h, layer, 0, proj=(g_mix, a_w_qkv[layer].astype(BF16), True))
        o = _attn_a(q, k, vt, _attn_a_bias_table(a_rel_bias[layer]))
        (h,) = half_ffn(h, layer, 1, mixer=(o, a_w_o[layer].astype(BF16)))

    wkv = b_w_kvf[:, :2 * D_MODEL].astype(BF16)
    wf = jnp.pad(b_w_kvf[:, 2 * D_MODEL:], ((0, 0), (0, EXT_W - N_HEADS))).astype(BF16)
    bf = jnp.pad(b_f_bias, (0, EXT_W - N_HEADS)).reshape(1, EXT_W).astype(F32)
    k_sh, vt_sh, q_ext, k_ext = _kvf(h, kv_norm.reshape(1, D_MODEL), wkv, wf, bf, batch, seq)

    for lb in range(n_b):
        layer = n_a + lb
        g_mix = mix_norm[layer].reshape(1, D_MODEL)
        h, q = half_ffn(h, layer, 0, proj=(g_mix, b_w_q[lb].astype(BF16), False))
        o = _fox(q, k_sh, vt_sh, q_ext, k_ext)
        (h,) = half_ffn(h, layer, 1, mixer=(o, b_w_o[lb].astype(BF16)),
                        final=(layer == depth - 1))

    return h.reshape(batch, seq, D_MODEL)
```

```python
import functools

import jax
import jax.numpy as jnp
from jax import lax
from jax.experimental import pallas as pl
from jax.experimental.pallas import tpu as pltpu

D_MODEL = 1024
N_HEADS = 16
HEAD_DIM = 64
N_PAIRS = N_HEADS // 2
PAIR_W = 2 * HEAD_DIM
D_FF = 2816
CHUNK = 64
LEFT_CHUNKS = 8
REL_CLIP = 256
EPS = 1e-6
NEG_INF = -1e30
ATTN_SCALE = HEAD_DIM ** -0.5
LOG2E = 1.4426950408889634
Q_SCALE = ATTN_SCALE * LOG2E
FFN_RES_WEIGHT = 0.5

TM = 512
FF_CHUNK = 256
QA = 256
A_KBLOCKS = LEFT_CHUNKS * CHUNK // QA + 1
A_PAIRS = N_PAIRS
QB = 512
FOX_PAIRS = N_PAIRS
EXT_W = 128
EXT_PER_HEAD = 6
LANES = 128
MXU_TILE = 256
KSTRIP = 128
ONES_ROWS = 16
VMEM_LIMIT = 56 * 1024 * 1024

F32 = jnp.float32
BF16 = jnp.bfloat16


def _rms_norm(x, g):
    y = x * lax.rsqrt(jnp.mean(x * x, axis=-1, keepdims=True) + EPS)
    return y * g


def _resident(block_shape, index_map):
    return pl.BlockSpec(block_shape, index_map, pipeline_mode=pl.Buffered(1))


def _params(*semantics):
    return pltpu.CompilerParams(dimension_semantics=semantics, vmem_limit_bytes=VMEM_LIMIT)


def _ffn_kernel(*refs, has_mixer, transposed, final):
    n_proj = len(transposed)
    x_ref, g_ref, wg_ref, wu_ref, wd_ref, gf_ref = refs[:6]
    at = 6
    if has_mixer:
        o_ref, wo_ref = refs[at:at + 2]
        at += 2
    if n_proj:
        gm_ref, wp_ref = refs[at:at + 2]
        at += 2
    out_ref = refs[at]
    proj_refs = refs[at + 1:at + 1 + n_proj]
    a_ref = refs[-1]

    x = x_ref[...]
    if has_mixer:
        o = jnp.concatenate([o_ref[p] for p in range(N_PAIRS)], axis=1)
        x = x + jnp.dot(o, wo_ref[...], preferred_element_type=F32)
    hn = _rms_norm(x, g_ref[...]).astype(BF16)
    for c in range(D_FF // FF_CHUNK):
        sl = slice(c * FF_CHUNK, (c + 1) * FF_CHUNK)
        gate = jnp.dot(hn, wg_ref[:, sl], preferred_element_type=F32)
        up = jnp.dot(hn, wu_ref[:, sl], preferred_element_type=F32)
        a_ref[:, sl] = (gate * jax.nn.sigmoid(gate) * up).astype(BF16)
    y = jnp.dot(a_ref[...], wd_ref[...], preferred_element_type=F32)
    out = x + FFN_RES_WEIGHT * y
    if final:
        out = _rms_norm(out, gf_ref[...])
    out_ref[...] = out
    if n_proj:
        hp = _rms_norm(out, gm_ref[...]).astype(BF16)
        for n, p_ref in enumerate(proj_refs):
            y = jnp.dot(hp, wp_ref[:, n * D_MODEL:(n + 1) * D_MODEL], preferred_element_type=F32)
            if n == 0:
                y = y * Q_SCALE
            for p in range(N_PAIRS):
                y_pair = y[:, p * PAIR_W:(p + 1) * PAIR_W]
                if transposed[n]:
                    y_pair = y_pair.T
                p_ref[p] = y_pair.astype(BF16)


def _ffn(h, g, wg, wu, wd, gf, layer, pos, batch, seq, mixer=None, proj=None, final=False):
    ns = seq // TM

    def rows(b, i):
        return (b * ns + i, 0)

    def fixed(b, i):
        return (0, 0)

    def weights(b, i):
        return (layer, pos, 0, 0)

    pair_spec = pl.BlockSpec((None, N_PAIRS, TM, PAIR_W), lambda b, i: (b, 0, i, 0))
    inputs = [h, g, wg, wu, wd, gf]
    in_specs = [
        pl.BlockSpec((TM, D_MODEL), rows),
        _resident((1, D_MODEL), fixed),
        _resident((None, None, D_MODEL, D_FF), weights),
        _resident((None, None, D_MODEL, D_FF), weights),
        _resident((None, None, D_FF, D_MODEL), weights),
        _resident((1, D_MODEL), fixed),
    ]
    if mixer is not None:
        inputs += list(mixer)
        in_specs += [pair_spec, _resident((D_MODEL, D_MODEL), fixed)]
    out_shape = [jax.ShapeDtypeStruct(h.shape, F32)]
    out_specs = [pl.BlockSpec((TM, D_MODEL), rows)]
    transposed = ()
    if proj is not None:
        g_mix, w_proj, transposed = proj
        assert w_proj.shape[1] == len(transposed) * D_MODEL
        inputs += [g_mix, w_proj]
        in_specs += [_resident((1, D_MODEL), fixed), _resident(w_proj.shape, fixed)]
        for flag in transposed:
            if flag:
                out_shape.append(jax.ShapeDtypeStruct((batch, N_PAIRS, PAIR_W, seq), BF16))
                out_specs.append(
                    pl.BlockSpec((None, N_PAIRS, PAIR_W, TM), lambda b, i: (b, 0, 0, i)))
            else:
                out_shape.append(jax.ShapeDtypeStruct((batch, N_PAIRS, seq, PAIR_W), BF16))
                out_specs.append(pair_spec)
    return pl.pallas_call(
        functools.partial(_ffn_kernel, has_mixer=mixer is not None, transposed=tuple(transposed),
                          final=final),
        out_shape=out_shape,
        grid=(batch, ns),
        in_specs=in_specs,
        out_specs=out_specs,
        scratch_shapes=[pltpu.VMEM((TM, D_FF), BF16)],
        compiler_params=_params("parallel", "parallel"),
        name="ffn",
    )(*inputs)


def _keep_head_rows(x_t, head_in_pair):
    row = lax.broadcasted_iota(jnp.int32, x_t.shape, 0)
    own = (row >= HEAD_DIM * head_in_pair) & (row < HEAD_DIM * (head_in_pair + 1))
    return jnp.where(own, x_t, jnp.zeros_like(x_t))


def _attn_a_kernel(q_ref, *refs):
    k_refs = refs[:A_KBLOCKS]
    vt_refs = refs[A_KBLOCKS:2 * A_KBLOCKS]
    bias_ref, zero_ref, o_ref = refs[2 * A_KBLOCKS:2 * A_KBLOCKS + 3]
    s_refs = refs[2 * A_KBLOCKS + 3:-1]
    m_ref = refs[-1]
    dz = zero_ref[0]
    i = pl.program_id(2)
    pens = [jnp.where(i - (A_KBLOCKS - 1) + j >= 0, 0.0, NEG_INF).astype(F32)
            for j in range(A_KBLOCKS)]
    heads = [(pp, hh) for pp in range(A_PAIRS) for hh in range(2)]
    chunks = [slice(c * LANES, (c + 1) * LANES) for c in range(QA // LANES)]
    for (pp, hh), s_ref in zip(heads, s_refs):
        qm_t = _keep_head_rows(q_ref[pp], hh)
        for j in range(A_KBLOCKS):
            s_ref[j + dz] = jnp.dot(k_refs[j][pp], qm_t, preferred_element_type=F32)
    for (pp, hh), s_ref in zip(heads, s_refs):
        for cols in chunks:
            blk_max = []
            for j in range(A_KBLOCKS):
                x = s_ref[j + dz, :, cols] + bias_ref[2 * pp + hh, j, :, cols]
                s_ref[j + dz, :, cols] = x
                blk_max.append(x.max(axis=0, keepdims=True) + pens[j])
            m_ref[pp, hh, :, cols] = functools.reduce(jnp.maximum, blk_max)
    ones = jnp.ones((ONES_ROWS, QA), BF16)
    for pp in range(A_PAIRS):
        outs = []
        for hh in range(2):
            s_ref = s_refs[2 * pp + hh]
            pv = None
            for j in range(A_KBLOCKS):
                columns = []
                for cols in chunks:
                    shift = m_ref[pp, hh, :, cols] - pens[j]
                    pieces = [jnp.exp2(s_ref[j + dz, r:r + KSTRIP, cols] - shift).astype(BF16)
                              for r in range(0, QA, KSTRIP)]
                    columns.append(jnp.concatenate(pieces, axis=0))
                p_tile = jnp.concatenate(columns, axis=1)
                vt = vt_refs[j][pp, hh * HEAD_DIM:(hh + 1) * HEAD_DIM, :]
                part = jnp.dot(jnp.concatenate([vt, ones], axis=0), p_tile,
                               preferred_element_type=F32)
                pv = part if pv is None else pv + part
            outs.append(pv[:HEAD_DIM] / pv[HEAD_DIM:HEAD_DIM + 1])
        o_ref[pp] = jnp.concatenate(outs, axis=0).T.astype(BF16)


def _attn_a(q, k, vt, bias_t):
    batch, _, seq, _ = k.shape
    nq = seq // QA

    def k_spec(j):
        back = A_KBLOCKS - 1 - j
        return pl.BlockSpec((None, A_PAIRS, QA, PAIR_W),
                            lambda b, p, i: (b, p, jnp.maximum(i - back, 0), 0))

    def vt_spec(j):
        back = A_KBLOCKS - 1 - j
        return pl.BlockSpec((None, A_PAIRS, PAIR_W, QA),
                            lambda b, p, i: (b, p, 0, jnp.maximum(i - back, 0)))

    return pl.pallas_call(
        _attn_a_kernel,
        out_shape=jax.ShapeDtypeStruct(k.shape, BF16),
        grid=(batch, N_PAIRS // A_PAIRS, nq),
        in_specs=(
            [pl.BlockSpec((None, A_PAIRS, PAIR_W, QA), lambda b, p, i: (b, p, 0, i))]
            + [k_spec(j) for j in range(A_KBLOCKS)]
            + [vt_spec(j) for j in range(A_KBLOCKS)]
            + [_resident((N_HEADS, A_KBLOCKS, QA, QA), lambda b, p, i: (0, 0, 0, 0))]
            + [pl.BlockSpec(memory_space=pltpu.SMEM)]
        ),
        out_specs=pl.BlockSpec((None, A_PAIRS, QA, PAIR_W), lambda b, p, i: (b, p, i, 0)),
        scratch_shapes=(
            [pltpu.VMEM((A_KBLOCKS, QA, QA), F32)] * (2 * A_PAIRS)
            + [pltpu.VMEM((A_PAIRS, 2, 1, QA), F32)]
        ),
        compiler_params=_params("parallel", "parallel", "arbitrary"),
        name="attn_a",
    )(q, *([k] * A_KBLOCKS), *([vt] * A_KBLOCKS), bias_t, jnp.zeros((1,), jnp.int32))


def _attn_a_bias_table(rel_bias):
    n_heads = rel_bias.shape[0]
    back = (A_KBLOCKS - 1) * QA
    keys = A_KBLOCKS * QA
    d = back + (QA - 1) - jnp.arange(keys + QA - 1)
    f = jnp.take(rel_bias.astype(F32), jnp.clip(d, -REL_CLIP, REL_CLIP) + REL_CLIP, axis=1)
    period = keys + QA
    g = jnp.pad(f, ((0, 0), (0, period - f.shape[1])))
    skew = jnp.tile(g, (1, QA))[:, :QA * (period - 1)].reshape(n_heads, QA, period - 1)
    toeplitz = skew[:, :, QA - 1:QA - 1 + keys]
    r = jnp.arange(QA)[:, None]
    x = jnp.arange(keys)[None, :]
    dchunk = (r + back) // CHUNK - x // CHUNK
    valid = (dchunk >= 0) & (dchunk <= LEFT_CHUNKS)
    table = jnp.where(valid[None], toeplitz * LOG2E, NEG_INF)
    return table.reshape(n_heads, QA, A_KBLOCKS, QA).transpose(0, 2, 3, 1)


def _split3(x):
    hi = x.astype(BF16)
    r1 = x - hi.astype(F32)
    mid = r1.astype(BF16)
    lo = (r1 - mid.astype(F32)).astype(BF16)
    return hi, mid, lo


def _kvf_kernel(x_ref, g_ref, wkv_ref, wf_ref, bf_ref, tri_ref, selq_ref, selk_ref, oneq_ref,
                onek_ref, k_ref, vt_ref, qe_ref, ke_ref, carry_ref):
    @pl.when(pl.program_id(1) == 0)
    def _():
        carry_ref[...] = jnp.zeros_like(carry_ref)

    hn = _rms_norm(x_ref[...], g_ref[...]).astype(BF16)
    yk = jnp.dot(hn, wkv_ref[:, :D_MODEL], preferred_element_type=F32)
    yv = jnp.dot(hn, wkv_ref[:, D_MODEL:], preferred_element_type=F32)
    for p in range(N_PAIRS):
        k_ref[p] = yk[:, p * PAIR_W:(p + 1) * PAIR_W].astype(BF16)
        vt_ref[p] = yv[:, p * PAIR_W:(p + 1) * PAIR_W].T.astype(BF16)
    z = jnp.dot(hn, wf_ref[...], preferred_element_type=F32) + bf_ref[...]
    log_f = jnp.minimum(z, 0.0) - jnp.log1p(jnp.exp(-jnp.abs(z)))
    tri = tri_ref[...]
    cum = sum(jnp.dot(tri, piece, preferred_element_type=F32) for piece in _split3(log_f))
    cum = cum + carry_ref[...]
    carry_ref[...] = cum[TM - 1:TM, :]
    pieces = jnp.concatenate(_split3(cum * LOG2E), axis=1)
    qe_ref[...] = (jnp.dot(pieces, selq_ref[...], preferred_element_type=F32)
                   + oneq_ref[...]).T.astype(BF16)
    ke_ref[...] = (jnp.dot(pieces, selk_ref[...], preferred_element_type=F32)
                   + onek_ref[...]).astype(BF16)


def _kvf(h, g, wkv, wf, bf, batch, seq):
    ns = seq // TM
    tri = (jnp.arange(TM)[:, None] >= jnp.arange(TM)[None, :]).astype(BF16)
    half = EXT_PER_HEAD // 2
    src = jnp.arange(3 * EXT_W)[:, None]
    dst = jnp.arange(EXT_W)[None, :]
    src_head, src_piece = src % EXT_W, src // EXT_W
    dst_head, dst_slot = dst // EXT_PER_HEAD, dst % EXT_PER_HEAD
    same_head = (src_head == dst_head) & (src_head < N_HEADS)
    sel_q = (same_head & (dst_slot == half + src_piece)).astype(BF16)
    sel_k = -(same_head & (dst_slot == src_piece)).astype(BF16)
    live = jnp.arange(EXT_W) < N_HEADS * EXT_PER_HEAD
    slot = jnp.arange(EXT_W) % EXT_PER_HEAD
    one_q = (live & (slot < half)).astype(F32).reshape(1, EXT_W)
    one_k = (live & (slot >= half)).astype(F32).reshape(1, EXT_W)

    k_sds = jax.ShapeDtypeStruct((batch, N_PAIRS, seq, PAIR_W), BF16)
    k_spec = pl.BlockSpec((None, N_PAIRS, TM, PAIR_W), lambda b, i: (b, 0, i, 0))
    vt_sds = jax.ShapeDtypeStruct((batch, N_PAIRS, ns, PAIR_W, TM), BF16)
    vt_spec = pl.BlockSpec((None, N_PAIRS, None, PAIR_W, TM), lambda b, i: (b, 0, i, 0, 0))
    ext_sds = jax.ShapeDtypeStruct((batch, seq, EXT_W), BF16)
    ext_spec = pl.BlockSpec((None, TM, EXT_W), lambda b, i: (b, i, 0))
    ext_t_sds = jax.ShapeDtypeStruct((batch, EXT_W, seq), BF16)
    ext_t_spec = pl.BlockSpec((None, EXT_W, TM), lambda b, i: (b, 0, i))
    return pl.pallas_call(
        _kvf_kernel,
        out_shape=[k_sds, vt_sds, ext_t_sds, ext_sds],
        grid=(batch, ns),
        in_specs=[
            pl.BlockSpec((TM, D_MODEL), lambda b, i: (b * ns + i, 0)),
            _resident((1, D_MODEL), lambda b, i: (0, 0)),
            _resident((D_MODEL, 2 * D_MODEL), lambda b, i: (0, 0)),
            _resident((D_MODEL, EXT_W), lambda b, i: (0, 0)),
            _resident((1, EXT_W), lambda b, i: (0, 0)),
            _resident((TM, TM), lambda b, i: (0, 0)),
            _resident((3 * EXT_W, EXT_W), lambda b, i: (0, 0)),
            _resident((3 * EXT_W, EXT_W), lambda b, i: (0, 0)),
            _resident((1, EXT_W), lambda b, i: (0, 0)),
            _resident((1, EXT_W), lambda b, i: (0, 0)),
        ],
        out_specs=[k_spec, vt_spec, ext_t_spec, ext_spec],
        scratch_shapes=[pltpu.VMEM((1, EXT_W), F32)],
        compiler_params=_params("parallel", "arbitrary"),
        name="kvf",
    )(h, g, wkv, wf, bf, tri, sel_q, sel_k, one_q, one_k)


def _fox_kernel(q_ref, k_ref, vt_ref, qe_ref, ke_ref, zero_ref, o_ref, qa_ref, *scratch):
    s_refs = scratch[:-3]
    m_ref, alpha_ref, acc_ref = scratch[-3:]
    dz = zero_ref[0]
    group = pl.program_id(1)
    i = pl.program_id(2)
    heads = [(pp, hh) for pp in range(FOX_PAIRS) for hh in range(2)]
    qe_t = qe_ref[...]
    ext_row = lax.broadcasted_iota(jnp.int32, qe_t.shape, 0)
    for slot, (pp, hh) in enumerate(heads):
        head = 2 * (group * FOX_PAIRS + pp) + hh
        own = (ext_row >= EXT_PER_HEAD * head) & (ext_row < EXT_PER_HEAD * (head + 1))
        qa_ref[slot] = jnp.concatenate(
            [_keep_head_rows(q_ref[pp], hh), jnp.where(own, qe_t, jnp.zeros_like(qe_t))], axis=0)

    half = QB // 2

    n_chunks = QB // LANES

    def key_block(blk, diagonal):
        off = pl.multiple_of(blk * QB, QB)
        k_ext = ke_ref[pl.ds(off, QB), :]
        for slot, (pp, hh) in enumerate(heads):
            ka = jnp.concatenate([k_ref[pp, pl.ds(off, QB), :], k_ext], axis=1)
            if diagonal:
                s_refs[slot][dz, :half, :half] = jnp.dot(
                    ka[:half], qa_ref[slot, :, :half], preferred_element_type=F32)
                s_refs[slot][dz, :, half:] = jnp.dot(
                    ka, qa_ref[slot, :, half:], preferred_element_type=F32)
            else:
                s_refs[slot][dz] = jnp.dot(ka, qa_ref[slot], preferred_element_type=F32)

        def live_keys(c):
            return (c + 1) * LANES if diagonal else QB

        for slot in range(len(heads)):
            for c in range(n_chunks):
                cols = slice(c * LANES, (c + 1) * LANES)
                parts = []
                n_full = c * LANES if diagonal else QB
                if n_full:
                    parts.append(s_refs[slot][dz, :n_full, cols].max(axis=0, keepdims=True))
                if diagonal:
                    tile = slice(c * LANES, (c + 1) * LANES)
                    key_id = lax.broadcasted_iota(jnp.int32, (LANES, LANES), 0)
                    qry_id = lax.broadcasted_iota(jnp.int32, (LANES, LANES), 1)
                    x = jnp.where(key_id <= qry_id, s_refs[slot][dz, tile, cols], NEG_INF)
                    s_refs[slot][dz, tile, cols] = x
                    parts.append(x.max(axis=0, keepdims=True))
                blk_max = functools.reduce(jnp.maximum, parts)
                if diagonal:
                    m_ref[slot, :, cols] = blk_max
                else:
                    m_old = m_ref[slot, :, cols]
                    m_new = jnp.maximum(m_old, blk_max)
                    m_ref[slot, :, cols] = m_new
                    alpha_ref[slot, :, cols] = jnp.exp2(m_old - m_new)
        chunks_per_tile = MXU_TILE // LANES
        ones = jnp.ones((ONES_ROWS, QB), BF16)
        for slot, (pp, hh) in enumerate(heads):
            vt = vt_ref[pp, blk, hh * HEAD_DIM:(hh + 1) * HEAD_DIM, :]
            vt = jnp.concatenate([vt, ones], axis=0)
            for nt in range(QB // MXU_TILE):
                tile_cols = slice(nt * MXU_TILE, (nt + 1) * MXU_TILE)
                n_key_tiles = (nt + 1) if diagonal else QB // MXU_TILE
                pv = None
                for kt in range(n_key_tiles):
                    columns = []
                    for c in range(nt * chunks_per_tile, (nt + 1) * chunks_per_tile):
                        cols = slice(c * LANES, (c + 1) * LANES)
                        m_new = m_ref[slot, :, cols]
                        pieces = []
                        for r in range(kt * MXU_TILE, (kt + 1) * MXU_TILE, KSTRIP):
                            if r < live_keys(c):
                                pr = jnp.exp2(s_refs[slot][dz, r:r + KSTRIP, cols] - m_new)
                                pieces.append(pr.astype(BF16))
                            else:
                                pieces.append(jnp.zeros((KSTRIP, LANES), BF16))
                        columns.append(jnp.concatenate(pieces, axis=0))
                    p_tile = jnp.concatenate(columns, axis=1)
                    part = jnp.dot(vt[:, kt * MXU_TILE:(kt + 1) * MXU_TILE], p_tile,
                                   preferred_element_type=F32)
                    pv = part if pv is None else pv + part
                if diagonal:
                    acc_ref[slot, :, tile_cols] = pv
                else:
                    acc_ref[slot, :, tile_cols] = (
                        alpha_ref[slot, :, tile_cols] * acc_ref[slot, :, tile_cols] + pv)

    def body(j, carry):
        key_block(j, False)
        return carry

    key_block(i, True)
    lax.fori_loop(0, i, body, 0)
    for pp in range(FOX_PAIRS):
        o_t = jnp.concatenate(
            [acc_ref[2 * pp + hh, :HEAD_DIM, :] / acc_ref[2 * pp + hh, HEAD_DIM:HEAD_DIM + 1, :]
             for hh in range(2)], axis=0)
        o_ref[pp] = o_t.T.astype(BF16)


def _fox(q, k, vt, q_ext, k_ext):
    batch, _, seq, _ = k.shape
    nq = seq // QB
    n_heads = 2 * FOX_PAIRS
    assert vt.shape == (batch, N_PAIRS, nq, PAIR_W, QB)
    return pl.pallas_call(
        _fox_kernel,
        out_shape=jax.ShapeDtypeStruct(k.shape, BF16),
        grid=(batch, N_PAIRS // FOX_PAIRS, nq),
        in_specs=[
            pl.BlockSpec((None, FOX_PAIRS, PAIR_W, QB), lambda b, p, i: (b, p, 0, i)),
            _resident((None, FOX_PAIRS, seq, PAIR_W), lambda b, p, i: (b, p, 0, 0)),
            _resident((None, FOX_PAIRS, nq, PAIR_W, QB), lambda b, p, i: (b, p, 0, 0, 0)),
            pl.BlockSpec((None, EXT_W, QB), lambda b, p, i: (b, 0, i)),
            _resident((None, seq, EXT_W), lambda b, p, i: (b, 0, 0)),
            pl.BlockSpec(memory_space=pltpu.SMEM),
        ],
        out_specs=pl.BlockSpec((None, FOX_PAIRS, QB, PAIR_W), lambda b, p, i: (b, p, i, 0)),
        scratch_shapes=(
            [pltpu.VMEM((n_heads, PAIR_W + EXT_W, QB), BF16)]
            + [pltpu.VMEM((1, QB, QB), F32)] * n_heads
            + [pltpu.VMEM((n_heads, 1, QB), F32),
               pltpu.VMEM((n_heads, 1, QB), F32),
               pltpu.VMEM((n_heads, HEAD_DIM + ONES_ROWS, QB), F32)]
        ),
        compiler_params=_params("parallel", "parallel", "arbitrary"),
        name="fox",
    )(q, k, vt, q_ext, k_ext, jnp.zeros((1,), jnp.int32))


def kernel(x, ffn_norm, ffn_w_gate, ffn_w_up, ffn_w_down, mix_norm, a_w_qkv, a_w_o, a_rel_bias,
           kv_norm, b_w_kvf, b_f_bias, b_w_q, b_w_o, final_norm):
    batch, seq, _ = x.shape
    depth = ffn_norm.shape[0]
    n_a = a_w_qkv.shape[0]
    n_b = b_w_q.shape[0]
    assert seq % TM == 0 and seq % QA == 0 and seq % QB == 0 and n_a + n_b == depth

    wg = ffn_w_gate.astype(BF16)
    wu = ffn_w_up.astype(BF16)
    wd = ffn_w_down.astype(BF16)
    gf = final_norm.reshape(1, D_MODEL)

    def half_ffn(h, layer, pos, **fused):
        return _ffn(h, ffn_norm[layer, pos].reshape(1, D_MODEL), wg, wu, wd, gf, layer, pos,
                    batch, seq, **fused)

    h = x.reshape(batch * seq, D_MODEL)
    for layer in range(n_a):
        g_mix = mix_norm[layer].reshape(1, D_MODEL)
        h, qt, k, vt = half_ffn(
            h, layer, 0, proj=(g_mix, a_w_qkv[layer].astype(BF16), (True, False, True)))
        o = _attn_a(qt, k, vt, _attn_a_bias_table(a_rel_bias[layer]))
        (h,) = half_ffn(h, layer, 1, mixer=(o, a_w_o[layer].astype(BF16)))

    wkv = b_w_kvf[:, :2 * D_MODEL].astype(BF16)
    wf = jnp.pad(b_w_kvf[:, 2 * D_MODEL:], ((0, 0), (0, EXT_W - N_HEADS))).astype(BF16)
    bf = jnp.pad(b_f_bias, (0, EXT_W - N_HEADS)).reshape(1, EXT_W).astype(F32)
    k_sh, vt_sh, q_ext_t, k_ext = _kvf(h, kv_norm.reshape(1, D_MODEL), wkv, wf, bf, batch, seq)

    for lb in range(n_b):
        layer = n_a + lb
        g_mix = mix_norm[layer].reshape(1, D_MODEL)
        h, qt = half_ffn(h, layer, 0, proj=(g_mix, b_w_q[lb].astype(BF16), (True,)))
        o = _fox(qt, k_sh, vt_sh, q_ext_t, k_ext)
        (h,) = half_ffn(h, layer, 1, mixer=(o, b_w_o[lb].astype(BF16)),
                        final=(layer == depth - 1))

    return h.reshape(batch, seq, D_MODEL)
```

```python
import functools

import jax
import jax.numpy as jnp
from jax import lax
from jax.experimental import pallas as pl
from jax.experimental.pallas import tpu as pltpu

D_MODEL = 1024
N_HEADS = 16
HEAD_DIM = 64
N_PAIRS = N_HEADS // 2
PAIR_W = 2 * HEAD_DIM
D_FF = 2816
CHUNK = 64
LEFT_CHUNKS = 8
REL_CLIP = 256
EPS = 1e-6
NEG_INF = -1e30
ATTN_SCALE = HEAD_DIM ** -0.5
LOG2E = 1.4426950408889634
Q_SCALE = ATTN_SCALE * LOG2E
FFN_RES_WEIGHT = 0.5

TM = 512
FF_CHUNK = 256
QA = 256
A_KBLOCKS = LEFT_CHUNKS * CHUNK // QA + 1
A_PAIRS = N_PAIRS
QB = 512
FOX_PAIRS = N_PAIRS
EXT_W = 128
EXT_PER_HEAD = 6
LANES = 128
MXU_TILE = 256
KSTRIP = 128
ONES_ROWS = 16
VMEM_LIMIT = 56 * 1024 * 1024

F32 = jnp.float32
BF16 = jnp.bfloat16


def _rms_norm(x, g):
    y = x * lax.rsqrt(jnp.mean(x * x, axis=-1, keepdims=True) + EPS)
    return y * g


def _resident(block_shape, index_map):
    return pl.BlockSpec(block_shape, index_map, pipeline_mode=pl.Buffered(1))


def _params(*semantics):
    return pltpu.CompilerParams(dimension_semantics=semantics, vmem_limit_bytes=VMEM_LIMIT)


def _ffn_kernel(*refs, has_mixer, transposed, final):
    n_proj = len(transposed)
    x_ref, g_ref, wg_ref, wu_ref, wd_ref, gf_ref = refs[:6]
    at = 6
    if has_mixer:
        o_ref, wo_ref = refs[at:at + 2]
        at += 2
    if n_proj:
        gm_ref, wp_ref = refs[at:at + 2]
        at += 2
    out_ref = refs[at]
    proj_refs = refs[at + 1:at + 1 + n_proj]
    a_ref = refs[-1]

    for sub in range(a_ref.shape[0]):
        rows = slice(sub * TM, (sub + 1) * TM)
        x = x_ref[rows, :]
        if has_mixer:
            o = jnp.concatenate([o_ref[p, rows, :] for p in range(N_PAIRS)], axis=1)
            x = x + jnp.dot(o, wo_ref[...], preferred_element_type=F32)
        hn = _rms_norm(x, g_ref[...]).astype(BF16)
        for c in range(D_FF // FF_CHUNK):
            sl = slice(c * FF_CHUNK, (c + 1) * FF_CHUNK)
            gate = jnp.dot(hn, wg_ref[:, sl], preferred_element_type=F32)
            up = jnp.dot(hn, wu_ref[:, sl], preferred_element_type=F32)
            a_ref[sub, :, sl] = (gate * jax.nn.sigmoid(gate) * up).astype(BF16)
        y = jnp.dot(a_ref[sub], wd_ref[...], preferred_element_type=F32)
        out = x + FFN_RES_WEIGHT * y
        if final:
            out = _rms_norm(out, gf_ref[...])
        out_ref[rows, :] = out
        if n_proj:
            hp = _rms_norm(out, gm_ref[...]).astype(BF16)
            for n, p_ref in enumerate(proj_refs):
                y = jnp.dot(hp, wp_ref[:, n * D_MODEL:(n + 1) * D_MODEL],
                            preferred_element_type=F32)
                if n == 0:
                    y = y * Q_SCALE
                for p in range(N_PAIRS):
                    y_pair = y[:, p * PAIR_W:(p + 1) * PAIR_W]
                    if transposed[n]:
                        p_ref[p, :, rows] = y_pair.T.astype(BF16)
                    else:
                        p_ref[p, rows, :] = y_pair.astype(BF16)


def _ffn_sub_tiles(n_side):
    def estimate(n_sub):
        rows = n_sub * TM
        stream = 2 * 2 * rows * D_MODEL * 4
        stream += 2 * n_side * rows * D_MODEL * 2
        resident = (3 * D_FF + n_side * D_MODEL) * D_MODEL * 2
        scratch = rows * D_FF * 2
        live = 4 * TM * D_MODEL * 4
        return stream + resident + scratch + live

    return 2 if estimate(2) <= VMEM_LIMIT else 1


def _ffn(h, g, wg, wu, wd, gf, layer, pos, batch, seq, mixer=None, proj=None, final=False):
    n_side = (0 if proj is None else len(proj[2])) + (mixer is not None)
    n_sub = _ffn_sub_tiles(n_side)
    tm = n_sub * TM
    ns = seq // tm

    def rows(b, i):
        return (b * ns + i, 0)

    def fixed(b, i):
        return (0, 0)

    def weights(b, i):
        return (layer, pos, 0, 0)

    pair_spec = pl.BlockSpec((None, N_PAIRS, tm, PAIR_W), lambda b, i: (b, 0, i, 0))
    inputs = [h, g, wg, wu, wd, gf]
    in_specs = [
        pl.BlockSpec((tm, D_MODEL), rows),
        _resident((1, D_MODEL), fixed),
        _resident((None, None, D_MODEL, D_FF), weights),
        _resident((None, None, D_MODEL, D_FF), weights),
        _resident((None, None, D_FF, D_MODEL), weights),
        _resident((1, D_MODEL), fixed),
    ]
    if mixer is not None:
        inputs += list(mixer)
        in_specs += [pair_spec, _resident((D_MODEL, D_MODEL), fixed)]
    out_shape = [jax.ShapeDtypeStruct(h.shape, F32)]
    out_specs = [pl.BlockSpec((tm, D_MODEL), rows)]
    transposed = ()
    if proj is not None:
        g_mix, w_proj, transposed = proj
        assert w_proj.shape[1] == len(transposed) * D_MODEL
        inputs += [g_mix, w_proj]
        in_specs += [_resident((1, D_MODEL), fixed), _resident(w_proj.shape, fixed)]
        for flag in transposed:
            if flag:
                out_shape.append(jax.ShapeDtypeStruct((batch, N_PAIRS, PAIR_W, seq), BF16))
                out_specs.append(
                    pl.BlockSpec((None, N_PAIRS, PAIR_W, tm), lambda b, i: (b, 0, 0, i)))
            else:
                out_shape.append(jax.ShapeDtypeStruct((batch, N_PAIRS, seq, PAIR_W), BF16))
                out_specs.append(pair_spec)
    return pl.pallas_call(
        functools.partial(_ffn_kernel, has_mixer=mixer is not None, transposed=tuple(transposed),
                          final=final),
        out_shape=out_shape,
        grid=(batch, ns),
        in_specs=in_specs,
        out_specs=out_specs,
        scratch_shapes=[pltpu.VMEM((n_sub, TM, D_FF), BF16)],
        compiler_params=_params("parallel", "parallel"),
        name="ffn",
    )(*inputs)


def _keep_head_rows(x_t, head_in_pair):
    row = lax.broadcasted_iota(jnp.int32, x_t.shape, 0)
    own = (row >= HEAD_DIM * head_in_pair) & (row < HEAD_DIM * (head_in_pair + 1))
    return jnp.where(own, x_t, jnp.zeros_like(x_t))


def _attn_a_kernel(q_ref, *refs):
    k_refs = refs[:A_KBLOCKS]
    vt_refs = refs[A_KBLOCKS:2 * A_KBLOCKS]
    bias_ref, zero_ref, o_ref = refs[2 * A_KBLOCKS:2 * A_KBLOCKS + 3]
    s_refs = refs[2 * A_KBLOCKS + 3:-1]
    m_ref = refs[-1]
    dz = zero_ref[0]
    i = pl.program_id(2)
    pens = [jnp.where(i - (A_KBLOCKS - 1) + j >= 0, 0.0, NEG_INF).astype(F32)
            for j in range(A_KBLOCKS)]
    heads = [(pp, hh) for pp in range(A_PAIRS) for hh in range(2)]
    chunks = [slice(c * LANES, (c + 1) * LANES) for c in range(QA // LANES)]
    for (pp, hh), s_ref in zip(heads, s_refs):
        qm_t = _keep_head_rows(q_ref[pp], hh)
        for j in range(A_KBLOCKS):
            s_ref[j + dz] = jnp.dot(k_refs[j][pp], qm_t, preferred_element_type=F32)
    for (pp, hh), s_ref in zip(heads, s_refs):
        for cols in chunks:
            blk_max = []
            for j in range(A_KBLOCKS):
                x = s_ref[j + dz, :, cols] + bias_ref[2 * pp + hh, j, :, cols]
                s_ref[j + dz, :, cols] = x
                blk_max.append(x.max(axis=0, keepdims=True) + pens[j])
            m_ref[pp, hh, :, cols] = functools.reduce(jnp.maximum, blk_max)
    ones = jnp.ones((ONES_ROWS, QA), BF16)
    for pp in range(A_PAIRS):
        outs = []
        for hh in range(2):
            s_ref = s_refs[2 * pp + hh]
            pv = None
            for j in range(A_KBLOCKS):
                columns = []
                for cols in chunks:
                    shift = m_ref[pp, hh, :, cols] - pens[j]
                    pieces = [jnp.exp2(s_ref[j + dz, r:r + KSTRIP, cols] - shift).astype(BF16)
                              for r in range(0, QA, KSTRIP)]
                    columns.append(jnp.concatenate(pieces, axis=0))
                p_tile = jnp.concatenate(columns, axis=1)
                vt = vt_refs[j][pp, hh * HEAD_DIM:(hh + 1) * HEAD_DIM, :]
                part = jnp.dot(jnp.concatenate([vt, ones], axis=0), p_tile,
                               preferred_element_type=F32)
                pv = part if pv is None else pv + part
            outs.append(pv[:HEAD_DIM] / pv[HEAD_DIM:HEAD_DIM + 1])
        o_ref[pp] = jnp.concatenate(outs, axis=0).T.astype(BF16)


def _attn_a(q, k, vt, bias_t):
    batch, _, seq, _ = k.shape
    nq = seq // QA

    def k_spec(j):
        back = A_KBLOCKS - 1 - j
        return pl.BlockSpec((None, A_PAIRS, QA, PAIR_W),
                            lambda b, p, i: (b, p, jnp.maximum(i - back, 0), 0))

    def vt_spec(j):
        back = A_KBLOCKS - 1 - j
        return pl.BlockSpec((None, A_PAIRS, PAIR_W, QA),
                            lambda b, p, i: (b, p, 0, jnp.maximum(i - back, 0)))

    return pl.pallas_call(
        _attn_a_kernel,
        out_shape=jax.ShapeDtypeStruct(k.shape, BF16),
        grid=(batch, N_PAIRS // A_PAIRS, nq),
        in_specs=(
            [pl.BlockSpec((None, A_PAIRS, PAIR_W, QA), lambda b, p, i: (b, p, 0, i))]
            + [k_spec(j) for j in range(A_KBLOCKS)]
            + [vt_spec(j) for j in range(A_KBLOCKS)]
            + [_resident((N_HEADS, A_KBLOCKS, QA, QA), lambda b, p, i: (0, 0, 0, 0))]
            + [pl.BlockSpec(memory_space=pltpu.SMEM)]
        ),
        out_specs=pl.BlockSpec((None, A_PAIRS, QA, PAIR_W), lambda b, p, i: (b, p, i, 0)),
        scratch_shapes=(
            [pltpu.VMEM((A_KBLOCKS, QA, QA), F32)] * (2 * A_PAIRS)
            + [pltpu.VMEM((A_PAIRS, 2, 1, QA), F32)]
        ),
        compiler_params=_params("parallel", "parallel", "arbitrary"),
        name="attn_a",
    )(q, *([k] * A_KBLOCKS), *([vt] * A_KBLOCKS), bias_t, jnp.zeros((1,), jnp.int32))


def _bias_table_kernel(vec_ref, out_ref):
    back = (A_KBLOCKS - 1) * QA
    key = lax.broadcasted_iota(jnp.int32, (QA, QA), 0)
    qry = lax.broadcasted_iota(jnp.int32, (QA, QA), 1)
    chunk_shift = CHUNK.bit_length() - 1
    for j in range(A_KBLOCKS):
        rolled = pltpu.roll(jnp.broadcast_to(vec_ref[j], (QA, 2 * QA)), 0, 1,
                            stride=1, stride_axis=0)
        dchunk = (lax.shift_right_logical(qry + back, chunk_shift)
                  - lax.shift_right_logical(key + j * QA, chunk_shift))
        valid = (dchunk >= 0) & (dchunk <= LEFT_CHUNKS)
        out_ref[j] = jnp.where(valid, rolled[:, QA:], NEG_INF)


def _attn_a_bias_table(rel_bias):
    n_heads = rel_bias.shape[0]
    back = (A_KBLOCKS - 1) * QA
    d = (back - QA * jnp.arange(A_KBLOCKS))[:, None] + jnp.arange(2 * QA)[None, :] - QA
    vec = jnp.take(rel_bias.astype(F32) * LOG2E, jnp.clip(d, -REL_CLIP, REL_CLIP) + REL_CLIP,
                   axis=1)
    vec = vec.reshape(n_heads, A_KBLOCKS, 1, 2 * QA)
    return pl.pallas_call(
        _bias_table_kernel,
        out_shape=jax.ShapeDtypeStruct((n_heads, A_KBLOCKS, QA, QA), F32),
        grid=(n_heads,),
        in_specs=[pl.BlockSpec((None, A_KBLOCKS, 1, 2 * QA), lambda h: (h, 0, 0, 0))],
        out_specs=pl.BlockSpec((None, A_KBLOCKS, QA, QA), lambda h: (h, 0, 0, 0)),
        compiler_params=_params("parallel"),
        name="bias_table",
    )(vec)


def _split3(x):
    hi = x.astype(BF16)
    r1 = x - hi.astype(F32)
    mid = r1.astype(BF16)
    lo = (r1 - mid.astype(F32)).astype(BF16)
    return hi, mid, lo


def _kvf_kernel(x_ref, g_ref, w_ref, bf_ref, tri_ref, sel_ref, one_ref,
                k_ref, vt_ref, qe_ref, ke_ref, carry_ref):
    @pl.when(pl.program_id(1) == 0)
    def _():
        carry_ref[...] = jnp.zeros_like(carry_ref)

    hn = _rms_norm(x_ref[...], g_ref[...]).astype(BF16)
    y = jnp.dot(hn, w_ref[...], preferred_element_type=F32)
    for p in range(N_PAIRS):
        k_cols = slice(p * PAIR_W, (p + 1) * PAIR_W)
        v_cols = slice(D_MODEL + p * PAIR_W, D_MODEL + (p + 1) * PAIR_W)
        k_ref[p] = y[:, k_cols].astype(BF16)
        vt_ref[p] = y[:, v_cols].T.astype(BF16)
    z = y[:, 2 * D_MODEL:] + bf_ref[...]
    log_f = jnp.minimum(z, 0.0) - jnp.log1p(jnp.exp(-jnp.abs(z)))
    cum3 = jnp.dot(tri_ref[...], jnp.concatenate(_split3(log_f), axis=1),
                   preferred_element_type=F32)
    cum = (cum3[:, :EXT_W] + cum3[:, EXT_W:2 * EXT_W] + cum3[:, 2 * EXT_W:]) + carry_ref[...]
    carry_ref[...] = cum[TM - 1:TM, :]
    pieces = jnp.concatenate(_split3(cum * LOG2E), axis=1)
    ext = jnp.dot(pieces, sel_ref[...], preferred_element_type=F32) + one_ref[...]
    qe_ref[...] = ext[:, :EXT_W].T.astype(BF16)
    ke_ref[...] = ext[:, EXT_W:].astype(BF16)


def _kvf(h, g, w, bf, batch, seq):
    ns = seq // TM
    tri = (jnp.arange(TM)[:, None] >= jnp.arange(TM)[None, :]).astype(BF16)
    half = EXT_PER_HEAD // 2
    src = jnp.arange(3 * EXT_W)[:, None]
    dst = jnp.arange(EXT_W)[None, :]
    src_head, src_piece = src % EXT_W, src // EXT_W
    dst_head, dst_slot = dst // EXT_PER_HEAD, dst % EXT_PER_HEAD
    same_head = (src_head == dst_head) & (src_head < N_HEADS)
    sel_q = (same_head & (dst_slot == half + src_piece)).astype(BF16)
    sel_k = -(same_head & (dst_slot == src_piece)).astype(BF16)
    live = jnp.arange(EXT_W) < N_HEADS * EXT_PER_HEAD
    slot = jnp.arange(EXT_W) % EXT_PER_HEAD
    one_q = (live & (slot < half)).astype(F32).reshape(1, EXT_W)
    one_k = (live & (slot >= half)).astype(F32).reshape(1, EXT_W)
    sel = jnp.concatenate([sel_q, sel_k], axis=1)
    one = jnp.concatenate([one_q, one_k], axis=1)

    k_sds = jax.ShapeDtypeStruct((batch, N_PAIRS, seq, PAIR_W), BF16)
    k_spec = pl.BlockSpec((None, N_PAIRS, TM, PAIR_W), lambda b, i: (b, 0, i, 0))
    vt_sds = jax.ShapeDtypeStruct((batch, N_PAIRS, ns, PAIR_W, TM), BF16)
    vt_spec = pl.BlockSpec((None, N_PAIRS, None, PAIR_W, TM), lambda b, i: (b, 0, i, 0, 0))
    ext_sds = jax.ShapeDtypeStruct((batch, seq, EXT_W), BF16)
    ext_spec = pl.BlockSpec((None, TM, EXT_W), lambda b, i: (b, i, 0))
    ext_t_sds = jax.ShapeDtypeStruct((batch, EXT_W, seq), BF16)
    ext_t_spec = pl.BlockSpec((None, EXT_W, TM), lambda b, i: (b, 0, i))
    return pl.pallas_call(
        _kvf_kernel,
        out_shape=[k_sds, vt_sds, ext_t_sds, ext_sds],
        grid=(batch, ns),
        in_specs=[
            pl.BlockSpec((TM, D_MODEL), lambda b, i: (b * ns + i, 0)),
            _resident((1, D_MODEL), lambda b, i: (0, 0)),
            _resident((D_MODEL, 2 * D_MODEL + EXT_W), lambda b, i: (0, 0)),
            _resident((1, EXT_W), lambda b, i: (0, 0)),
            _resident((TM, TM), lambda b, i: (0, 0)),
            _resident((3 * EXT_W, 2 * EXT_W), lambda b, i: (0, 0)),
            _resident((1, 2 * EXT_W), lambda b, i: (0, 0)),
        ],
        out_specs=[k_spec, vt_spec, ext_t_spec, ext_spec],
        scratch_shapes=[pltpu.VMEM((1, EXT_W), F32)],
        compiler_params=_params("parallel", "arbitrary"),
        name="kvf",
    )(h, g, w, bf, tri, sel, one)


def _fox_kernel(q_ref, k_ref, vt_ref, qe_ref, ke_ref, zero_ref, o_ref, qa_ref, *scratch):
    s_refs = scratch[:-3]
    m_ref, alpha_ref, acc_ref = scratch[-3:]
    dz = zero_ref[0]
    group = pl.program_id(1)
    i = pl.program_id(2)
    heads = [(pp, hh) for pp in range(FOX_PAIRS) for hh in range(2)]
    qe_t = qe_ref[...]
    ext_row = lax.broadcasted_iota(jnp.int32, qe_t.shape, 0)
    for slot, (pp, hh) in enumerate(heads):
        head = 2 * (group * FOX_PAIRS + pp) + hh
        own = (ext_row >= EXT_PER_HEAD * head) & (ext_row < EXT_PER_HEAD * (head + 1))
        qa_ref[slot] = jnp.concatenate(
            [_keep_head_rows(q_ref[pp], hh), jnp.where(own, qe_t, jnp.zeros_like(qe_t))], axis=0)

    half = QB // 2

    n_chunks = QB // LANES

    def key_block(blk, diagonal):
        off = pl.multiple_of(blk * QB, QB)
        k_ext = ke_ref[pl.ds(off, QB), :]
        for slot, (pp, hh) in enumerate(heads):
            ka = jnp.concatenate([k_ref[pp, pl.ds(off, QB), :], k_ext], axis=1)
            if diagonal:
                s_refs[slot][dz, :half, :half] = jnp.dot(
                    ka[:half], qa_ref[slot, :, :half], preferred_element_type=F32)
                s_refs[slot][dz, :, half:] = jnp.dot(
                    ka, qa_ref[slot, :, half:], preferred_element_type=F32)
            else:
                s_refs[slot][dz] = jnp.dot(ka, qa_ref[slot], preferred_element_type=F32)

        def live_keys(c):
            return (c + 1) * LANES if diagonal else QB

        for slot in range(len(heads)):
            for c in range(n_chunks):
                cols = slice(c * LANES, (c + 1) * LANES)
                parts = []
                n_full = c * LANES if diagonal else QB
                if n_full:
                    parts.append(s_refs[slot][dz, :n_full, cols].max(axis=0, keepdims=True))
                if diagonal:
                    tile = slice(c * LANES, (c + 1) * LANES)
                    key_id = lax.broadcasted_iota(jnp.int32, (LANES, LANES), 0)
                    qry_id = lax.broadcasted_iota(jnp.int32, (LANES, LANES), 1)
                    x = jnp.where(key_id <= qry_id, s_refs[slot][dz, tile, cols], NEG_INF)
                    s_refs[slot][dz, tile, cols] = x
                    parts.append(x.max(axis=0, keepdims=True))
                blk_max = functools.reduce(jnp.maximum, parts)
                if diagonal:
                    m_ref[slot, :, cols] = blk_max
                else:
                    m_old = m_ref[slot, :, cols]
                    m_new = jnp.maximum(m_old, blk_max)
                    m_ref[slot, :, cols] = m_new
                    alpha_ref[slot, :, cols] = jnp.exp2(m_old - m_new)
        chunks_per_tile = MXU_TILE // LANES
        ones = jnp.ones((ONES_ROWS, QB), BF16)
        for slot, (pp, hh) in enumerate(heads):
            vt = vt_ref[pp, blk, hh * HEAD_DIM:(hh + 1) * HEAD_DIM, :]
            vt = jnp.concatenate([vt, ones], axis=0)
            for nt in range(QB // MXU_TILE):
                tile_cols = slice(nt * MXU_TILE, (nt + 1) * MXU_TILE)
                n_key_tiles = (nt + 1) if diagonal else QB // MXU_TILE
                pv = None
                for kt in range(n_key_tiles):
                    columns = []
                    for c in range(nt * chunks_per_tile, (nt + 1) * chunks_per_tile):
                        cols = slice(c * LANES, (c + 1) * LANES)
                        m_new = m_ref[slot, :, cols]
                        pieces = []
                        for r in range(kt * MXU_TILE, (kt + 1) * MXU_TILE, KSTRIP):
                            if r < live_keys(c):
                                pr = jnp.exp2(s_refs[slot][dz, r:r + KSTRIP, cols] - m_new)
                                pieces.append(pr.astype(BF16))
                            else:
                                pieces.append(jnp.zeros((KSTRIP, LANES), BF16))
                        columns.append(jnp.concatenate(pieces, axis=0))
                    p_tile = jnp.concatenate(columns, axis=1)
                    part = jnp.dot(vt[:, kt * MXU_TILE:(kt + 1) * MXU_TILE], p_tile,
                                   preferred_element_type=F32)
                    pv = part if pv is None else pv + part
                if diagonal:
                    acc_ref[slot, :, tile_cols] = pv
                else:
                    acc_ref[slot, :, tile_cols] = (
                        alpha_ref[slot, :, tile_cols] * acc_ref[slot, :, tile_cols] + pv)

    def body(j, carry):
        key_block(j, False)
        return carry

    key_block(i, True)
    lax.fori_loop(0, i, body, 0)
    for pp in range(FOX_PAIRS):
        o_t = jnp.concatenate(
            [acc_ref[2 * pp + hh, :HEAD_DIM, :] / acc_ref[2 * pp + hh, HEAD_DIM:HEAD_DIM + 1, :]
             for hh in range(2)], axis=0)
        o_ref[pp] = o_t.T.astype(BF16)


def _fox(q, k, vt, q_ext, k_ext):
    batch, _, seq, _ = k.shape
    nq = seq // QB
    n_heads = 2 * FOX_PAIRS
    assert vt.shape == (batch, N_PAIRS, nq, PAIR_W, QB)
    return pl.pallas_call(
        _fox_kernel,
        out_shape=jax.ShapeDtypeStruct(k.shape, BF16),
        grid=(batch, N_PAIRS // FOX_PAIRS, nq),
        in_specs=[
            pl.BlockSpec((None, FOX_PAIRS, PAIR_W, QB), lambda b, p, i: (b, p, 0, i)),
            _resident((None, FOX_PAIRS, seq, PAIR_W), lambda b, p, i: (b, p, 0, 0)),
            _resident((None, FOX_PAIRS, nq, PAIR_W, QB), lambda b, p, i: (b, p, 0, 0, 0)),
            pl.BlockSpec((None, EXT_W, QB), lambda b, p, i: (b, 0, i)),
            _resident((None, seq, EXT_W), lambda b, p, i: (b, 0, 0)),
            pl.BlockSpec(memory_space=pltpu.SMEM),
        ],
        out_specs=pl.BlockSpec((None, FOX_PAIRS, QB, PAIR_W), lambda b, p, i: (b, p, i, 0)),
        scratch_shapes=(
            [pltpu.VMEM((n_heads, PAIR_W + EXT_W, QB), BF16)]
            + [pltpu.VMEM((1, QB, QB), F32)] * n_heads
            + [pltpu.VMEM((n_heads, 1, QB), F32),
               pltpu.VMEM((n_heads, 1, QB), F32),
               pltpu.VMEM((n_heads, HEAD_DIM + ONES_ROWS, QB), F32)]
        ),
        compiler_params=_params("parallel", "parallel", "arbitrary"),
        name="fox",
    )(q, k, vt, q_ext, k_ext, jnp.zeros((1,), jnp.int32))


def kernel(x, ffn_norm, ffn_w_gate, ffn_w_up, ffn_w_down, mix_norm, a_w_qkv, a_w_o, a_rel_bias,
           kv_norm, b_w_kvf, b_f_bias, b_w_q, b_w_o, final_norm):
    batch, seq, _ = x.shape
    depth = ffn_norm.shape[0]
    n_a = a_w_qkv.shape[0]
    n_b = b_w_q.shape[0]
    assert seq % TM == 0 and seq % QA == 0 and seq % QB == 0 and n_a + n_b == depth

    wg = ffn_w_gate.astype(BF16)
    wu = ffn_w_up.astype(BF16)
    wd = ffn_w_down.astype(BF16)
    gf = final_norm.reshape(1, D_MODEL)

    def half_ffn(h, layer, pos, **fused):
        return _ffn(h, ffn_norm[layer, pos].reshape(1, D_MODEL), wg, wu, wd, gf, layer, pos,
                    batch, seq, **fused)

    h = x.reshape(batch * seq, D_MODEL)
    for layer in range(n_a):
        g_mix = mix_norm[layer].reshape(1, D_MODEL)
        h, qt, k, vt = half_ffn(
            h, layer, 0, proj=(g_mix, a_w_qkv[layer].astype(BF16), (True, False, True)))
        o = _attn_a(qt, k, vt, _attn_a_bias_table(a_rel_bias[layer]))
        (h,) = half_ffn(h, layer, 1, mixer=(o, a_w_o[layer].astype(BF16)))

    w_kvf = jnp.pad(b_w_kvf, ((0, 0), (0, EXT_W - N_HEADS))).astype(BF16)
    bf = jnp.pad(b_f_bias, (0, EXT_W - N_HEADS)).reshape(1, EXT_W).astype(F32)
    k_sh, vt_sh, q_ext_t, k_ext = _kvf(h, kv_norm.reshape(1, D_MODEL), w_kvf, bf, batch, seq)

    for lb in range(n_b):
        layer = n_a + lb
        g_mix = mix_norm[layer].reshape(1, D_MODEL)
        h, qt = half_ffn(h, layer, 0, proj=(g_mix, b_w_q[lb].astype(BF16), (True,)))
        o = _fox(qt, k_sh, vt_sh, q_ext_t, k_ext)
        (h,) = half_ffn(h, layer, 1, mixer=(o, b_w_o[lb].astype(BF16)),
                        final=(layer == depth - 1))

    return h.reshape(batch, seq, D_MODEL)
```

```python
import functools

import jax
import jax.numpy as jnp
from jax import lax
from jax.experimental import pallas as pl
from jax.experimental.pallas import tpu as pltpu

D_MODEL = 1024
N_HEADS = 16
HEAD_DIM = 64
N_PAIRS = N_HEADS // 2
PAIR_W = 2 * HEAD_DIM
D_FF = 2816
CHUNK = 64
LEFT_CHUNKS = 8
REL_CLIP = 256
EPS = 1e-6
NEG_INF = -1e30
ATTN_SCALE = HEAD_DIM ** -0.5
LOG2E = 1.4426950408889634
Q_SCALE = ATTN_SCALE * LOG2E
FFN_RES_WEIGHT = 0.5

TM = 512
FF_CHUNK = 256
QA = 256
A_KBLOCKS = LEFT_CHUNKS * CHUNK // QA + 1
A_PAIRS = N_PAIRS
QB = 512
FOX_PAIRS = N_PAIRS
EXT_W = 128
EXT_PER_HEAD = 6
LANES = 128
MXU_TILE = 256
KSTRIP = 128
ONES_ROWS = 16
VMEM_LIMIT = 56 * 1024 * 1024

F32 = jnp.float32
BF16 = jnp.bfloat16


def _rms_norm(x, g):
    y = x * lax.rsqrt(jnp.mean(x * x, axis=-1, keepdims=True) + EPS)
    return y * g


def _resident(block_shape, index_map):
    return pl.BlockSpec(block_shape, index_map, pipeline_mode=pl.Buffered(1))


def _params(*semantics):
    return pltpu.CompilerParams(dimension_semantics=semantics, vmem_limit_bytes=VMEM_LIMIT)


def _ffn_kernel(*refs, has_mixer, transposed, final, n_cast):
    n_proj = len(transposed)
    x_ref, g_ref, wg_ref, wu_ref, wd_ref, gf_ref = refs[:6]
    at = 6
    if has_mixer:
        o_ref, wo_ref = refs[at:at + 2]
        at += 2
    if n_proj:
        gm_ref, wp_ref = refs[at:at + 2]
        at += 2
    cast_in = refs[at:at + n_cast]
    at += n_cast
    out_ref = refs[at]
    proj_refs = refs[at + 1:at + 1 + n_proj]
    cast_out = refs[at + 1 + n_proj:at + 1 + n_proj + n_cast]
    a_ref = refs[-1]

    for src, dst in zip(cast_in, cast_out):
        dst[...] = src[...].astype(BF16)

    for sub in range(a_ref.shape[0]):
        rows = slice(sub * TM, (sub + 1) * TM)
        x = x_ref[rows, :]
        if has_mixer:
            o = jnp.concatenate([o_ref[p, rows, :] for p in range(N_PAIRS)], axis=1)
            x = x + jnp.dot(o, wo_ref[...], preferred_element_type=F32)
        hn = _rms_norm(x, g_ref[...]).astype(BF16)
        for c in range(D_FF // FF_CHUNK):
            sl = slice(c * FF_CHUNK, (c + 1) * FF_CHUNK)
            gate = jnp.dot(hn, wg_ref[:, sl], preferred_element_type=F32)
            up = jnp.dot(hn, wu_ref[:, sl], preferred_element_type=F32)
            a_ref[sub, :, sl] = (gate * jax.nn.sigmoid(gate) * up).astype(BF16)
        y = jnp.dot(a_ref[sub], wd_ref[...], preferred_element_type=F32)
        out = x + FFN_RES_WEIGHT * y
        if final:
            out = _rms_norm(out, gf_ref[...])
        out_ref[rows, :] = out
        if n_proj:
            hp = _rms_norm(out, gm_ref[...]).astype(BF16)
            for n, p_ref in enumerate(proj_refs):
                y = jnp.dot(hp, wp_ref[:, n * D_MODEL:(n + 1) * D_MODEL],
                            preferred_element_type=F32)
                if n == 0:
                    y = y * Q_SCALE
                for p in range(N_PAIRS):
                    y_pair = y[:, p * PAIR_W:(p + 1) * PAIR_W]
                    if transposed[n]:
                        p_ref[p, :, rows] = y_pair.T.astype(BF16)
                    else:
                        p_ref[p, rows, :] = y_pair.astype(BF16)


def _ffn_sub_tiles(n_side):
    def estimate(n_sub):
        rows = n_sub * TM
        stream = 2 * 2 * rows * D_MODEL * 4
        stream += 2 * n_side * rows * D_MODEL * 2
        resident = (3 * D_FF + n_side * D_MODEL) * D_MODEL * 2
        scratch = rows * D_FF * 2
        live = 4 * TM * D_MODEL * 4
        return stream + resident + scratch + live

    return 2 if estimate(2) <= VMEM_LIMIT else 1


def _ffn(h, g, weights, gf, batch, seq, mixer=None, proj=None, final=False, cast_next=None):
    n_side = (0 if proj is None else len(proj[2])) + (mixer is not None)
    n_sub = _ffn_sub_tiles(n_side)
    tm = n_sub * TM
    ns = seq // tm

    def rows(b, i):
        return (b * ns + i, 0)

    def fixed(b, i):
        return (0, 0)

    pair_spec = pl.BlockSpec((None, N_PAIRS, tm, PAIR_W), lambda b, i: (b, 0, i, 0))
    inputs = [h, g, *weights, gf]
    in_specs = [
        pl.BlockSpec((tm, D_MODEL), rows),
        _resident((1, D_MODEL), fixed),
        _resident((D_MODEL, D_FF), fixed),
        _resident((D_MODEL, D_FF), fixed),
        _resident((D_FF, D_MODEL), fixed),
        _resident((1, D_MODEL), fixed),
    ]
    if mixer is not None:
        inputs += list(mixer)
        in_specs += [pair_spec, _resident((D_MODEL, D_MODEL), fixed)]
    out_shape = [jax.ShapeDtypeStruct(h.shape, F32)]
    out_specs = [pl.BlockSpec((tm, D_MODEL), rows)]
    transposed = ()
    if proj is not None:
        g_mix, w_proj, transposed = proj
        assert w_proj.shape[1] == len(transposed) * D_MODEL
        inputs += [g_mix, w_proj]
        in_specs += [_resident((1, D_MODEL), fixed), _resident(w_proj.shape, fixed)]
        for flag in transposed:
            if flag:
                out_shape.append(jax.ShapeDtypeStruct((batch, N_PAIRS, PAIR_W, seq), BF16))
                out_specs.append(
                    pl.BlockSpec((None, N_PAIRS, PAIR_W, tm), lambda b, i: (b, 0, 0, i)))
            else:
                out_shape.append(jax.ShapeDtypeStruct((batch, N_PAIRS, seq, PAIR_W), BF16))
                out_specs.append(pair_spec)
    n_cast = 0
    if cast_next is not None:
        stacks, layer, pos = cast_next
        n_cast = len(stacks)
        cast_rows = D_MODEL // (batch * ns)
        assert cast_rows * batch * ns == D_MODEL and cast_rows % 16 == 0
        inputs += list(stacks)
        in_specs += [pl.BlockSpec((None, None, cast_rows, D_FF),
                                  lambda b, i: (layer, pos, b * ns + i, 0))] * n_cast
        out_shape += [jax.ShapeDtypeStruct((D_MODEL, D_FF), BF16)] * n_cast
        out_specs += [pl.BlockSpec((cast_rows, D_FF), rows)] * n_cast
    return pl.pallas_call(
        functools.partial(_ffn_kernel, has_mixer=mixer is not None, transposed=tuple(transposed),
                          final=final, n_cast=n_cast),
        out_shape=out_shape,
        grid=(batch, ns),
        in_specs=in_specs,
        out_specs=out_specs,
        scratch_shapes=[pltpu.VMEM((n_sub, TM, D_FF), BF16)],
        compiler_params=_params("parallel", "parallel"),
        name="ffn",
    )(*inputs)


def _keep_head_rows(x_t, head_in_pair):
    row = lax.broadcasted_iota(jnp.int32, x_t.shape, 0)
    own = (row >= HEAD_DIM * head_in_pair) & (row < HEAD_DIM * (head_in_pair + 1))
    return jnp.where(own, x_t, jnp.zeros_like(x_t))


def _attn_a_kernel(q_ref, *refs):
    k_refs = refs[:A_KBLOCKS]
    vt_refs = refs[A_KBLOCKS:2 * A_KBLOCKS]
    bias_ref, zero_ref, o_ref = refs[2 * A_KBLOCKS:2 * A_KBLOCKS + 3]
    s_refs = refs[2 * A_KBLOCKS + 3:-1]
    m_ref = refs[-1]
    dz = zero_ref[0]
    i = pl.program_id(2)
    pens = [jnp.where(i - (A_KBLOCKS - 1) + j >= 0, 0.0, NEG_INF).astype(F32)
            for j in range(A_KBLOCKS)]
    heads = [(pp, hh) for pp in range(A_PAIRS) for hh in range(2)]
    chunks = [slice(c * LANES, (c + 1) * LANES) for c in range(QA // LANES)]
    for (pp, hh), s_ref in zip(heads, s_refs):
        qm_t = _keep_head_rows(q_ref[pp], hh)
        for j in range(A_KBLOCKS):
            s_ref[j + dz] = jnp.dot(k_refs[j][pp], qm_t, preferred_element_type=F32)
    for (pp, hh), s_ref in zip(heads, s_refs):
        for cols in chunks:
            blk_max = []
            for j in range(A_KBLOCKS):
                x = s_ref[j + dz, :, cols] + bias_ref[2 * pp + hh, j, :, cols]
                s_ref[j + dz, :, cols] = x
                blk_max.append(x.max(axis=0, keepdims=True) + pens[j])
            m_ref[pp, hh, :, cols] = functools.reduce(jnp.maximum, blk_max)
    ones = jnp.ones((ONES_ROWS, QA), BF16)
    for pp in range(A_PAIRS):
        outs = []
        for hh in range(2):
            s_ref = s_refs[2 * pp + hh]
            pv = None
            for j in range(A_KBLOCKS):
                columns = []
                for cols in chunks:
                    shift = m_ref[pp, hh, :, cols] - pens[j]
                    pieces = [jnp.exp2(s_ref[j + dz, r:r + KSTRIP, cols] - shift).astype(BF16)
                              for r in range(0, QA, KSTRIP)]
                    columns.append(jnp.concatenate(pieces, axis=0))
                p_tile = jnp.concatenate(columns, axis=1)
                vt = vt_refs[j][pp, hh * HEAD_DIM:(hh + 1) * HEAD_DIM, :]
                part = jnp.dot(jnp.concatenate([vt, ones], axis=0), p_tile,
                               preferred_element_type=F32)
                pv = part if pv is None else pv + part
            outs.append(pv[:HEAD_DIM] / pv[HEAD_DIM:HEAD_DIM + 1])
        o_ref[pp] = jnp.concatenate(outs, axis=0).T.astype(BF16)


def _attn_a(q, k, vt, bias_t):
    batch, _, seq, _ = k.shape
    nq = seq // QA

    def k_spec(j):
        back = A_KBLOCKS - 1 - j
        return pl.BlockSpec((None, A_PAIRS, QA, PAIR_W),
                            lambda b, p, i: (b, p, jnp.maximum(i - back, 0), 0))

    def vt_spec(j):
        back = A_KBLOCKS - 1 - j
        return pl.BlockSpec((None, A_PAIRS, PAIR_W, QA),
                            lambda b, p, i: (b, p, 0, jnp.maximum(i - back, 0)))

    return pl.pallas_call(
        _attn_a_kernel,
        out_shape=jax.ShapeDtypeStruct(k.shape, BF16),
        grid=(batch, N_PAIRS // A_PAIRS, nq),
        in_specs=(
            [pl.BlockSpec((None, A_PAIRS, PAIR_W, QA), lambda b, p, i: (b, p, 0, i))]
            + [k_spec(j) for j in range(A_KBLOCKS)]
            + [vt_spec(j) for j in range(A_KBLOCKS)]
            + [_resident((N_HEADS, A_KBLOCKS, QA, QA), lambda b, p, i: (0, 0, 0, 0))]
            + [pl.BlockSpec(memory_space=pltpu.SMEM)]
        ),
        out_specs=pl.BlockSpec((None, A_PAIRS, QA, PAIR_W), lambda b, p, i: (b, p, i, 0)),
        scratch_shapes=(
            [pltpu.VMEM((A_KBLOCKS, QA, QA), F32)] * (2 * A_PAIRS)
            + [pltpu.VMEM((A_PAIRS, 2, 1, QA), F32)]
        ),
        compiler_params=_params("parallel", "parallel", "arbitrary"),
        name="attn_a",
    )(q, *([k] * A_KBLOCKS), *([vt] * A_KBLOCKS), bias_t, jnp.zeros((1,), jnp.int32))


def _bias_table_kernel(vec_ref, out_ref):
    back = (A_KBLOCKS - 1) * QA
    key = lax.broadcasted_iota(jnp.int32, (QA, QA), 0)
    qry = lax.broadcasted_iota(jnp.int32, (QA, QA), 1)
    chunk_shift = CHUNK.bit_length() - 1
    for j in range(A_KBLOCKS):
        rolled = pltpu.roll(jnp.broadcast_to(vec_ref[j], (QA, 2 * QA)), 0, 1,
                            stride=1, stride_axis=0)
        dchunk = (lax.shift_right_logical(qry + back, chunk_shift)
                  - lax.shift_right_logical(key + j * QA, chunk_shift))
        valid = (dchunk >= 0) & (dchunk <= LEFT_CHUNKS)
        out_ref[j] = jnp.where(valid, rolled[:, QA:], NEG_INF)


def _attn_a_bias_table(rel_bias):
    n_heads = rel_bias.shape[0]
    back = (A_KBLOCKS - 1) * QA
    d = (back - QA * jnp.arange(A_KBLOCKS))[:, None] + jnp.arange(2 * QA)[None, :] - QA
    vec = jnp.take(rel_bias.astype(F32) * LOG2E, jnp.clip(d, -REL_CLIP, REL_CLIP) + REL_CLIP,
                   axis=1)
    vec = vec.reshape(n_heads, A_KBLOCKS, 1, 2 * QA)
    return pl.pallas_call(
        _bias_table_kernel,
        out_shape=jax.ShapeDtypeStruct((n_heads, A_KBLOCKS, QA, QA), F32),
        grid=(n_heads,),
        in_specs=[pl.BlockSpec((None, A_KBLOCKS, 1, 2 * QA), lambda h: (h, 0, 0, 0))],
        out_specs=pl.BlockSpec((None, A_KBLOCKS, QA, QA), lambda h: (h, 0, 0, 0)),
        compiler_params=_params("parallel"),
        name="bias_table",
    )(vec)


def _split3(x):
    hi = x.astype(BF16)
    r1 = x - hi.astype(F32)
    mid = r1.astype(BF16)
    lo = (r1 - mid.astype(F32)).astype(BF16)
    return hi, mid, lo


def _kvf_kernel(x_ref, g_ref, w_ref, bf_ref, tri_ref, sel_ref, one_ref,
                k_ref, vt_ref, qe_ref, ke_ref, carry_ref):
    @pl.when(pl.program_id(1) == 0)
    def _():
        carry_ref[...] = jnp.zeros_like(carry_ref)

    hn = _rms_norm(x_ref[...], g_ref[...]).astype(BF16)
    y = jnp.dot(hn, w_ref[...], preferred_element_type=F32)
    for p in range(N_PAIRS):
        k_cols = slice(p * PAIR_W, (p + 1) * PAIR_W)
        v_cols = slice(D_MODEL + p * PAIR_W, D_MODEL + (p + 1) * PAIR_W)
        k_ref[p] = y[:, k_cols].astype(BF16)
        vt_ref[p] = y[:, v_cols].T.astype(BF16)
    z = y[:, 2 * D_MODEL:] + bf_ref[...]
    log_f = jnp.minimum(z, 0.0) - jnp.log1p(jnp.exp(-jnp.abs(z)))
    cum3 = jnp.dot(tri_ref[...], jnp.concatenate(_split3(log_f), axis=1),
                   preferred_element_type=F32)
    cum = (cum3[:, :EXT_W] + cum3[:, EXT_W:2 * EXT_W] + cum3[:, 2 * EXT_W:]) + carry_ref[...]
    carry_ref[...] = cum[TM - 1:TM, :]
    pieces = jnp.concatenate(_split3(cum * LOG2E), axis=1)
    ext = jnp.dot(pieces, sel_ref[...], preferred_element_type=F32) + one_ref[...]
    qe_ref[...] = ext[:, :EXT_W].T.astype(BF16)
    ke_ref[...] = ext[:, EXT_W:].astype(BF16)


def _kvf(h, g, w, bf, batch, seq):
    ns = seq // TM
    tri = (jnp.arange(TM)[:, None] >= jnp.arange(TM)[None, :]).astype(BF16)
    half = EXT_PER_HEAD // 2
    src = jnp.arange(3 * EXT_W)[:, None]
    dst = jnp.arange(EXT_W)[None, :]
    src_head, src_piece = src % EXT_W, src // EXT_W
    dst_head, dst_slot = dst // EXT_PER_HEAD, dst % EXT_PER_HEAD
    same_head = (src_head == dst_head) & (src_head < N_HEADS)
    sel_q = (same_head & (dst_slot == half + src_piece)).astype(BF16)
    sel_k = -(same_head & (dst_slot == src_piece)).astype(BF16)
    live = jnp.arange(EXT_W) < N_HEADS * EXT_PER_HEAD
    slot = jnp.arange(EXT_W) % EXT_PER_HEAD
    one_q = (live & (slot < half)).astype(F32).reshape(1, EXT_W)
    one_k = (live & (slot >= half)).astype(F32).reshape(1, EXT_W)
    sel = jnp.concatenate([sel_q, sel_k], axis=1)
    one = jnp.concatenate([one_q, one_k], axis=1)

    k_sds = jax.ShapeDtypeStruct((batch, N_PAIRS, seq, PAIR_W), BF16)
    k_spec = pl.BlockSpec((None, N_PAIRS, TM, PAIR_W), lambda b, i: (b, 0, i, 0))
    vt_sds = jax.ShapeDtypeStruct((batch, N_PAIRS, ns, PAIR_W, TM), BF16)
    vt_spec = pl.BlockSpec((None, N_PAIRS, None, PAIR_W, TM), lambda b, i: (b, 0, i, 0, 0))
    ext_sds = jax.ShapeDtypeStruct((batch, seq, EXT_W), BF16)
    ext_spec = pl.BlockSpec((None, TM, EXT_W), lambda b, i: (b, i, 0))
    ext_t_sds = jax.ShapeDtypeStruct((batch, EXT_W, seq), BF16)
    ext_t_spec = pl.BlockSpec((None, EXT_W, TM), lambda b, i: (b, 0, i))
    return pl.pallas_call(
        _kvf_kernel,
        out_shape=[k_sds, vt_sds, ext_t_sds, ext_sds],
        grid=(batch, ns),
        in_specs=[
            pl.BlockSpec((TM, D_MODEL), lambda b, i: (b * ns + i, 0)),
            _resident((1, D_MODEL), lambda b, i: (0, 0)),
            _resident((D_MODEL, 2 * D_MODEL + EXT_W), lambda b, i: (0, 0)),
            _resident((1, EXT_W), lambda b, i: (0, 0)),
            _resident((TM, TM), lambda b, i: (0, 0)),
            _resident((3 * EXT_W, 2 * EXT_W), lambda b, i: (0, 0)),
            _resident((1, 2 * EXT_W), lambda b, i: (0, 0)),
        ],
        out_specs=[k_spec, vt_spec, ext_t_spec, ext_spec],
        scratch_shapes=[pltpu.VMEM((1, EXT_W), F32)],
        compiler_params=_params("parallel", "arbitrary"),
        name="kvf",
    )(h, g, w, bf, tri, sel, one)


def _fox_kernel(q_ref, k_ref, vt_ref, qe_ref, ke_ref, zero_ref, o_ref, qa_ref, *scratch):
    s_refs = scratch[:-3]
    m_ref, alpha_ref, acc_ref = scratch[-3:]
    dz = zero_ref[0]
    group = pl.program_id(1)
    i = pl.program_id(2)
    heads = [(pp, hh) for pp in range(FOX_PAIRS) for hh in range(2)]
    qe_t = qe_ref[...]
    ext_row = lax.broadcasted_iota(jnp.int32, qe_t.shape, 0)
    for slot, (pp, hh) in enumerate(heads):
        head = 2 * (group * FOX_PAIRS + pp) + hh
        own = (ext_row >= EXT_PER_HEAD * head) & (ext_row < EXT_PER_HEAD * (head + 1))
        qa_ref[slot] = jnp.concatenate(
            [_keep_head_rows(q_ref[pp], hh), jnp.where(own, qe_t, jnp.zeros_like(qe_t))], axis=0)

    half = QB // 2

    n_chunks = QB // LANES

    def key_block(blk, diagonal):
        off = pl.multiple_of(blk * QB, QB)
        k_ext = ke_ref[pl.ds(off, QB), :]
        for slot, (pp, hh) in enumerate(heads):
            ka = jnp.concatenate([k_ref[pp, pl.ds(off, QB), :], k_ext], axis=1)
            if diagonal:
                s_refs[slot][dz, :half, :half] = jnp.dot(
                    ka[:half], qa_ref[slot, :, :half], preferred_element_type=F32)
                s_refs[slot][dz, :, half:] = jnp.dot(
                    ka, qa_ref[slot, :, half:], preferred_element_type=F32)
            else:
                s_refs[slot][dz] = jnp.dot(ka, qa_ref[slot], preferred_element_type=F32)

        def live_keys(c):
            return (c + 1) * LANES if diagonal else QB

        for slot in range(len(heads)):
            for c in range(n_chunks):
                cols = slice(c * LANES, (c + 1) * LANES)
                parts = []
                n_full = c * LANES if diagonal else QB
                if n_full:
                    parts.append(s_refs[slot][dz, :n_full, cols].max(axis=0, keepdims=True))
                if diagonal:
                    tile = slice(c * LANES, (c + 1) * LANES)
                    key_id = lax.broadcasted_iota(jnp.int32, (LANES, LANES), 0)
                    qry_id = lax.broadcasted_iota(jnp.int32, (LANES, LANES), 1)
                    x = jnp.where(key_id <= qry_id, s_refs[slot][dz, tile, cols], NEG_INF)
                    s_refs[slot][dz, tile, cols] = x
                    parts.append(x.max(axis=0, keepdims=True))
                blk_max = functools.reduce(jnp.maximum, parts)
                if diagonal:
                    m_ref[slot, :, cols] = blk_max
                else:
                    m_old = m_ref[slot, :, cols]
                    m_new = jnp.maximum(m_old, blk_max)
                    m_ref[slot, :, cols] = m_new
                    alpha_ref[slot, :, cols] = jnp.exp2(m_old - m_new)
        chunks_per_tile = MXU_TILE // LANES
        ones = jnp.ones((ONES_ROWS, QB), BF16)
        for slot, (pp, hh) in enumerate(heads):
            vt = vt_ref[pp, blk, hh * HEAD_DIM:(hh + 1) * HEAD_DIM, :]
            vt = jnp.concatenate([vt, ones], axis=0)
            for nt in range(QB // MXU_TILE):
                tile_cols = slice(nt * MXU_TILE, (nt + 1) * MXU_TILE)
                n_key_tiles = (nt + 1) if diagonal else QB // MXU_TILE
                pv = None
                for kt in range(n_key_tiles):
                    columns = []
                    for c in range(nt * chunks_per_tile, (nt + 1) * chunks_per_tile):
                        cols = slice(c * LANES, (c + 1) * LANES)
                        m_new = m_ref[slot, :, cols]
                        pieces = []
                        for r in range(kt * MXU_TILE, (kt + 1) * MXU_TILE, KSTRIP):
                            if r < live_keys(c):
                                pr = jnp.exp2(s_refs[slot][dz, r:r + KSTRIP, cols] - m_new)
                                pieces.append(pr.astype(BF16))
                            else:
                                pieces.append(jnp.zeros((KSTRIP, LANES), BF16))
                        columns.append(jnp.concatenate(pieces, axis=0))
                    p_tile = jnp.concatenate(columns, axis=1)
                    part = jnp.dot(vt[:, kt * MXU_TILE:(kt + 1) * MXU_TILE], p_tile,
                                   preferred_element_type=F32)
                    pv = part if pv is None else pv + part
                if diagonal:
                    acc_ref[slot, :, tile_cols] = pv
                else:
                    acc_ref[slot, :, tile_cols] = (
                        alpha_ref[slot, :, tile_cols] * acc_ref[slot, :, tile_cols] + pv)

    def body(j, carry):
        key_block(j, False)
        return carry

    key_block(i, True)
    lax.fori_loop(0, i, body, 0)
    for pp in range(FOX_PAIRS):
        o_t = jnp.concatenate(
            [acc_ref[2 * pp + hh, :HEAD_DIM, :] / acc_ref[2 * pp + hh, HEAD_DIM:HEAD_DIM + 1, :]
             for hh in range(2)], axis=0)
        o_ref[pp] = o_t.T.astype(BF16)


def _fox(q, k, vt, q_ext, k_ext):
    batch, _, seq, _ = k.shape
    nq = seq // QB
    n_heads = 2 * FOX_PAIRS
    assert vt.shape == (batch, N_PAIRS, nq, PAIR_W, QB)
    return pl.pallas_call(
        _fox_kernel,
        out_shape=jax.ShapeDtypeStruct(k.shape, BF16),
        grid=(batch, N_PAIRS // FOX_PAIRS, nq),
        in_specs=[
            pl.BlockSpec((None, FOX_PAIRS, PAIR_W, QB), lambda b, p, i: (b, p, 0, i)),
            _resident((None, FOX_PAIRS, seq, PAIR_W), lambda b, p, i: (b, p, 0, 0)),
            _resident((None, FOX_PAIRS, nq, PAIR_W, QB), lambda b, p, i: (b, p, 0, 0, 0)),
            pl.BlockSpec((None, EXT_W, QB), lambda b, p, i: (b, 0, i)),
            _resident((None, seq, EXT_W), lambda b, p, i: (b, 0, 0)),
            pl.BlockSpec(memory_space=pltpu.SMEM),
        ],
        out_specs=pl.BlockSpec((None, FOX_PAIRS, QB, PAIR_W), lambda b, p, i: (b, p, i, 0)),
        scratch_shapes=(
            [pltpu.VMEM((n_heads, PAIR_W + EXT_W, QB), BF16)]
            + [pltpu.VMEM((1, QB, QB), F32)] * n_heads
            + [pltpu.VMEM((n_heads, 1, QB), F32),
               pltpu.VMEM((n_heads, 1, QB), F32),
               pltpu.VMEM((n_heads, HEAD_DIM + ONES_ROWS, QB), F32)]
        ),
        compiler_params=_params("parallel", "parallel", "arbitrary"),
        name="fox",
    )(q, k, vt, q_ext, k_ext, jnp.zeros((1,), jnp.int32))


def kernel(x, ffn_norm, ffn_w_gate, ffn_w_up, ffn_w_down, mix_norm, a_w_qkv, a_w_o, a_rel_bias,
           kv_norm, b_w_kvf, b_f_bias, b_w_q, b_w_o, final_norm):
    batch, seq, _ = x.shape
    depth = ffn_norm.shape[0]
    n_a = a_w_qkv.shape[0]
    n_b = b_w_q.shape[0]
    assert seq % TM == 0 and seq % QA == 0 and seq % QB == 0 and n_a + n_b == depth

    gf = final_norm.reshape(1, D_MODEL)
    stacks = (ffn_w_gate, ffn_w_up, ffn_w_down.reshape(depth, 2, D_MODEL, D_FF))
    ffn_weights = [ffn_w_gate[0, 0].astype(BF16), ffn_w_up[0, 0].astype(BF16),
                   ffn_w_down[0, 0].astype(BF16)]

    def half_ffn(h, layer, pos, **fused):
        nxt = 2 * layer + pos + 1
        cast_next = (stacks, nxt // 2, nxt % 2) if nxt < 2 * depth else None
        outs = _ffn(h, ffn_norm[layer, pos].reshape(1, D_MODEL), tuple(ffn_weights), gf,
                    batch, seq, cast_next=cast_next, **fused)
        if cast_next is not None:
            ffn_weights[:] = [outs[-3], outs[-2], outs[-1].reshape(D_FF, D_MODEL)]
            outs = outs[:-3]
        return outs

    h = x.reshape(batch * seq, D_MODEL)
    for layer in range(n_a):
        g_mix = mix_norm[layer].reshape(1, D_MODEL)
        h, qt, k, vt = half_ffn(
            h, layer, 0, proj=(g_mix, a_w_qkv[layer].astype(BF16), (True, False, True)))
        o = _attn_a(qt, k, vt, _attn_a_bias_table(a_rel_bias[layer]))
        (h,) = half_ffn(h, layer, 1, mixer=(o, a_w_o[layer].astype(BF16)))

    w_kvf = jnp.pad(b_w_kvf, ((0, 0), (0, EXT_W - N_HEADS))).astype(BF16)
    bf = jnp.pad(b_f_bias, (0, EXT_W - N_HEADS)).reshape(1, EXT_W).astype(F32)
    k_sh, vt_sh, q_ext_t, k_ext = _kvf(h, kv_norm.reshape(1, D_MODEL), w_kvf, bf, batch, seq)

    for lb in range(n_b):
        layer = n_a + lb
        g_mix = mix_norm[layer].reshape(1, D_MODEL)
        h, qt = half_ffn(h, layer, 0, proj=(g_mix, b_w_q[lb].astype(BF16), (True,)))
        o = _fox(qt, k_sh, vt_sh, q_ext_t, k_ext)
        (h,) = half_ffn(h, layer, 1, mixer=(o, b_w_o[lb].astype(BF16)),
                        final=(layer == depth - 1))

    return h.reshape(batch, seq, D_MODEL)
```

```python
import functools

import jax
import jax.numpy as jnp
from jax import lax
from jax.experimental import pallas as pl
from jax.experimental.pallas import tpu as pltpu

D_MODEL = 1024
N_HEADS = 16
HEAD_DIM = 64
N_PAIRS = N_HEADS // 2
PAIR_W = 2 * HEAD_DIM
D_FF = 2816
CHUNK = 64
LEFT_CHUNKS = 8
REL_CLIP = 256
EPS = 1e-6
NEG_INF = -1e30
ATTN_SCALE = HEAD_DIM ** -0.5
LOG2E = 1.4426950408889634
Q_SCALE = ATTN_SCALE * LOG2E
FFN_RES_WEIGHT = 0.5

TM = 512
FF_CHUNK = 256
CAST_BLOCKS = 16
QA = 256
A_KBLOCKS = LEFT_CHUNKS * CHUNK // QA + 1
A_PAIRS = N_PAIRS
QB = 512
FOX_PAIRS = N_PAIRS
EXT_W = 128
EXT_PER_HEAD = 6
LANES = 128
MXU_TILE = 256
KSTRIP = 128
ONES_ROWS = 16
VMEM_LIMIT = 56 * 1024 * 1024

F32 = jnp.float32
BF16 = jnp.bfloat16


def _rms_norm(x, g):
    y = x * lax.rsqrt(jnp.mean(x * x, axis=-1, keepdims=True) + EPS)
    return y * g


def _resident(block_shape, index_map):
    return pl.BlockSpec(block_shape, index_map, pipeline_mode=pl.Buffered(1))


def _params(*semantics):
    return pltpu.CompilerParams(dimension_semantics=semantics, vmem_limit_bytes=VMEM_LIMIT)


def _ffn_kernel(*refs, has_mixer, transposed, final, n_cast, cast_blocks):
    n_proj = len(transposed)
    x_ref, g_ref, wg_ref, wu_ref, wd_ref, gf_ref = refs[:6]
    at = 6
    if has_mixer:
        o_ref, wo_ref = refs[at:at + 2]
        at += 2
    if n_proj:
        gm_ref, wp_ref = refs[at:at + 2]
        at += 2
    cast_in = refs[at:at + n_cast]
    at += n_cast
    out_ref = refs[at]
    proj_refs = refs[at + 1:at + 1 + n_proj]
    cast_out = refs[at + 1 + n_proj:at + 1 + n_proj + n_cast]
    a_ref = refs[-1]

    if n_cast:
        @pl.when(pl.program_id(0) * pl.num_programs(1) + pl.program_id(1) < cast_blocks)
        def _():
            for src, dst in zip(cast_in, cast_out):
                dst[...] = src[...].astype(BF16)

    for sub in range(a_ref.shape[0]):
        rows = slice(sub * TM, (sub + 1) * TM)
        x = x_ref[rows, :]
        if has_mixer:
            o = jnp.concatenate([o_ref[p, rows, :] for p in range(N_PAIRS)], axis=1)
            x = x + jnp.dot(o, wo_ref[...], preferred_element_type=F32)
        hn = _rms_norm(x, g_ref[...]).astype(BF16)
        for c in range(D_FF // FF_CHUNK):
            sl = slice(c * FF_CHUNK, (c + 1) * FF_CHUNK)
            gate = jnp.dot(hn, wg_ref[:, sl], preferred_element_type=F32)
            up = jnp.dot(hn, wu_ref[:, sl], preferred_element_type=F32)
            a_ref[sub, :, sl] = (gate * jax.nn.sigmoid(gate) * up).astype(BF16)
        y = jnp.dot(a_ref[sub], wd_ref[...], preferred_element_type=F32)
        out = x + FFN_RES_WEIGHT * y
        if final:
            out = _rms_norm(out, gf_ref[...])
        out_ref[rows, :] = out
        if n_proj:
            hp = _rms_norm(out, gm_ref[...]).astype(BF16)
            for n, p_ref in enumerate(proj_refs):
                y = jnp.dot(hp, wp_ref[:, n * D_MODEL:(n + 1) * D_MODEL],
                            preferred_element_type=F32)
                if n == 0:
                    y = y * Q_SCALE
                for p in range(N_PAIRS):
                    y_pair = y[:, p * PAIR_W:(p + 1) * PAIR_W]
                    if transposed[n]:
                        p_ref[p, :, rows] = y_pair.T.astype(BF16)
                    else:
                        p_ref[p, rows, :] = y_pair.astype(BF16)


def _ffn_sub_tiles(n_side):
    def estimate(n_sub):
        rows = n_sub * TM
        stream = 2 * 2 * rows * D_MODEL * 4
        stream += 2 * n_side * rows * D_MODEL * 2
        resident = (3 * D_FF + n_side * D_MODEL) * D_MODEL * 2
        scratch = rows * D_FF * 2
        live = 4 * TM * D_MODEL * 4
        return stream + resident + scratch + live

    return 2 if estimate(2) <= VMEM_LIMIT else 1


def _ffn(h, g, weights, gf, batch, seq, mixer=None, proj=None, final=False, cast_next=None):
    n_side = (0 if proj is None else len(proj[2])) + (mixer is not None)
    n_sub = _ffn_sub_tiles(n_side)
    tm = n_sub * TM
    ns = seq // tm

    def rows(b, i):
        return (b * ns + i, 0)

    def fixed(b, i):
        return (0, 0)

    pair_spec = pl.BlockSpec((None, N_PAIRS, tm, PAIR_W), lambda b, i: (b, 0, i, 0))
    inputs = [h, g, *weights, gf]
    in_specs = [
        pl.BlockSpec((tm, D_MODEL), rows),
        _resident((1, D_MODEL), fixed),
        _resident((D_MODEL, D_FF), fixed),
        _resident((D_MODEL, D_FF), fixed),
        _resident((D_FF, D_MODEL), fixed),
        _resident((1, D_MODEL), fixed),
    ]
    if mixer is not None:
        inputs += list(mixer)
        in_specs += [pair_spec, _resident((D_MODEL, D_MODEL), fixed)]
    out_shape = [jax.ShapeDtypeStruct(h.shape, F32)]
    out_specs = [pl.BlockSpec((tm, D_MODEL), rows)]
    transposed = ()
    if proj is not None:
        g_mix, w_proj, transposed = proj
        assert w_proj.shape[1] == len(transposed) * D_MODEL
        inputs += [g_mix, w_proj]
        in_specs += [_resident((1, D_MODEL), fixed), _resident(w_proj.shape, fixed)]
        for flag in transposed:
            if flag:
                out_shape.append(jax.ShapeDtypeStruct((batch, N_PAIRS, PAIR_W, seq), BF16))
                out_specs.append(
                    pl.BlockSpec((None, N_PAIRS, PAIR_W, tm), lambda b, i: (b, 0, 0, i)))
            else:
                out_shape.append(jax.ShapeDtypeStruct((batch, N_PAIRS, seq, PAIR_W), BF16))
                out_specs.append(pair_spec)
    n_cast = 0
    cast_blocks = min(CAST_BLOCKS, batch * ns)
    if cast_next is not None:
        stacks, layer, pos = cast_next
        n_cast = len(stacks)
        inputs += list(stacks)
        for stack in stacks:
            n_rows, n_cols = stack.shape[2:]
            cast_rows = n_rows // cast_blocks
            assert cast_rows * cast_blocks == n_rows and cast_rows % 16 == 0
            in_specs.append(pl.BlockSpec(
                (None, None, cast_rows, n_cols),
                lambda b, i: (layer, pos, jnp.minimum(b * ns + i, cast_blocks - 1), 0)))
            out_shape.append(jax.ShapeDtypeStruct((n_rows, n_cols), BF16))
            out_specs.append(pl.BlockSpec(
                (cast_rows, n_cols), lambda b, i: (jnp.minimum(b * ns + i, cast_blocks - 1), 0)))
    return pl.pallas_call(
        functools.partial(_ffn_kernel, has_mixer=mixer is not None, transposed=tuple(transposed),
                          final=final, n_cast=n_cast, cast_blocks=cast_blocks),
        out_shape=out_shape,
        grid=(batch, ns),
        in_specs=in_specs,
        out_specs=out_specs,
        scratch_shapes=[pltpu.VMEM((n_sub, TM, D_FF), BF16)],
        compiler_params=_params("arbitrary", "arbitrary"),
        name="ffn",
    )(*inputs)


def _keep_head_rows(x_t, head_in_pair):
    row = lax.broadcasted_iota(jnp.int32, x_t.shape, 0)
    own = (row >= HEAD_DIM * head_in_pair) & (row < HEAD_DIM * (head_in_pair + 1))
    return jnp.where(own, x_t, jnp.zeros_like(x_t))


def _attn_a_kernel(q_ref, *refs):
    k_refs = refs[:A_KBLOCKS]
    vt_refs = refs[A_KBLOCKS:2 * A_KBLOCKS]
    bias_ref, zero_ref, o_ref = refs[2 * A_KBLOCKS:2 * A_KBLOCKS + 3]
    s_refs = refs[2 * A_KBLOCKS + 3:-1]
    m_ref = refs[-1]
    dz = zero_ref[0]
    i = pl.program_id(2)
    pens = [jnp.where(i - (A_KBLOCKS - 1) + j >= 0, 0.0, NEG_INF).astype(F32)
            for j in range(A_KBLOCKS)]
    heads = [(pp, hh) for pp in range(A_PAIRS) for hh in range(2)]
    chunks = [slice(c * LANES, (c + 1) * LANES) for c in range(QA // LANES)]
    for (pp, hh), s_ref in zip(heads, s_refs):
        qm_t = _keep_head_rows(q_ref[pp], hh)
        for j in range(A_KBLOCKS):
            s_ref[j + dz] = jnp.dot(k_refs[j][pp], qm_t, preferred_element_type=F32)
    for (pp, hh), s_ref in zip(heads, s_refs):
        for cols in chunks:
            blk_max = []
            for j in range(A_KBLOCKS):
                x = s_ref[j + dz, :, cols] + bias_ref[2 * pp + hh, j, :, cols]
                s_ref[j + dz, :, cols] = x
                blk_max.append(x.max(axis=0, keepdims=True) + pens[j])
            m_ref[pp, hh, :, cols] = functools.reduce(jnp.maximum, blk_max)
    ones = jnp.ones((ONES_ROWS, QA), BF16)
    for pp in range(A_PAIRS):
        outs = []
        for hh in range(2):
            s_ref = s_refs[2 * pp + hh]
            pv = None
            for j in range(A_KBLOCKS):
                columns = []
                for cols in chunks:
                    shift = m_ref[pp, hh, :, cols] - pens[j]
                    pieces = [jnp.exp2(s_ref[j + dz, r:r + KSTRIP, cols] - shift).astype(BF16)
                              for r in range(0, QA, KSTRIP)]
                    columns.append(jnp.concatenate(pieces, axis=0))
                p_tile = jnp.concatenate(columns, axis=1)
                vt = vt_refs[j][pp, hh * HEAD_DIM:(hh + 1) * HEAD_DIM, :]
                part = jnp.dot(jnp.concatenate([vt, ones], axis=0), p_tile,
                               preferred_element_type=F32)
                pv = part if pv is None else pv + part
            outs.append(pv[:HEAD_DIM] / pv[HEAD_DIM:HEAD_DIM + 1])
        o_ref[pp] = jnp.concatenate(outs, axis=0).T.astype(BF16)


def _attn_a(q, k, vt, bias_t):
    batch, _, seq, _ = k.shape
    nq = seq // QA

    def k_spec(j):
        back = A_KBLOCKS - 1 - j
        return pl.BlockSpec((None, A_PAIRS, QA, PAIR_W),
                            lambda b, p, i: (b, p, jnp.maximum(i - back, 0), 0))

    def vt_spec(j):
        back = A_KBLOCKS - 1 - j
        return pl.BlockSpec((None, A_PAIRS, PAIR_W, QA),
                            lambda b, p, i: (b, p, 0, jnp.maximum(i - back, 0)))

    return pl.pallas_call(
        _attn_a_kernel,
        out_shape=jax.ShapeDtypeStruct(k.shape, BF16),
        grid=(batch, N_PAIRS // A_PAIRS, nq),
        in_specs=(
            [pl.BlockSpec((None, A_PAIRS, PAIR_W, QA), lambda b, p, i: (b, p, 0, i))]
            + [k_spec(j) for j in range(A_KBLOCKS)]
            + [vt_spec(j) for j in range(A_KBLOCKS)]
            + [_resident((N_HEADS, A_KBLOCKS, QA, QA), lambda b, p, i: (0, 0, 0, 0))]
            + [pl.BlockSpec(memory_space=pltpu.SMEM)]
        ),
        out_specs=pl.BlockSpec((None, A_PAIRS, QA, PAIR_W), lambda b, p, i: (b, p, i, 0)),
        scratch_shapes=(
            [pltpu.VMEM((A_KBLOCKS, QA, QA), F32)] * (2 * A_PAIRS)
            + [pltpu.VMEM((A_PAIRS, 2, 1, QA), F32)]
        ),
        compiler_params=_params("parallel", "parallel", "arbitrary"),
        name="attn_a",
    )(q, *([k] * A_KBLOCKS), *([vt] * A_KBLOCKS), bias_t, jnp.zeros((1,), jnp.int32))


def _bias_table_kernel(vec_ref, out_ref):
    back = (A_KBLOCKS - 1) * QA
    key = lax.broadcasted_iota(jnp.int32, (QA, QA), 0)
    qry = lax.broadcasted_iota(jnp.int32, (QA, QA), 1)
    chunk_shift = CHUNK.bit_length() - 1
    for j in range(A_KBLOCKS):
        rolled = pltpu.roll(jnp.broadcast_to(vec_ref[j], (QA, 2 * QA)), 0, 1,
                            stride=1, stride_axis=0)
        dchunk = (lax.shift_right_logical(qry + back, chunk_shift)
                  - lax.shift_right_logical(key + j * QA, chunk_shift))
        valid = (dchunk >= 0) & (dchunk <= LEFT_CHUNKS)
        out_ref[j] = jnp.where(valid, rolled[:, QA:], NEG_INF)


def _attn_a_bias_table(rel_bias):
    n_heads = rel_bias.shape[0]
    back = (A_KBLOCKS - 1) * QA
    d = (back - QA * jnp.arange(A_KBLOCKS))[:, None] + jnp.arange(2 * QA)[None, :] - QA
    vec = jnp.take(rel_bias.astype(F32) * LOG2E, jnp.clip(d, -REL_CLIP, REL_CLIP) + REL_CLIP,
                   axis=1)
    vec = vec.reshape(n_heads, A_KBLOCKS, 1, 2 * QA)
    return pl.pallas_call(
        _bias_table_kernel,
        out_shape=jax.ShapeDtypeStruct((n_heads, A_KBLOCKS, QA, QA), F32),
        grid=(n_heads,),
        in_specs=[pl.BlockSpec((None, A_KBLOCKS, 1, 2 * QA), lambda h: (h, 0, 0, 0))],
        out_specs=pl.BlockSpec((None, A_KBLOCKS, QA, QA), lambda h: (h, 0, 0, 0)),
        compiler_params=_params("parallel"),
        name="bias_table",
    )(vec)


def _split3(x):
    hi = x.astype(BF16)
    r1 = x - hi.astype(F32)
    mid = r1.astype(BF16)
    lo = (r1 - mid.astype(F32)).astype(BF16)
    return hi, mid, lo


def _kvf_kernel(x_ref, g_ref, w_ref, bf_ref, tri_ref, sel_ref, one_ref,
                k_ref, vt_ref, qe_ref, ke_ref, carry_ref):
    @pl.when(pl.program_id(1) == 0)
    def _():
        carry_ref[...] = jnp.zeros_like(carry_ref)

    hn = _rms_norm(x_ref[...], g_ref[...]).astype(BF16)
    y = jnp.dot(hn, w_ref[...], preferred_element_type=F32)
    for p in range(N_PAIRS):
        k_cols = slice(p * PAIR_W, (p + 1) * PAIR_W)
        v_cols = slice(D_MODEL + p * PAIR_W, D_MODEL + (p + 1) * PAIR_W)
        k_ref[p] = y[:, k_cols].astype(BF16)
        vt_ref[p] = y[:, v_cols].T.astype(BF16)
    z = y[:, 2 * D_MODEL:] + bf_ref[...]
    log_f = jnp.minimum(z, 0.0) - jnp.log1p(jnp.exp(-jnp.abs(z)))
    cum3 = jnp.dot(tri_ref[...], jnp.concatenate(_split3(log_f), axis=1),
                   preferred_element_type=F32)
    cum = (cum3[:, :EXT_W] + cum3[:, EXT_W:2 * EXT_W] + cum3[:, 2 * EXT_W:]) + carry_ref[...]
    carry_ref[...] = cum[TM - 1:TM, :]
    pieces = jnp.concatenate(_split3(cum * LOG2E), axis=1)
    ext = jnp.dot(pieces, sel_ref[...], preferred_element_type=F32) + one_ref[...]
    qe_ref[...] = ext[:, :EXT_W].T.astype(BF16)
    ke_ref[...] = ext[:, EXT_W:].astype(BF16)


def _kvf(h, g, w, bf, batch, seq):
    ns = seq // TM
    tri = (jnp.arange(TM)[:, None] >= jnp.arange(TM)[None, :]).astype(BF16)
    half = EXT_PER_HEAD // 2
    src = jnp.arange(3 * EXT_W)[:, None]
    dst = jnp.arange(EXT_W)[None, :]
    src_head, src_piece = src % EXT_W, src // EXT_W
    dst_head, dst_slot = dst // EXT_PER_HEAD, dst % EXT_PER_HEAD
    same_head = (src_head == dst_head) & (src_head < N_HEADS)
    sel_q = (same_head & (dst_slot == half + src_piece)).astype(BF16)
    sel_k = -(same_head & (dst_slot == src_piece)).astype(BF16)
    live = jnp.arange(EXT_W) < N_HEADS * EXT_PER_HEAD
    slot = jnp.arange(EXT_W) % EXT_PER_HEAD
    one_q = (live & (slot < half)).astype(F32).reshape(1, EXT_W)
    one_k = (live & (slot >= half)).astype(F32).reshape(1, EXT_W)
    sel = jnp.concatenate([sel_q, sel_k], axis=1)
    one = jnp.concatenate([one_q, one_k], axis=1)

    k_sds = jax.ShapeDtypeStruct((batch, N_PAIRS, seq, PAIR_W), BF16)
    k_spec = pl.BlockSpec((None, N_PAIRS, TM, PAIR_W), lambda b, i: (b, 0, i, 0))
    vt_sds = jax.ShapeDtypeStruct((batch, N_PAIRS, ns, PAIR_W, TM), BF16)
    vt_spec = pl.BlockSpec((None, N_PAIRS, None, PAIR_W, TM), lambda b, i: (b, 0, i, 0, 0))
    ext_sds = jax.ShapeDtypeStruct((batch, seq, EXT_W), BF16)
    ext_spec = pl.BlockSpec((None, TM, EXT_W), lambda b, i: (b, i, 0))
    ext_t_sds = jax.ShapeDtypeStruct((batch, EXT_W, seq), BF16)
    ext_t_spec = pl.BlockSpec((None, EXT_W, TM), lambda b, i: (b, 0, i))
    return pl.pallas_call(
        _kvf_kernel,
        out_shape=[k_sds, vt_sds, ext_t_sds, ext_sds],
        grid=(batch, ns),
        in_specs=[
            pl.BlockSpec((TM, D_MODEL), lambda b, i: (b * ns + i, 0)),
            _resident((1, D_MODEL), lambda b, i: (0, 0)),
            _resident((D_MODEL, 2 * D_MODEL + EXT_W), lambda b, i: (0, 0)),
            _resident((1, EXT_W), lambda b, i: (0, 0)),
            _resident((TM, TM), lambda b, i: (0, 0)),
            _resident((3 * EXT_W, 2 * EXT_W), lambda b, i: (0, 0)),
            _resident((1, 2 * EXT_W), lambda b, i: (0, 0)),
        ],
        out_specs=[k_spec, vt_spec, ext_t_spec, ext_spec],
        scratch_shapes=[pltpu.VMEM((1, EXT_W), F32)],
        compiler_params=_params("parallel", "arbitrary"),
        name="kvf",
    )(h, g, w, bf, tri, sel, one)


def _fox_kernel(q_ref, k_ref, vt_ref, qe_ref, ke_ref, zero_ref, o_ref, qa_ref, *scratch):
    s_refs = scratch[:-3]
    m_ref, alpha_ref, acc_ref = scratch[-3:]
    dz = zero_ref[0]
    group = pl.program_id(1)
    i = pl.program_id(2)
    heads = [(pp, hh) for pp in range(FOX_PAIRS) for hh in range(2)]
    qe_t = qe_ref[...]
    ext_row = lax.broadcasted_iota(jnp.int32, qe_t.shape, 0)
    for slot, (pp, hh) in enumerate(heads):
        head = 2 * (group * FOX_PAIRS + pp) + hh
        own = (ext_row >= EXT_PER_HEAD * head) & (ext_row < EXT_PER_HEAD * (head + 1))
        qa_ref[slot] = jnp.concatenate(
            [_keep_head_rows(q_ref[pp], hh), jnp.where(own, qe_t, jnp.zeros_like(qe_t))], axis=0)

    half = QB // 2

    n_chunks = QB // LANES

    def key_block(blk, diagonal):
        off = pl.multiple_of(blk * QB, QB)
        k_ext = ke_ref[pl.ds(off, QB), :]
        for slot, (pp, hh) in enumerate(heads):
            ka = jnp.concatenate([k_ref[pp, pl.ds(off, QB), :], k_ext], axis=1)
            if diagonal:
                s_refs[slot][dz, :half, :half] = jnp.dot(
                    ka[:half], qa_ref[slot, :, :half], preferred_element_type=F32)
                s_refs[slot][dz, :, half:] = jnp.dot(
                    ka, qa_ref[slot, :, half:], preferred_element_type=F32)
            else:
                s_refs[slot][dz] = jnp.dot(ka, qa_ref[slot], preferred_element_type=F32)

        def live_keys(c):
            return (c + 1) * LANES if diagonal else QB

        for slot in range(len(heads)):
            for c in range(n_chunks):
                cols = slice(c * LANES, (c + 1) * LANES)
                parts = []
                n_full = c * LANES if diagonal else QB
                if n_full:
                    parts.append(s_refs[slot][dz, :n_full, cols].max(axis=0, keepdims=True))
                if diagonal:
                    tile = slice(c * LANES, (c + 1) * LANES)
                    key_id = lax.broadcasted_iota(jnp.int32, (LANES, LANES), 0)
                    qry_id = lax.broadcasted_iota(jnp.int32, (LANES, LANES), 1)
                    x = jnp.where(key_id <= qry_id, s_refs[slot][dz, tile, cols], NEG_INF)
                    s_refs[slot][dz, tile, cols] = x
                    parts.append(x.max(axis=0, keepdims=True))
                blk_max = functools.reduce(jnp.maximum, parts)
                if diagonal:
                    m_ref[slot, :, cols] = blk_max
                else:
                    m_old = m_ref[slot, :, cols]
                    m_new = jnp.maximum(m_old, blk_max)
                    m_ref[slot, :, cols] = m_new
                    alpha_ref[slot, :, cols] = jnp.exp2(m_old - m_new)
        chunks_per_tile = MXU_TILE // LANES
        ones = jnp.ones((ONES_ROWS, QB), BF16)
        for slot, (pp, hh) in enumerate(heads):
            vt = vt_ref[pp, blk, hh * HEAD_DIM:(hh + 1) * HEAD_DIM, :]
            vt = jnp.concatenate([vt, ones], axis=0)
            for nt in range(QB // MXU_TILE):
                tile_cols = slice(nt * MXU_TILE, (nt + 1) * MXU_TILE)
                n_key_tiles = (nt + 1) if diagonal else QB // MXU_TILE
                pv = None
                for kt in range(n_key_tiles):
                    columns = []
                    for c in range(nt * chunks_per_tile, (nt + 1) * chunks_per_tile):
                        cols = slice(c * LANES, (c + 1) * LANES)
                        m_new = m_ref[slot, :, cols]
                        pieces = []
                        for r in range(kt * MXU_TILE, (kt + 1) * MXU_TILE, KSTRIP):
                            if r < live_keys(c):
                                pr = jnp.exp2(s_refs[slot][dz, r:r + KSTRIP, cols] - m_new)
                                pieces.append(pr.astype(BF16))
                            else:
                                pieces.append(jnp.zeros((KSTRIP, LANES), BF16))
                        columns.append(jnp.concatenate(pieces, axis=0))
                    p_tile = jnp.concatenate(columns, axis=1)
                    part = jnp.dot(vt[:, kt * MXU_TILE:(kt + 1) * MXU_TILE], p_tile,
                                   preferred_element_type=F32)
                    pv = part if pv is None else pv + part
                if diagonal:
                    acc_ref[slot, :, tile_cols] = pv
                else:
                    acc_ref[slot, :, tile_cols] = (
                        alpha_ref[slot, :, tile_cols] * acc_ref[slot, :, tile_cols] + pv)

    def body(j, carry):
        key_block(j, False)
        return carry

    key_block(i, True)
    lax.fori_loop(0, i, body, 0)
    for pp in range(FOX_PAIRS):
        o_t = jnp.concatenate(
            [acc_ref[2 * pp + hh, :HEAD_DIM, :] / acc_ref[2 * pp + hh, HEAD_DIM:HEAD_DIM + 1, :]
             for hh in range(2)], axis=0)
        o_ref[pp] = o_t.T.astype(BF16)


def _fox(q, k, vt, q_ext, k_ext):
    batch, _, seq, _ = k.shape
    nq = seq // QB
    n_heads = 2 * FOX_PAIRS
    assert vt.shape == (batch, N_PAIRS, nq, PAIR_W, QB)
    return pl.pallas_call(
        _fox_kernel,
        out_shape=jax.ShapeDtypeStruct(k.shape, BF16),
        grid=(batch, N_PAIRS // FOX_PAIRS, nq),
        in_specs=[
            pl.BlockSpec((None, FOX_PAIRS, PAIR_W, QB), lambda b, p, i: (b, p, 0, i)),
            _resident((None, FOX_PAIRS, seq, PAIR_W), lambda b, p, i: (b, p, 0, 0)),
            _resident((None, FOX_PAIRS, nq, PAIR_W, QB), lambda b, p, i: (b, p, 0, 0, 0)),
            pl.BlockSpec((None, EXT_W, QB), lambda b, p, i: (b, 0, i)),
            _resident((None, seq, EXT_W), lambda b, p, i: (b, 0, 0)),
            pl.BlockSpec(memory_space=pltpu.SMEM),
        ],
        out_specs=pl.BlockSpec((None, FOX_PAIRS, QB, PAIR_W), lambda b, p, i: (b, p, i, 0)),
        scratch_shapes=(
            [pltpu.VMEM((n_heads, PAIR_W + EXT_W, QB), BF16)]
            + [pltpu.VMEM((1, QB, QB), F32)] * n_heads
            + [pltpu.VMEM((n_heads, 1, QB), F32),
               pltpu.VMEM((n_heads, 1, QB), F32),
               pltpu.VMEM((n_heads, HEAD_DIM + ONES_ROWS, QB), F32)]
        ),
        compiler_params=_params("parallel", "parallel", "arbitrary"),
        name="fox",
    )(q, k, vt, q_ext, k_ext, jnp.zeros((1,), jnp.int32))


def kernel(x, ffn_norm, ffn_w_gate, ffn_w_up, ffn_w_down, mix_norm, a_w_qkv, a_w_o, a_rel_bias,
           kv_norm, b_w_kvf, b_f_bias, b_w_q, b_w_o, final_norm):
    batch, seq, _ = x.shape
    depth = ffn_norm.shape[0]
    n_a = a_w_qkv.shape[0]
    n_b = b_w_q.shape[0]
    assert seq % TM == 0 and seq % QA == 0 and seq % QB == 0 and n_a + n_b == depth

    gf = final_norm.reshape(1, D_MODEL)
    stacks = (ffn_w_gate, ffn_w_up, ffn_w_down)
    ffn_weights = [ffn_w_gate[0, 0].astype(BF16), ffn_w_up[0, 0].astype(BF16),
                   ffn_w_down[0, 0].astype(BF16)]

    def half_ffn(h, layer, pos, **fused):
        nxt = 2 * layer + pos + 1
        cast_next = (stacks, nxt // 2, nxt % 2) if nxt < 2 * depth else None
        outs = _ffn(h, ffn_norm[layer, pos].reshape(1, D_MODEL), tuple(ffn_weights), gf,
                    batch, seq, cast_next=cast_next, **fused)
        if cast_next is not None:
            ffn_weights[:] = outs[-3:]
            outs = outs[:-3]
        return outs

    h = x.reshape(batch * seq, D_MODEL)
    for layer in range(n_a):
        g_mix = mix_norm[layer].reshape(1, D_MODEL)
        h, qt, k, vt = half_ffn(
            h, layer, 0, proj=(g_mix, a_w_qkv[layer].astype(BF16), (True, False, True)))
        o = _attn_a(qt, k, vt, _attn_a_bias_table(a_rel_bias[layer]))
        (h,) = half_ffn(h, layer, 1, mixer=(o, a_w_o[layer].astype(BF16)))

    w_kvf = jnp.pad(b_w_kvf, ((0, 0), (0, EXT_W - N_HEADS))).astype(BF16)
    bf = jnp.pad(b_f_bias, (0, EXT_W - N_HEADS)).reshape(1, EXT_W).astype(F32)
    k_sh, vt_sh, q_ext_t, k_ext = _kvf(h, kv_norm.reshape(1, D_MODEL), w_kvf, bf, batch, seq)

    for lb in range(n_b):
        layer = n_a + lb
        g_mix = mix_norm[layer].reshape(1, D_MODEL)
        h, qt = half_ffn(h, layer, 0, proj=(g_mix, b_w_q[lb].astype(BF16), (True,)))
        o = _fox(qt, k_sh, vt_sh, q_ext_t, k_ext)
        (h,) = half_ffn(h, layer, 1, mixer=(o, b_w_o[lb].astype(BF16)),
                        final=(layer == depth - 1))

    return h.reshape(batch, seq, D_MODEL)
```

```python
import functools

import jax
import jax.numpy as jnp
from jax import lax
from jax.experimental import pallas as pl
from jax.experimental.pallas import tpu as pltpu

D_MODEL = 1024
N_HEADS = 16
HEAD_DIM = 64
N_PAIRS = N_HEADS // 2
PAIR_W = 2 * HEAD_DIM
D_FF = 2816
CHUNK = 64
LEFT_CHUNKS = 8
REL_CLIP = 256
EPS = 1e-6
NEG_INF = -1e30
ATTN_SCALE = HEAD_DIM ** -0.5
LOG2E = 1.4426950408889634
Q_SCALE = ATTN_SCALE * LOG2E
FFN_RES_WEIGHT = 0.5

TM = 512
FF_CHUNK = 256
CAST_BLOCKS = 16
QA = 256
A_KBLOCKS = LEFT_CHUNKS * CHUNK // QA + 1
A_PAIRS = N_PAIRS
A_QSUB = 2
QB = 512
FOX_PAIRS = N_PAIRS
EXT_W = 128
EXT_PER_HEAD = 6
LANES = 128
MXU_TILE = 256
KSTRIP = LANES
ONES_ROWS = 16
VMEM_LIMIT = 56 * 1024 * 1024

F32 = jnp.float32
BF16 = jnp.bfloat16


def _rms_norm(x, g):
    y = x * lax.rsqrt(jnp.mean(x * x, axis=-1, keepdims=True) + EPS)
    return y * g


def _resident(block_shape, index_map):
    return pl.BlockSpec(block_shape, index_map, pipeline_mode=pl.Buffered(1))


def _params(*semantics):
    return pltpu.CompilerParams(dimension_semantics=semantics, vmem_limit_bytes=VMEM_LIMIT)


def _ffn_kernel(*refs, has_mixer, transposed, final, n_cast, cast_blocks):
    n_proj = len(transposed)
    x_ref, g_ref, wg_ref, wu_ref, wd_ref, gf_ref = refs[:6]
    at = 6
    if has_mixer:
        o_ref, wo_ref = refs[at:at + 2]
        at += 2
    if n_proj:
        gm_ref, wp_ref = refs[at:at + 2]
        at += 2
    cast_in = refs[at:at + n_cast]
    at += n_cast
    out_ref = refs[at]
    proj_refs = refs[at + 1:at + 1 + n_proj]
    cast_out = refs[at + 1 + n_proj:at + 1 + n_proj + n_cast]
    a_ref = refs[-1]

    if n_cast:
        @pl.when(pl.program_id(0) * pl.num_programs(1) + pl.program_id(1) < cast_blocks)
        def _():
            for src, dst in zip(cast_in, cast_out):
                dst[...] = src[...].astype(BF16)

    for sub in range(a_ref.shape[0]):
        rows = slice(sub * TM, (sub + 1) * TM)
        x = x_ref[rows, :]
        if has_mixer:
            o = jnp.concatenate([o_ref[p, rows, :] for p in range(N_PAIRS)], axis=1)
            x = x + jnp.dot(o, wo_ref[...], preferred_element_type=F32)
        hn = _rms_norm(x, g_ref[...]).astype(BF16)
        for c in range(D_FF // FF_CHUNK):
            sl = slice(c * FF_CHUNK, (c + 1) * FF_CHUNK)
            gate = jnp.dot(hn, wg_ref[:, sl], preferred_element_type=F32)
            up = jnp.dot(hn, wu_ref[:, sl], preferred_element_type=F32)
            a_ref[sub, :, sl] = (gate * jax.nn.sigmoid(gate) * up).astype(BF16)
        y = jnp.dot(a_ref[sub], wd_ref[...], preferred_element_type=F32)
        out = x + FFN_RES_WEIGHT * y
        if final:
            out = _rms_norm(out, gf_ref[...])
        out_ref[rows, :] = out
        if n_proj:
            hp = _rms_norm(out, gm_ref[...]).astype(BF16)
            for n, p_ref in enumerate(proj_refs):
                y = jnp.dot(hp, wp_ref[:, n * D_MODEL:(n + 1) * D_MODEL],
                            preferred_element_type=F32)
                if n == 0:
                    y = y * Q_SCALE
                for p in range(N_PAIRS):
                    y_pair = y[:, p * PAIR_W:(p + 1) * PAIR_W]
                    if transposed[n]:
                        p_ref[p, :, rows] = y_pair.T.astype(BF16)
                    else:
                        p_ref[p, rows, :] = y_pair.astype(BF16)


def _ffn_sub_tiles(n_side):
    def estimate(n_sub):
        rows = n_sub * TM
        stream = 2 * 2 * rows * D_MODEL * 4
        stream += 2 * n_side * rows * D_MODEL * 2
        resident = (3 * D_FF + n_side * D_MODEL) * D_MODEL * 2
        scratch = rows * D_FF * 2
        live = 4 * TM * D_MODEL * 4
        return stream + resident + scratch + live

    return 2 if estimate(2) <= VMEM_LIMIT else 1


def _ffn(h, g, weights, gf, batch, seq, mixer=None, proj=None, final=False, cast_next=None):
    n_side = (0 if proj is None else len(proj[2])) + (mixer is not None)
    n_sub = _ffn_sub_tiles(n_side)
    tm = n_sub * TM
    ns = seq // tm

    def rows(b, i):
        return (b * ns + i, 0)

    def fixed(b, i):
        return (0, 0)

    pair_spec = pl.BlockSpec((None, N_PAIRS, tm, PAIR_W), lambda b, i: (b, 0, i, 0))
    inputs = [h, g, *weights, gf]
    in_specs = [
        pl.BlockSpec((tm, D_MODEL), rows),
        _resident((1, D_MODEL), fixed),
        _resident((D_MODEL, D_FF), fixed),
        _resident((D_MODEL, D_FF), fixed),
        _resident((D_FF, D_MODEL), fixed),
        _resident((1, D_MODEL), fixed),
    ]
    if mixer is not None:
        inputs += list(mixer)
        in_specs += [pair_spec, _resident((D_MODEL, D_MODEL), fixed)]
    out_shape = [jax.ShapeDtypeStruct(h.shape, F32)]
    out_specs = [pl.BlockSpec((tm, D_MODEL), rows)]
    transposed = ()
    if proj is not None:
        g_mix, w_proj, transposed = proj
        assert w_proj.shape[1] == len(transposed) * D_MODEL
        inputs += [g_mix, w_proj]
        in_specs += [_resident((1, D_MODEL), fixed), _resident(w_proj.shape, fixed)]
        for flag in transposed:
            if flag:
                out_shape.append(jax.ShapeDtypeStruct((batch, N_PAIRS, PAIR_W, seq), BF16))
                out_specs.append(
                    pl.BlockSpec((None, N_PAIRS, PAIR_W, tm), lambda b, i: (b, 0, 0, i)))
            else:
                out_shape.append(jax.ShapeDtypeStruct((batch, N_PAIRS, seq, PAIR_W), BF16))
                out_specs.append(pair_spec)
    n_cast = 0
    cast_blocks = min(CAST_BLOCKS, batch * ns)
    if cast_next is not None:
        stacks, layer, pos = cast_next
        n_cast = len(stacks)
        inputs += list(stacks)
        for stack in stacks:
            n_rows, n_cols = stack.shape[2:]
            cast_rows = n_rows // cast_blocks
            assert cast_rows * cast_blocks == n_rows and cast_rows % 16 == 0
            in_specs.append(pl.BlockSpec(
                (None, None, cast_rows, n_cols),
                lambda b, i: (layer, pos, jnp.minimum(b * ns + i, cast_blocks - 1), 0)))
            out_shape.append(jax.ShapeDtypeStruct((n_rows, n_cols), BF16))
            out_specs.append(pl.BlockSpec(
                (cast_rows, n_cols), lambda b, i: (jnp.minimum(b * ns + i, cast_blocks - 1), 0)))
    return pl.pallas_call(
        functools.partial(_ffn_kernel, has_mixer=mixer is not None, transposed=tuple(transposed),
                          final=final, n_cast=n_cast, cast_blocks=cast_blocks),
        out_shape=out_shape,
        grid=(batch, ns),
        in_specs=in_specs,
        out_specs=out_specs,
        scratch_shapes=[pltpu.VMEM((n_sub, TM, D_FF), BF16)],
        compiler_params=_params("arbitrary", "arbitrary"),
        name="ffn",
    )(*inputs)


def _keep_head_rows(x_t, head_in_pair):
    row = lax.broadcasted_iota(jnp.int32, x_t.shape, 0)
    own = (row >= HEAD_DIM * head_in_pair) & (row < HEAD_DIM * (head_in_pair + 1))
    return jnp.where(own, x_t, jnp.zeros_like(x_t))


def _attn_a_kernel(q_ref, *refs):
    n_kv = A_KBLOCKS + A_QSUB - 1
    k_refs = refs[:n_kv]
    vt_refs = refs[n_kv:2 * n_kv]
    bias_ref, zero_ref, o_ref = refs[2 * n_kv:2 * n_kv + 3]
    s_refs = refs[2 * n_kv + 3:-1]
    m_ref = refs[-1]
    dz = zero_ref[0]
    i = pl.program_id(2)
    heads = [(pp, hh) for pp in range(A_PAIRS) for hh in range(2)]
    chunks = [slice(c * LANES, (c + 1) * LANES) for c in range(QA // LANES)]
    ones = jnp.ones((ONES_ROWS, QA), BF16)
    for sub in range(A_QSUB):
        q_cols = slice(sub * QA, (sub + 1) * QA)
        sub_refs = s_refs[sub * len(heads):(sub + 1) * len(heads)]
        pens = [jnp.where(i * A_QSUB + sub - (A_KBLOCKS - 1) + j >= 0, 0.0, NEG_INF).astype(F32)
                for j in range(A_KBLOCKS)]
        for (pp, hh), s_ref in zip(heads, sub_refs):
            qm_t = _keep_head_rows(q_ref[pp, :, q_cols], hh)
            for j in range(A_KBLOCKS):
                s_ref[j + dz] = jnp.dot(k_refs[sub + j][pp], qm_t, preferred_element_type=F32)
        for (pp, hh), s_ref in zip(heads, sub_refs):
            for cols in chunks:
                blk_max = []
                for j in range(A_KBLOCKS):
                    x = s_ref[j + dz, :, cols] + bias_ref[2 * pp + hh, j, :, cols]
                    s_ref[j + dz, :, cols] = x
                    blk_max.append(x.max(axis=0, keepdims=True) + pens[j])
                m_ref[sub, pp, hh, :, cols] = functools.reduce(jnp.maximum, blk_max)
        for pp in range(A_PAIRS):
            outs = []
            for hh in range(2):
                s_ref = sub_refs[2 * pp + hh]
                pv = None
                for j in range(A_KBLOCKS):
                    columns = []
                    for cols in chunks:
                        shift = m_ref[sub, pp, hh, :, cols] - pens[j]
                        pieces = [
                            jnp.exp2(s_ref[j + dz, r:r + KSTRIP, cols] - shift).astype(BF16)
                            for r in range(0, QA, KSTRIP)]
                        columns.append(jnp.concatenate(pieces, axis=0))
                    p_tile = jnp.concatenate(columns, axis=1)
                    vt = vt_refs[sub + j][pp, hh * HEAD_DIM:(hh + 1) * HEAD_DIM, :]
                    part = jnp.dot(jnp.concatenate([vt, ones], axis=0), p_tile,
                                   preferred_element_type=F32)
                    pv = part if pv is None else pv + part
                outs.append(pv[:HEAD_DIM] / pv[HEAD_DIM:HEAD_DIM + 1])
            o_ref[pp, q_cols, :] = jnp.concatenate(outs, axis=0).T.astype(BF16)


def _attn_a(q, k, vt, bias_t):
    batch, _, seq, _ = k.shape
    n_steps = seq // (A_QSUB * QA)
    n_kv = A_KBLOCKS + A_QSUB - 1

    def k_spec(t):
        back = A_KBLOCKS - 1 - t
        return pl.BlockSpec((None, A_PAIRS, QA, PAIR_W),
                            lambda b, p, i: (b, p, jnp.maximum(i * A_QSUB - back, 0), 0))

    def vt_spec(t):
        back = A_KBLOCKS - 1 - t
        return pl.BlockSpec((None, A_PAIRS, PAIR_W, QA),
                            lambda b, p, i: (b, p, 0, jnp.maximum(i * A_QSUB - back, 0)))

    return pl.pallas_call(
        _attn_a_kernel,
        out_shape=jax.ShapeDtypeStruct(k.shape, BF16),
        grid=(batch, N_PAIRS // A_PAIRS, n_steps),
        in_specs=(
            [pl.BlockSpec((None, A_PAIRS, PAIR_W, A_QSUB * QA), lambda b, p, i: (b, p, 0, i))]
            + [k_spec(t) for t in range(n_kv)]
            + [vt_spec(t) for t in range(n_kv)]
            + [_resident((N_HEADS, A_KBLOCKS, QA, QA), lambda b, p, i: (0, 0, 0, 0))]
            + [pl.BlockSpec(memory_space=pltpu.SMEM)]
        ),
        out_specs=pl.BlockSpec((None, A_PAIRS, A_QSUB * QA, PAIR_W),
                               lambda b, p, i: (b, p, i, 0)),
        scratch_shapes=(
            [pltpu.VMEM((A_KBLOCKS, QA, QA), F32)] * (A_QSUB * 2 * A_PAIRS)
            + [pltpu.VMEM((A_QSUB, A_PAIRS, 2, 1, QA), F32)]
        ),
        compiler_params=_params("parallel", "parallel", "arbitrary"),
        name="attn_a",
    )(q, *([k] * n_kv), *([vt] * n_kv), bias_t, jnp.zeros((1,), jnp.int32))


def _bias_table_kernel(vec_ref, out_ref):
    back = (A_KBLOCKS - 1) * QA
    key = lax.broadcasted_iota(jnp.int32, (QA, QA), 0)
    qry = lax.broadcasted_iota(jnp.int32, (QA, QA), 1)
    chunk_shift = CHUNK.bit_length() - 1
    for j in range(A_KBLOCKS):
        rolled = pltpu.roll(jnp.broadcast_to(vec_ref[j], (QA, 2 * QA)), 0, 1,
                            stride=1, stride_axis=0)
        dchunk = (lax.shift_right_logical(qry + back, chunk_shift)
                  - lax.shift_right_logical(key + j * QA, chunk_shift))
        valid = (dchunk >= 0) & (dchunk <= LEFT_CHUNKS)
        out_ref[j] = jnp.where(valid, rolled[:, QA:], NEG_INF)


def _attn_a_bias_table(rel_bias):
    n_heads = rel_bias.shape[0]
    back = (A_KBLOCKS - 1) * QA
    d = (back - QA * jnp.arange(A_KBLOCKS))[:, None] + jnp.arange(2 * QA)[None, :] - QA
    vec = jnp.take(rel_bias.astype(F32) * LOG2E, jnp.clip(d, -REL_CLIP, REL_CLIP) + REL_CLIP,
                   axis=1)
    vec = vec.reshape(n_heads, A_KBLOCKS, 1, 2 * QA)
    return pl.pallas_call(
        _bias_table_kernel,
        out_shape=jax.ShapeDtypeStruct((n_heads, A_KBLOCKS, QA, QA), F32),
        grid=(n_heads,),
        in_specs=[pl.BlockSpec((None, A_KBLOCKS, 1, 2 * QA), lambda h: (h, 0, 0, 0))],
        out_specs=pl.BlockSpec((None, A_KBLOCKS, QA, QA), lambda h: (h, 0, 0, 0)),
        compiler_params=_params("parallel"),
        name="bias_table",
    )(vec)


def _split3(x):
    hi = x.astype(BF16)
    r1 = x - hi.astype(F32)
    mid = r1.astype(BF16)
    lo = (r1 - mid.astype(F32)).astype(BF16)
    return hi, mid, lo


def _kvf_kernel(x_ref, g_ref, w_ref, bf_ref, tri_ref, sel_ref, one_ref,
                k_ref, vt_ref, qe_ref, ke_ref, carry_ref):
    @pl.when(pl.program_id(1) == 0)
    def _():
        carry_ref[...] = jnp.zeros_like(carry_ref)

    hn = _rms_norm(x_ref[...], g_ref[...]).astype(BF16)
    y = jnp.dot(hn, w_ref[...], preferred_element_type=F32)
    for p in range(N_PAIRS):
        k_cols = slice(p * PAIR_W, (p + 1) * PAIR_W)
        v_cols = slice(D_MODEL + p * PAIR_W, D_MODEL + (p + 1) * PAIR_W)
        k_ref[p] = y[:, k_cols].astype(BF16)
        vt_ref[p] = y[:, v_cols].T.astype(BF16)
    z = y[:, 2 * D_MODEL:] + bf_ref[...]
    log_f = jnp.minimum(z, 0.0) - jnp.log1p(jnp.exp(-jnp.abs(z)))
    cum3 = jnp.dot(tri_ref[...], jnp.concatenate(_split3(log_f), axis=1),
                   preferred_element_type=F32)
    cum = (cum3[:, :EXT_W] + cum3[:, EXT_W:2 * EXT_W] + cum3[:, 2 * EXT_W:]) + carry_ref[...]
    carry_ref[...] = cum[TM - 1:TM, :]
    pieces = jnp.concatenate(_split3(cum * LOG2E), axis=1)
    ext = jnp.dot(pieces, sel_ref[...], preferred_element_type=F32) + one_ref[...]
    qe_ref[...] = ext[:, :EXT_W].T.astype(BF16)
    ke_ref[...] = ext[:, EXT_W:].astype(BF16)


def _kvf(h, g, w, bf, batch, seq):
    ns = seq // TM
    tri = (jnp.arange(TM)[:, None] >= jnp.arange(TM)[None, :]).astype(BF16)
    half = EXT_PER_HEAD // 2
    src = jnp.arange(3 * EXT_W)[:, None]
    dst = jnp.arange(EXT_W)[None, :]
    src_head, src_piece = src % EXT_W, src // EXT_W
    dst_head, dst_slot = dst // EXT_PER_HEAD, dst % EXT_PER_HEAD
    same_head = (src_head == dst_head) & (src_head < N_HEADS)
    sel_q = (same_head & (dst_slot == half + src_piece)).astype(BF16)
    sel_k = -(same_head & (dst_slot == src_piece)).astype(BF16)
    live = jnp.arange(EXT_W) < N_HEADS * EXT_PER_HEAD
    slot = jnp.arange(EXT_W) % EXT_PER_HEAD
    one_q = (live & (slot < half)).astype(F32).reshape(1, EXT_W)
    one_k = (live & (slot >= half)).astype(F32).reshape(1, EXT_W)
    sel = jnp.concatenate([sel_q, sel_k], axis=1)
    one = jnp.concatenate([one_q, one_k], axis=1)

    k_sds = jax.ShapeDtypeStruct((batch, N_PAIRS, seq, PAIR_W), BF16)
    k_spec = pl.BlockSpec((None, N_PAIRS, TM, PAIR_W), lambda b, i: (b, 0, i, 0))
    vt_sds = jax.ShapeDtypeStruct((batch, N_PAIRS, ns, PAIR_W, TM), BF16)
    vt_spec = pl.BlockSpec((None, N_PAIRS, None, PAIR_W, TM), lambda b, i: (b, 0, i, 0, 0))
    ext_sds = jax.ShapeDtypeStruct((batch, seq, EXT_W), BF16)
    ext_spec = pl.BlockSpec((None, TM, EXT_W), lambda b, i: (b, i, 0))
    ext_t_sds = jax.ShapeDtypeStruct((batch, EXT_W, seq), BF16)
    ext_t_spec = pl.BlockSpec((None, EXT_W, TM), lambda b, i: (b, 0, i))
    return pl.pallas_call(
        _kvf_kernel,
        out_shape=[k_sds, vt_sds, ext_t_sds, ext_sds],
        grid=(batch, ns),
        in_specs=[
            pl.BlockSpec((TM, D_MODEL), lambda b, i: (b * ns + i, 0)),
            _resident((1, D_MODEL), lambda b, i: (0, 0)),
            _resident((D_MODEL, 2 * D_MODEL + EXT_W), lambda b, i: (0, 0)),
            _resident((1, EXT_W), lambda b, i: (0, 0)),
            _resident((TM, TM), lambda b, i: (0, 0)),
            _resident((3 * EXT_W, 2 * EXT_W), lambda b, i: (0, 0)),
            _resident((1, 2 * EXT_W), lambda b, i: (0, 0)),
        ],
        out_specs=[k_spec, vt_spec, ext_t_spec, ext_spec],
        scratch_shapes=[pltpu.VMEM((1, EXT_W), F32)],
        compiler_params=_params("parallel", "arbitrary"),
        name="kvf",
    )(h, g, w, bf, tri, sel, one)


def _fox_kernel(q_ref, k_ref, vt_ref, qe_ref, ke_ref, zero_ref, o_ref, qa_ref, *scratch):
    s_refs = scratch[:-3]
    m_ref, alpha_ref, acc_ref = scratch[-3:]
    dz = zero_ref[0]
    group = pl.program_id(1)
    i = pl.program_id(2)
    heads = [(pp, hh) for pp in range(FOX_PAIRS) for hh in range(2)]
    qe_t = qe_ref[...]
    ext_row = lax.broadcasted_iota(jnp.int32, qe_t.shape, 0)
    for slot, (pp, hh) in enumerate(heads):
        head = 2 * (group * FOX_PAIRS + pp) + hh
        own = (ext_row >= EXT_PER_HEAD * head) & (ext_row < EXT_PER_HEAD * (head + 1))
        qa_ref[slot] = jnp.concatenate(
            [_keep_head_rows(q_ref[pp], hh), jnp.where(own, qe_t, jnp.zeros_like(qe_t))], axis=0)

    half = QB // 2

    n_chunks = QB // LANES

    def key_block(blk, diagonal):
        off = pl.multiple_of(blk * QB, QB)
        k_ext = ke_ref[pl.ds(off, QB), :]
        for slot, (pp, hh) in enumerate(heads):
            ka = jnp.concatenate([k_ref[pp, pl.ds(off, QB), :], k_ext], axis=1)
            if diagonal:
                s_refs[slot][dz, :half, :half] = jnp.dot(
                    ka[:half], qa_ref[slot, :, :half], preferred_element_type=F32)
                s_refs[slot][dz, :, half:] = jnp.dot(
                    ka, qa_ref[slot, :, half:], preferred_element_type=F32)
            else:
                s_refs[slot][dz] = jnp.dot(ka, qa_ref[slot], preferred_element_type=F32)

        def live_keys(c):
            return (c + 1) * LANES if diagonal else QB

        for slot in range(len(heads)):
            for c in range(n_chunks):
                cols = slice(c * LANES, (c + 1) * LANES)
                parts = []
                n_full = c * LANES if diagonal else QB
                if n_full:
                    parts.append(s_refs[slot][dz, :n_full, cols].max(axis=0, keepdims=True))
                if diagonal:
                    tile = slice(c * LANES, (c + 1) * LANES)
                    key_id = lax.broadcasted_iota(jnp.int32, (LANES, LANES), 0)
                    qry_id = lax.broadcasted_iota(jnp.int32, (LANES, LANES), 1)
                    x = jnp.where(key_id <= qry_id, s_refs[slot][dz, tile, cols], NEG_INF)
                    s_refs[slot][dz, tile, cols] = x
                    parts.append(x.max(axis=0, keepdims=True))
                blk_max = functools.reduce(jnp.maximum, parts)
                if diagonal:
                    m_ref[slot, :, cols] = blk_max
                else:
                    m_old = m_ref[slot, :, cols]
                    m_new = jnp.maximum(m_old, blk_max)
                    m_ref[slot, :, cols] = m_new
                    alpha_ref[slot, :, cols] = jnp.exp2(m_old - m_new)
        chunks_per_tile = MXU_TILE // LANES
        ones = jnp.ones((ONES_ROWS, QB), BF16)
        for slot, (pp, hh) in enumerate(heads):
            vt = vt_ref[pp, blk, hh * HEAD_DIM:(hh + 1) * HEAD_DIM, :]
            vt = jnp.concatenate([vt, ones], axis=0)
            for nt in range(QB // MXU_TILE):
                tile_cols = slice(nt * MXU_TILE, (nt + 1) * MXU_TILE)
                n_key_tiles = (nt + 1) if diagonal else QB // MXU_TILE
                pv = None
                for kt in range(n_key_tiles):
                    columns = []
                    for c in range(nt * chunks_per_tile, (nt + 1) * chunks_per_tile):
                        cols = slice(c * LANES, (c + 1) * LANES)
                        m_new = m_ref[slot, :, cols]
                        pieces = []
                        for r in range(kt * MXU_TILE, (kt + 1) * MXU_TILE, KSTRIP):
                            if r < live_keys(c):
                                pr = jnp.exp2(s_refs[slot][dz, r:r + KSTRIP, cols] - m_new)
                                pieces.append(pr.astype(BF16))
                            else:
                                pieces.append(jnp.zeros((KSTRIP, LANES), BF16))
                        columns.append(jnp.concatenate(pieces, axis=0))
                    p_tile = jnp.concatenate(columns, axis=1)
                    part = jnp.dot(vt[:, kt * MXU_TILE:(kt + 1) * MXU_TILE], p_tile,
                                   preferred_element_type=F32)
                    pv = part if pv is None else pv + part
                if diagonal:
                    acc_ref[slot, :, tile_cols] = pv
                else:
                    acc_ref[slot, :, tile_cols] = (
                        alpha_ref[slot, :, tile_cols] * acc_ref[slot, :, tile_cols] + pv)

    def body(j, carry):
        key_block(j, False)
        return carry

    key_block(i, True)
    lax.fori_loop(0, i, body, 0)
    for pp in range(FOX_PAIRS):
        o_t = jnp.concatenate(
            [acc_ref[2 * pp + hh, :HEAD_DIM, :] / acc_ref[2 * pp + hh, HEAD_DIM:HEAD_DIM + 1, :]
             for hh in range(2)], axis=0)
        o_ref[pp] = o_t.T.astype(BF16)


def _fox(q, k, vt, q_ext, k_ext):
    batch, _, seq, _ = k.shape
    nq = seq // QB
    n_heads = 2 * FOX_PAIRS
    assert vt.shape == (batch, N_PAIRS, nq, PAIR_W, QB)
    return pl.pallas_call(
        _fox_kernel,
        out_shape=jax.ShapeDtypeStruct(k.shape, BF16),
        grid=(batch, N_PAIRS // FOX_PAIRS, nq),
        in_specs=[
            pl.BlockSpec((None, FOX_PAIRS, PAIR_W, QB), lambda b, p, i: (b, p, 0, i)),
            _resident((None, FOX_PAIRS, seq, PAIR_W), lambda b, p, i: (b, p, 0, 0)),
            _resident((None, FOX_PAIRS, nq, PAIR_W, QB), lambda b, p, i: (b, p, 0, 0, 0)),
            pl.BlockSpec((None, EXT_W, QB), lambda b, p, i: (b, 0, i)),
            _resident((None, seq, EXT_W), lambda b, p, i: (b, 0, 0)),
            pl.BlockSpec(memory_space=pltpu.SMEM),
        ],
        out_specs=pl.BlockSpec((None, FOX_PAIRS, QB, PAIR_W), lambda b, p, i: (b, p, i, 0)),
        scratch_shapes=(
            [pltpu.VMEM((n_heads, PAIR_W + EXT_W, QB), BF16)]
            + [pltpu.VMEM((1, QB, QB), F32)] * n_heads
            + [pltpu.VMEM((n_heads, 1, QB), F32),
               pltpu.VMEM((n_heads, 1, QB), F32),
               pltpu.VMEM((n_heads, HEAD_DIM + ONES_ROWS, QB), F32)]
        ),
        compiler_params=_params("parallel", "parallel", "arbitrary"),
        name="fox",
    )(q, k, vt, q_ext, k_ext, jnp.zeros((1,), jnp.int32))


def kernel(x, ffn_norm, ffn_w_gate, ffn_w_up, ffn_w_down, mix_norm, a_w_qkv, a_w_o, a_rel_bias,
           kv_norm, b_w_kvf, b_f_bias, b_w_q, b_w_o, final_norm):
    batch, seq, _ = x.shape
    depth = ffn_norm.shape[0]
    n_a = a_w_qkv.shape[0]
    n_b = b_w_q.shape[0]
    assert seq % (2 * TM) == 0 and seq % (A_QSUB * QA) == 0 and seq % QB == 0
    assert n_a + n_b == depth

    gf = final_norm.reshape(1, D_MODEL)
    stacks = (ffn_w_gate, ffn_w_up, ffn_w_down)
    ffn_weights = [ffn_w_gate[0, 0].astype(BF16), ffn_w_up[0, 0].astype(BF16),
                   ffn_w_down[0, 0].astype(BF16)]

    def half_ffn(h, layer, pos, **fused):
        nxt = 2 * layer + pos + 1
        cast_next = (stacks, nxt // 2, nxt % 2) if nxt < 2 * depth else None
        outs = _ffn(h, ffn_norm[layer, pos].reshape(1, D_MODEL), tuple(ffn_weights), gf,
                    batch, seq, cast_next=cast_next, **fused)
        if cast_next is not None:
            ffn_weights[:] = outs[-3:]
            outs = outs[:-3]
        return outs

    h = x.reshape(batch * seq, D_MODEL)
    for layer in range(n_a):
        g_mix = mix_norm[layer].reshape(1, D_MODEL)
        h, qt, k, vt = half_ffn(
            h, layer, 0, proj=(g_mix, a_w_qkv[layer].astype(BF16), (True, False, True)))
        o = _attn_a(qt, k, vt, _attn_a_bias_table(a_rel_bias[layer]))
        (h,) = half_ffn(h, layer, 1, mixer=(o, a_w_o[layer].astype(BF16)))

    w_kvf = jnp.pad(b_w_kvf, ((0, 0), (0, EXT_W - N_HEADS))).astype(BF16)
    bf = jnp.pad(b_f_bias, (0, EXT_W - N_HEADS)).reshape(1, EXT_W).astype(F32)
    k_sh, vt_sh, q_ext_t, k_ext = _kvf(h, kv_norm.reshape(1, D_MODEL), w_kvf, bf, batch, seq)

    for lb in range(n_b):
        layer = n_a + lb
        g_mix = mix_norm[layer].reshape(1, D_MODEL)
        h, qt = half_ffn(h, layer, 0, proj=(g_mix, b_w_q[lb].astype(BF16), (True,)))
        o = _fox(qt, k_sh, vt_sh, q_ext_t, k_ext)
        (h,) = half_ffn(h, layer, 1, mixer=(o, b_w_o[lb].astype(BF16)),
                        final=(layer == depth - 1))

    return h.reshape(batch, seq, D_MODEL)
```

```python
import functools

import jax
import jax.numpy as jnp
from jax import lax
from jax.experimental import pallas as pl
from jax.experimental.pallas import tpu as pltpu

D_MODEL = 1024
N_HEADS = 16
HEAD_DIM = 64
N_PAIRS = N_HEADS // 2
PAIR_W = 2 * HEAD_DIM
D_FF = 2816
CHUNK = 64
LEFT_CHUNKS = 8
REL_CLIP = 256
EPS = 1e-6
NEG_INF = -1e30
ATTN_SCALE = HEAD_DIM ** -0.5
LOG2E = 1.4426950408889634
Q_SCALE = ATTN_SCALE * LOG2E
FFN_RES_WEIGHT = 0.5

TM = 512
FF_CHUNK = 256
CAST_BLOCKS = 16
KVF_SUB = 2
QA = 256
A_KBLOCKS = LEFT_CHUNKS * CHUNK // QA + 1
A_PAIRS = N_PAIRS
A_QSUB = 2
QB = 512
FOX_PAIRS = N_PAIRS
EXT_W = 128
EXT_PER_HEAD = 6
LANES = 128
MXU_TILE = 256
KSTRIP = LANES
ONES_ROWS = 16
VMEM_LIMIT = 56 * 1024 * 1024

F32 = jnp.float32
BF16 = jnp.bfloat16


def _rms_norm(x, g):
    y = x * lax.rsqrt(jnp.mean(x * x, axis=-1, keepdims=True) + EPS)
    return y * g


def _resident(block_shape, index_map):
    return pl.BlockSpec(block_shape, index_map, pipeline_mode=pl.Buffered(1))


def _params(*semantics):
    return pltpu.CompilerParams(dimension_semantics=semantics, vmem_limit_bytes=VMEM_LIMIT)


def _ffn_kernel(*refs, has_mixer, transposed, final, n_cast, cast_blocks):
    n_proj = len(transposed)
    x_ref, g_ref, wg_ref, wu_ref, wd_ref, gf_ref = refs[:6]
    at = 6
    if has_mixer:
        o_ref, wo_ref = refs[at:at + 2]
        at += 2
    if n_proj:
        gm_ref, wp_ref = refs[at:at + 2]
        at += 2
    cast_in = refs[at:at + n_cast]
    at += n_cast
    out_ref = refs[at]
    proj_refs = refs[at + 1:at + 1 + n_proj]
    cast_out = refs[at + 1 + n_proj:at + 1 + n_proj + n_cast]
    a_ref = refs[-1]

    if n_cast:
        @pl.when(pl.program_id(0) * pl.num_programs(1) + pl.program_id(1) < cast_blocks)
        def _():
            for src, dst in zip(cast_in, cast_out):
                dst[...] = src[...].astype(BF16)

    for sub in range(a_ref.shape[0]):
        rows = slice(sub * TM, (sub + 1) * TM)
        x = x_ref[rows, :]
        if has_mixer:
            o = jnp.concatenate([o_ref[p, rows, :] for p in range(N_PAIRS)], axis=1)
            x = x + jnp.dot(o, wo_ref[...], preferred_element_type=F32)
        hn = _rms_norm(x, g_ref[...]).astype(BF16)
        for c in range(D_FF // FF_CHUNK):
            sl = slice(c * FF_CHUNK, (c + 1) * FF_CHUNK)
            gate = jnp.dot(hn, wg_ref[:, sl], preferred_element_type=F32)
            up = jnp.dot(hn, wu_ref[:, sl], preferred_element_type=F32)
            a_ref[sub, :, sl] = (gate * jax.nn.sigmoid(gate) * up).astype(BF16)
        y = jnp.dot(a_ref[sub], wd_ref[...], preferred_element_type=F32)
        out = x + FFN_RES_WEIGHT * y
        if final:
            out = _rms_norm(out, gf_ref[...])
        out_ref[rows, :] = out
        if n_proj:
            hp = _rms_norm(out, gm_ref[...]).astype(BF16)
            for n, p_ref in enumerate(proj_refs):
                y = jnp.dot(hp, wp_ref[:, n * D_MODEL:(n + 1) * D_MODEL],
                            preferred_element_type=F32)
                if n == 0:
                    y = y * Q_SCALE
                for p in range(N_PAIRS):
                    y_pair = y[:, p * PAIR_W:(p + 1) * PAIR_W]
                    if transposed[n]:
                        p_ref[p, :, rows] = y_pair.T.astype(BF16)
                    else:
                        p_ref[p, rows, :] = y_pair.astype(BF16)


def _ffn_sub_tiles(n_side):
    def estimate(n_sub):
        rows = n_sub * TM
        stream = 2 * 2 * rows * D_MODEL * 4
        stream += 2 * n_side * rows * D_MODEL * 2
        resident = (3 * D_FF + n_side * D_MODEL) * D_MODEL * 2
        scratch = rows * D_FF * 2
        live = 4 * TM * D_MODEL * 4
        return stream + resident + scratch + live

    return 2 if estimate(2) <= VMEM_LIMIT else 1


def _ffn(h, g, weights, gf, batch, seq, mixer=None, proj=None, final=False, cast_next=None):
    n_side = (0 if proj is None else len(proj[2])) + (mixer is not None)
    n_sub = _ffn_sub_tiles(n_side)
    tm = n_sub * TM
    ns = seq // tm

    def rows(b, i):
        return (b * ns + i, 0)

    def fixed(b, i):
        return (0, 0)

    pair_spec = pl.BlockSpec((None, N_PAIRS, tm, PAIR_W), lambda b, i: (b, 0, i, 0))
    inputs = [h, g, *weights, gf]
    in_specs = [
        pl.BlockSpec((tm, D_MODEL), rows),
        _resident((1, D_MODEL), fixed),
        _resident((D_MODEL, D_FF), fixed),
        _resident((D_MODEL, D_FF), fixed),
        _resident((D_FF, D_MODEL), fixed),
        _resident((1, D_MODEL), fixed),
    ]
    if mixer is not None:
        inputs += list(mixer)
        in_specs += [pair_spec, _resident((D_MODEL, D_MODEL), fixed)]
    out_shape = [jax.ShapeDtypeStruct(h.shape, F32)]
    out_specs = [pl.BlockSpec((tm, D_MODEL), rows)]
    transposed = ()
    if proj is not None:
        g_mix, w_proj, transposed = proj
        assert w_proj.shape[1] == len(transposed) * D_MODEL
        inputs += [g_mix, w_proj]
        in_specs += [_resident((1, D_MODEL), fixed), _resident(w_proj.shape, fixed)]
        for flag in transposed:
            if flag:
                out_shape.append(jax.ShapeDtypeStruct((batch, N_PAIRS, PAIR_W, seq), BF16))
                out_specs.append(
                    pl.BlockSpec((None, N_PAIRS, PAIR_W, tm), lambda b, i: (b, 0, 0, i)))
            else:
                out_shape.append(jax.ShapeDtypeStruct((batch, N_PAIRS, seq, PAIR_W), BF16))
                out_specs.append(pair_spec)
    n_cast = 0
    cast_blocks = min(CAST_BLOCKS, batch * ns)
    if cast_next is not None:
        stacks, layer, pos = cast_next
        n_cast = len(stacks)
        inputs += list(stacks)
        for stack in stacks:
            n_rows, n_cols = stack.shape[2:]
            cast_rows = n_rows // cast_blocks
            assert cast_rows * cast_blocks == n_rows and cast_rows % 16 == 0
            in_specs.append(pl.BlockSpec(
                (None, None, cast_rows, n_cols),
                lambda b, i: (layer, pos, jnp.minimum(b * ns + i, cast_blocks - 1), 0)))
            out_shape.append(jax.ShapeDtypeStruct((n_rows, n_cols), BF16))
            out_specs.append(pl.BlockSpec(
                (cast_rows, n_cols), lambda b, i: (jnp.minimum(b * ns + i, cast_blocks - 1), 0)))
    return pl.pallas_call(
        functools.partial(_ffn_kernel, has_mixer=mixer is not None, transposed=tuple(transposed),
                          final=final, n_cast=n_cast, cast_blocks=cast_blocks),
        out_shape=out_shape,
        grid=(batch, ns),
        in_specs=in_specs,
        out_specs=out_specs,
        scratch_shapes=[pltpu.VMEM((n_sub, TM, D_FF), BF16)],
        compiler_params=_params("arbitrary", "arbitrary"),
        name="ffn",
    )(*inputs)


def _keep_head_rows(x_t, head_in_pair):
    row = lax.broadcasted_iota(jnp.int32, x_t.shape, 0)
    own = (row >= HEAD_DIM * head_in_pair) & (row < HEAD_DIM * (head_in_pair + 1))
    return jnp.where(own, x_t, jnp.zeros_like(x_t))


def _attn_a_kernel(q_ref, *refs):
    n_kv = A_KBLOCKS + A_QSUB - 1
    k_refs = refs[:n_kv]
    vt_refs = refs[n_kv:2 * n_kv]
    bias_ref, zero_ref, o_ref = refs[2 * n_kv:2 * n_kv + 3]
    s_refs = refs[2 * n_kv + 3:-1]
    m_ref = refs[-1]
    dz = zero_ref[0]
    i = pl.program_id(2)
    heads = [(pp, hh) for pp in range(A_PAIRS) for hh in range(2)]
    chunks = [slice(c * LANES, (c + 1) * LANES) for c in range(QA // LANES)]
    ones = jnp.ones((ONES_ROWS, QA), BF16)
    for sub in range(A_QSUB):
        q_cols = slice(sub * QA, (sub + 1) * QA)
        sub_refs = s_refs[sub * len(heads):(sub + 1) * len(heads)]
        pens = [jnp.where(i * A_QSUB + sub - (A_KBLOCKS - 1) + j >= 0, 0.0, NEG_INF).astype(F32)
                for j in range(A_KBLOCKS)]
        for (pp, hh), s_ref in zip(heads, sub_refs):
            qm_t = _keep_head_rows(q_ref[pp, :, q_cols], hh)
            for j in range(A_KBLOCKS):
                s_ref[j + dz] = jnp.dot(k_refs[sub + j][pp], qm_t, preferred_element_type=F32)
        for (pp, hh), s_ref in zip(heads, sub_refs):
            for cols in chunks:
                blk_max = []
                for j in range(A_KBLOCKS):
                    x = s_ref[j + dz, :, cols] + bias_ref[2 * pp + hh, j, :, cols]
                    s_ref[j + dz, :, cols] = x
                    blk_max.append(x.max(axis=0, keepdims=True) + pens[j])
                m_ref[sub, pp, hh, :, cols] = functools.reduce(jnp.maximum, blk_max)
        for pp in range(A_PAIRS):
            outs = []
            for hh in range(2):
                s_ref = sub_refs[2 * pp + hh]
                pv = None
                for j in range(A_KBLOCKS):
                    columns = []
                    for cols in chunks:
                        shift = m_ref[sub, pp, hh, :, cols] - pens[j]
                        pieces = [
                            jnp.exp2(s_ref[j + dz, r:r + KSTRIP, cols] - shift).astype(BF16)
                            for r in range(0, QA, KSTRIP)]
                        columns.append(jnp.concatenate(pieces, axis=0))
                    p_tile = jnp.concatenate(columns, axis=1)
                    vt = vt_refs[sub + j][pp, hh * HEAD_DIM:(hh + 1) * HEAD_DIM, :]
                    part = jnp.dot(jnp.concatenate([vt, ones], axis=0), p_tile,
                                   preferred_element_type=F32)
                    pv = part if pv is None else pv + part
                outs.append(pv[:HEAD_DIM] / pv[HEAD_DIM:HEAD_DIM + 1])
            o_ref[pp, q_cols, :] = jnp.concatenate(outs, axis=0).T.astype(BF16)


def _attn_a(q, k, vt, bias_t):
    batch, _, seq, _ = k.shape
    n_steps = seq // (A_QSUB * QA)
    n_kv = A_KBLOCKS + A_QSUB - 1

    def k_spec(t):
        back = A_KBLOCKS - 1 - t
        return pl.BlockSpec((None, A_PAIRS, QA, PAIR_W),
                            lambda b, p, i: (b, p, jnp.maximum(i * A_QSUB - back, 0), 0))

    def vt_spec(t):
        back = A_KBLOCKS - 1 - t
        return pl.BlockSpec((None, A_PAIRS, PAIR_W, QA),
                            lambda b, p, i: (b, p, 0, jnp.maximum(i * A_QSUB - back, 0)))

    return pl.pallas_call(
        _attn_a_kernel,
        out_shape=jax.ShapeDtypeStruct(k.shape, BF16),
        grid=(batch, N_PAIRS // A_PAIRS, n_steps),
        in_specs=(
            [pl.BlockSpec((None, A_PAIRS, PAIR_W, A_QSUB * QA), lambda b, p, i: (b, p, 0, i))]
            + [k_spec(t) for t in range(n_kv)]
            + [vt_spec(t) for t in range(n_kv)]
            + [_resident((N_HEADS, A_KBLOCKS, QA, QA), lambda b, p, i: (0, 0, 0, 0))]
            + [pl.BlockSpec(memory_space=pltpu.SMEM)]
        ),
        out_specs=pl.BlockSpec((None, A_PAIRS, A_QSUB * QA, PAIR_W),
                               lambda b, p, i: (b, p, i, 0)),
        scratch_shapes=(
            [pltpu.VMEM((A_KBLOCKS, QA, QA), F32)] * (A_QSUB * 2 * A_PAIRS)
            + [pltpu.VMEM((A_QSUB, A_PAIRS, 2, 1, QA), F32)]
        ),
        compiler_params=_params("parallel", "parallel", "arbitrary"),
        name="attn_a",
    )(q, *([k] * n_kv), *([vt] * n_kv), bias_t, jnp.zeros((1,), jnp.int32))


def _bias_table_kernel(vec_ref, out_ref):
    back = (A_KBLOCKS - 1) * QA
    key = lax.broadcasted_iota(jnp.int32, (QA, QA), 0)
    qry = lax.broadcasted_iota(jnp.int32, (QA, QA), 1)
    chunk_shift = CHUNK.bit_length() - 1
    for j in range(A_KBLOCKS):
        rolled = pltpu.roll(jnp.broadcast_to(vec_ref[j], (QA, 2 * QA)), 0, 1,
                            stride=1, stride_axis=0)
        dchunk = (lax.shift_right_logical(qry + back, chunk_shift)
                  - lax.shift_right_logical(key + j * QA, chunk_shift))
        valid = (dchunk >= 0) & (dchunk <= LEFT_CHUNKS)
        out_ref[j] = jnp.where(valid, rolled[:, QA:], NEG_INF)


def _attn_a_bias_table(rel_bias):
    n_heads = rel_bias.shape[0]
    back = (A_KBLOCKS - 1) * QA
    d = (back - QA * jnp.arange(A_KBLOCKS))[:, None] + jnp.arange(2 * QA)[None, :] - QA
    vec = jnp.take(rel_bias.astype(F32) * LOG2E, jnp.clip(d, -REL_CLIP, REL_CLIP) + REL_CLIP,
                   axis=1)
    vec = vec.reshape(n_heads, A_KBLOCKS, 1, 2 * QA)
    return pl.pallas_call(
        _bias_table_kernel,
        out_shape=jax.ShapeDtypeStruct((n_heads, A_KBLOCKS, QA, QA), F32),
        grid=(n_heads,),
        in_specs=[pl.BlockSpec((None, A_KBLOCKS, 1, 2 * QA), lambda h: (h, 0, 0, 0))],
        out_specs=pl.BlockSpec((None, A_KBLOCKS, QA, QA), lambda h: (h, 0, 0, 0)),
        compiler_params=_params("parallel"),
        name="bias_table",
    )(vec)


def _split3(x):
    hi = x.astype(BF16)
    r1 = x - hi.astype(F32)
    mid = r1.astype(BF16)
    lo = (r1 - mid.astype(F32)).astype(BF16)
    return hi, mid, lo


def _kvf_kernel(x_ref, g_ref, w_ref, bf_ref, tri_ref, sel_ref, one_ref,
                k_ref, vt_ref, qe_ref, ke_ref, carry_ref):
    @pl.when(pl.program_id(1) == 0)
    def _():
        carry_ref[...] = jnp.zeros_like(carry_ref)

    for sub in range(KVF_SUB):
        rows = slice(sub * TM, (sub + 1) * TM)
        hn = _rms_norm(x_ref[rows, :], g_ref[...]).astype(BF16)
        y = jnp.dot(hn, w_ref[...], preferred_element_type=F32)
        for p in range(N_PAIRS):
            k_cols = slice(p * PAIR_W, (p + 1) * PAIR_W)
            v_cols = slice(D_MODEL + p * PAIR_W, D_MODEL + (p + 1) * PAIR_W)
            k_ref[p, rows, :] = y[:, k_cols].astype(BF16)
            vt_ref[p, sub] = y[:, v_cols].T.astype(BF16)
        z = y[:, 2 * D_MODEL:] + bf_ref[...]
        log_f = jnp.minimum(z, 0.0) - jnp.log1p(jnp.exp(-jnp.abs(z)))
        cum3 = jnp.dot(tri_ref[...], jnp.concatenate(_split3(log_f), axis=1),
                       preferred_element_type=F32)
        cum = (cum3[:, :EXT_W] + cum3[:, EXT_W:2 * EXT_W] + cum3[:, 2 * EXT_W:]) + carry_ref[...]
        carry_ref[...] = cum[TM - 1:TM, :]
        pieces = jnp.concatenate(_split3(cum * LOG2E), axis=1)
        ext = jnp.dot(pieces, sel_ref[...], preferred_element_type=F32) + one_ref[...]
        qe_ref[:, rows] = ext[:, :EXT_W].T.astype(BF16)
        ke_ref[rows, :] = ext[:, EXT_W:].astype(BF16)


def _kvf(h, g, w, bf, batch, seq):
    tm = KVF_SUB * TM
    ns = seq // tm
    tri = (jnp.arange(TM)[:, None] >= jnp.arange(TM)[None, :]).astype(BF16)
    half = EXT_PER_HEAD // 2
    src = jnp.arange(3 * EXT_W)[:, None]
    dst = jnp.arange(EXT_W)[None, :]
    src_head, src_piece = src % EXT_W, src // EXT_W
    dst_head, dst_slot = dst // EXT_PER_HEAD, dst % EXT_PER_HEAD
    same_head = (src_head == dst_head) & (src_head < N_HEADS)
    sel_q = (same_head & (dst_slot == half + src_piece)).astype(BF16)
    sel_k = -(same_head & (dst_slot == src_piece)).astype(BF16)
    live = jnp.arange(EXT_W) < N_HEADS * EXT_PER_HEAD
    slot = jnp.arange(EXT_W) % EXT_PER_HEAD
    one_q = (live & (slot < half)).astype(F32).reshape(1, EXT_W)
    one_k = (live & (slot >= half)).astype(F32).reshape(1, EXT_W)
    sel = jnp.concatenate([sel_q, sel_k], axis=1)
    one = jnp.concatenate([one_q, one_k], axis=1)

    k_sds = jax.ShapeDtypeStruct((batch, N_PAIRS, seq, PAIR_W), BF16)
    k_spec = pl.BlockSpec((None, N_PAIRS, tm, PAIR_W), lambda b, i: (b, 0, i, 0))
    vt_sds = jax.ShapeDtypeStruct((batch, N_PAIRS, seq // TM, PAIR_W, TM), BF16)
    vt_spec = pl.BlockSpec((None, N_PAIRS, KVF_SUB, PAIR_W, TM), lambda b, i: (b, 0, i, 0, 0))
    ext_sds = jax.ShapeDtypeStruct((batch, seq, EXT_W), BF16)
    ext_spec = pl.BlockSpec((None, tm, EXT_W), lambda b, i: (b, i, 0))
    ext_t_sds = jax.ShapeDtypeStruct((batch, EXT_W, seq), BF16)
    ext_t_spec = pl.BlockSpec((None, EXT_W, tm), lambda b, i: (b, 0, i))
    return pl.pallas_call(
        _kvf_kernel,
        out_shape=[k_sds, vt_sds, ext_t_sds, ext_sds],
        grid=(batch, ns),
        in_specs=[
            pl.BlockSpec((tm, D_MODEL), lambda b, i: (b * ns + i, 0)),
            _resident((1, D_MODEL), lambda b, i: (0, 0)),
            _resident((D_MODEL, 2 * D_MODEL + EXT_W), lambda b, i: (0, 0)),
            _resident((1, EXT_W), lambda b, i: (0, 0)),
            _resident((TM, TM), lambda b, i: (0, 0)),
            _resident((3 * EXT_W, 2 * EXT_W), lambda b, i: (0, 0)),
            _resident((1, 2 * EXT_W), lambda b, i: (0, 0)),
        ],
        out_specs=[k_spec, vt_spec, ext_t_spec, ext_spec],
        scratch_shapes=[pltpu.VMEM((1, EXT_W), F32)],
        compiler_params=_params("parallel", "arbitrary"),
        name="kvf",
    )(h, g, w, bf, tri, sel, one)


def _fox_kernel(q_ref, k_ref, vt_ref, qe_ref, ke_ref, zero_ref, o_ref, qa_ref, *scratch):
    s_refs = scratch[:-3]
    m_ref, alpha_ref, acc_ref = scratch[-3:]
    dz = zero_ref[0]
    group = pl.program_id(1)
    i = pl.program_id(2)
    heads = [(pp, hh) for pp in range(FOX_PAIRS) for hh in range(2)]
    qe_t = qe_ref[...]
    ext_row = lax.broadcasted_iota(jnp.int32, qe_t.shape, 0)
    for slot, (pp, hh) in enumerate(heads):
        head = 2 * (group * FOX_PAIRS + pp) + hh
        own = (ext_row >= EXT_PER_HEAD * head) & (ext_row < EXT_PER_HEAD * (head + 1))
        qa_ref[slot] = jnp.concatenate(
            [_keep_head_rows(q_ref[pp], hh), jnp.where(own, qe_t, jnp.zeros_like(qe_t))], axis=0)

    half = QB // 2

    n_chunks = QB // LANES

    def key_block(blk, diagonal):
        off = pl.multiple_of(blk * QB, QB)
        k_ext = ke_ref[pl.ds(off, QB), :]
        for slot, (pp, hh) in enumerate(heads):
            ka = jnp.concatenate([k_ref[pp, pl.ds(off, QB), :], k_ext], axis=1)
            if diagonal:
                s_refs[slot][dz, :half, :half] = jnp.dot(
                    ka[:half], qa_ref[slot, :, :half], preferred_element_type=F32)
                s_refs[slot][dz, :, half:] = jnp.dot(
                    ka, qa_ref[slot, :, half:], preferred_element_type=F32)
            else:
                s_refs[slot][dz] = jnp.dot(ka, qa_ref[slot], preferred_element_type=F32)

        def live_keys(c):
            return (c + 1) * LANES if diagonal else QB

        for slot in range(len(heads)):
            for c in range(n_chunks):
                cols = slice(c * LANES, (c + 1) * LANES)
                parts = []
                n_full = c * LANES if diagonal else QB
                if n_full:
                    parts.append(s_refs[slot][dz, :n_full, cols].max(axis=0, keepdims=True))
                if diagonal:
                    tile = slice(c * LANES, (c + 1) * LANES)
                    key_id = lax.broadcasted_iota(jnp.int32, (LANES, LANES), 0)
                    qry_id = lax.broadcasted_iota(jnp.int32, (LANES, LANES), 1)
                    x = jnp.where(key_id <= qry_id, s_refs[slot][dz, tile, cols], NEG_INF)
                    s_refs[slot][dz, tile, cols] = x
                    parts.append(x.max(axis=0, keepdims=True))
                blk_max = functools.reduce(jnp.maximum, parts)
                if diagonal:
                    m_ref[slot, :, cols] = blk_max
                else:
                    m_old = m_ref[slot, :, cols]
                    m_new = jnp.maximum(m_old, blk_max)
                    m_ref[slot, :, cols] = m_new
                    alpha_ref[slot, :, cols] = jnp.exp2(m_old - m_new)
        chunks_per_tile = MXU_TILE // LANES
        ones = jnp.ones((ONES_ROWS, QB), BF16)
        for slot, (pp, hh) in enumerate(heads):
            vt = vt_ref[pp, blk, hh * HEAD_DIM:(hh + 1) * HEAD_DIM, :]
            vt = jnp.concatenate([vt, ones], axis=0)
            for nt in range(QB // MXU_TILE):
                tile_cols = slice(nt * MXU_TILE, (nt + 1) * MXU_TILE)
                n_key_tiles = (nt + 1) if diagonal else QB // MXU_TILE
                pv = None
                for kt in range(n_key_tiles):
                    columns = []
                    for c in range(nt * chunks_per_tile, (nt + 1) * chunks_per_tile):
                        cols = slice(c * LANES, (c + 1) * LANES)
                        m_new = m_ref[slot, :, cols]
                        pieces = []
                        for r in range(kt * MXU_TILE, (kt + 1) * MXU_TILE, KSTRIP):
                            if r < live_keys(c):
                                pr = jnp.exp2(s_refs[slot][dz, r:r + KSTRIP, cols] - m_new)
                                pieces.append(pr.astype(BF16))
                            else:
                                pieces.append(jnp.zeros((KSTRIP, LANES), BF16))
                        columns.append(jnp.concatenate(pieces, axis=0))
                    p_tile = jnp.concatenate(columns, axis=1)
                    part = jnp.dot(vt[:, kt * MXU_TILE:(kt + 1) * MXU_TILE], p_tile,
                                   preferred_element_type=F32)
                    pv = part if pv is None else pv + part
                if diagonal:
                    acc_ref[slot, :, tile_cols] = pv
                else:
                    acc_ref[slot, :, tile_cols] = (
                        alpha_ref[slot, :, tile_cols] * acc_ref[slot, :, tile_cols] + pv)

    def body(j, carry):
        key_block(j, False)
        return carry

    key_block(i, True)
    lax.fori_loop(0, i, body, 0)
    for pp in range(FOX_PAIRS):
        o_t = jnp.concatenate(
            [acc_ref[2 * pp + hh, :HEAD_DIM, :] / acc_ref[2 * pp + hh, HEAD_DIM:HEAD_DIM + 1, :]
             for hh in range(2)], axis=0)
        o_ref[pp] = o_t.T.astype(BF16)


def _fox(q, k, vt, q_ext, k_ext):
    batch, _, seq, _ = k.shape
    nq = seq // QB
    n_heads = 2 * FOX_PAIRS
    assert vt.shape == (batch, N_PAIRS, nq, PAIR_W, QB)
    return pl.pallas_call(
        _fox_kernel,
        out_shape=jax.ShapeDtypeStruct(k.shape, BF16),
        grid=(batch, N_PAIRS // FOX_PAIRS, nq),
        in_specs=[
            pl.BlockSpec((None, FOX_PAIRS, PAIR_W, QB), lambda b, p, i: (b, p, 0, i)),
            _resident((None, FOX_PAIRS, seq, PAIR_W), lambda b, p, i: (b, p, 0, 0)),
            _resident((None, FOX_PAIRS, nq, PAIR_W, QB), lambda b, p, i: (b, p, 0, 0, 0)),
            pl.BlockSpec((None, EXT_W, QB), lambda b, p, i: (b, 0, i)),
            _resident((None, seq, EXT_W), lambda b, p, i: (b, 0, 0)),
            pl.BlockSpec(memory_space=pltpu.SMEM),
        ],
        out_specs=pl.BlockSpec((None, FOX_PAIRS, QB, PAIR_W), lambda b, p, i: (b, p, i, 0)),
        scratch_shapes=(
            [pltpu.VMEM((n_heads, PAIR_W + EXT_W, QB), BF16)]
            + [pltpu.VMEM((1, QB, QB), F32)] * n_heads
            + [pltpu.VMEM((n_heads, 1, QB), F32),
               pltpu.VMEM((n_heads, 1, QB), F32),
               pltpu.VMEM((n_heads, HEAD_DIM + ONES_ROWS, QB), F32)]
        ),
        compiler_params=_params("parallel", "parallel", "arbitrary"),
        name="fox",
    )(q, k, vt, q_ext, k_ext, jnp.zeros((1,), jnp.int32))


def kernel(x, ffn_norm, ffn_w_gate, ffn_w_up, ffn_w_down, mix_norm, a_w_qkv, a_w_o, a_rel_bias,
           kv_norm, b_w_kvf, b_f_bias, b_w_q, b_w_o, final_norm):
    batch, seq, _ = x.shape
    depth = ffn_norm.shape[0]
    n_a = a_w_qkv.shape[0]
    n_b = b_w_q.shape[0]
    assert seq % (2 * TM) == 0 and seq % (A_QSUB * QA) == 0 and seq % QB == 0
    assert n_a + n_b == depth

    gf = final_norm.reshape(1, D_MODEL)
    stacks = (ffn_w_gate, ffn_w_up, ffn_w_down)
    ffn_weights = [ffn_w_gate[0, 0].astype(BF16), ffn_w_up[0, 0].astype(BF16),
                   ffn_w_down[0, 0].astype(BF16)]

    def half_ffn(h, layer, pos, **fused):
        nxt = 2 * layer + pos + 1
        cast_next = (stacks, nxt // 2, nxt % 2) if nxt < 2 * depth else None
        outs = _ffn(h, ffn_norm[layer, pos].reshape(1, D_MODEL), tuple(ffn_weights), gf,
                    batch, seq, cast_next=cast_next, **fused)
        if cast_next is not None:
            ffn_weights[:] = outs[-3:]
            outs = outs[:-3]
        return outs

    h = x.reshape(batch * seq, D_MODEL)
    for layer in range(n_a):
        g_mix = mix_norm[layer].reshape(1, D_MODEL)
        h, qt, k, vt = half_ffn(
            h, layer, 0, proj=(g_mix, a_w_qkv[layer].astype(BF16), (True, False, True)))
        o = _attn_a(qt, k, vt, _attn_a_bias_table(a_rel_bias[layer]))
        (h,) = half_ffn(h, layer, 1, mixer=(o, a_w_o[layer].astype(BF16)))

    w_kvf = jnp.pad(b_w_kvf, ((0, 0), (0, EXT_W - N_HEADS))).astype(BF16)
    bf = jnp.pad(b_f_bias, (0, EXT_W - N_HEADS)).reshape(1, EXT_W).astype(F32)
    k_sh, vt_sh, q_ext_t, k_ext = _kvf(h, kv_norm.reshape(1, D_MODEL), w_kvf, bf, batch, seq)

    for lb in range(n_b):
        layer = n_a + lb
        g_mix = mix_norm[layer].reshape(1, D_MODEL)
        h, qt = half_ffn(h, layer, 0, proj=(g_mix, b_w_q[lb].astype(BF16), (True,)))
        o = _fox(qt, k_sh, vt_sh, q_ext_t, k_ext)
        (h,) = half_ffn(h, layer, 1, mixer=(o, b_w_o[lb].astype(BF16)),
                        final=(layer == depth - 1))

    return h.reshape(batch, seq, D_MODEL)
```

```python
import functools

import jax
import jax.numpy as jnp
from jax import lax
from jax.experimental import pallas as pl
from jax.experimental.pallas import tpu as pltpu

D_MODEL = 1024
N_HEADS = 16
HEAD_DIM = 64
N_PAIRS = N_HEADS // 2
PAIR_W = 2 * HEAD_DIM
D_FF = 2816
CHUNK = 64
LEFT_CHUNKS = 8
REL_CLIP = 256
EPS = 1e-6
NEG_INF = -1e30
ATTN_SCALE = HEAD_DIM ** -0.5
LOG2E = 1.4426950408889634
Q_SCALE = ATTN_SCALE * LOG2E
FFN_RES_WEIGHT = 0.5

TM = 512
FF_CHUNK = 256
CAST_BLOCKS = 16
KVF_SUB = 2
QA = 256
A_KBLOCKS = LEFT_CHUNKS * CHUNK // QA + 1
A_PAIRS = N_PAIRS
A_QSUB = 2
QB = 512
FOX_PAIRS = N_PAIRS // 2
EXT_W = 128
EXT_PER_HEAD = 6
LANES = 128
MXU_TILE = 256
KSTRIP = LANES
ONES_ROWS = 16
VMEM_LIMIT = 56 * 1024 * 1024

F32 = jnp.float32
BF16 = jnp.bfloat16


def _rms_norm(x, g):
    y = x * lax.rsqrt(jnp.mean(x * x, axis=-1, keepdims=True) + EPS)
    return y * g


def _resident(block_shape, index_map):
    return pl.BlockSpec(block_shape, index_map, pipeline_mode=pl.Buffered(1))


def _params(*semantics):
    return pltpu.CompilerParams(dimension_semantics=semantics, vmem_limit_bytes=VMEM_LIMIT)


def _ffn_kernel(*refs, has_mixer, transposed, final, n_cast, cast_blocks):
    n_proj = len(transposed)
    x_ref, g_ref, wg_ref, wu_ref, wd_ref, gf_ref = refs[:6]
    at = 6
    if has_mixer:
        o_ref, wo_ref = refs[at:at + 2]
        at += 2
    if n_proj:
        gm_ref, wp_ref = refs[at:at + 2]
        at += 2
    cast_in = refs[at:at + n_cast]
    at += n_cast
    out_ref = refs[at]
    proj_refs = refs[at + 1:at + 1 + n_proj]
    cast_out = refs[at + 1 + n_proj:at + 1 + n_proj + n_cast]
    a_ref = refs[-1]

    if n_cast:
        @pl.when(pl.program_id(0) * pl.num_programs(1) + pl.program_id(1) < cast_blocks)
        def _():
            for src, dst in zip(cast_in, cast_out):
                dst[...] = src[...].astype(BF16)

    for sub in range(a_ref.shape[0]):
        rows = slice(sub * TM, (sub + 1) * TM)
        x = x_ref[rows, :]
        if has_mixer:
            o = jnp.concatenate([o_ref[p, rows, :] for p in range(N_PAIRS)], axis=1)
            x = x + jnp.dot(o, wo_ref[...], preferred_element_type=F32)
        hn = _rms_norm(x, g_ref[...]).astype(BF16)
        for c in range(D_FF // FF_CHUNK):
            sl = slice(c * FF_CHUNK, (c + 1) * FF_CHUNK)
            gate = jnp.dot(hn, wg_ref[:, sl], preferred_element_type=F32)
            up = jnp.dot(hn, wu_ref[:, sl], preferred_element_type=F32)
            a_ref[sub, :, sl] = (gate * jax.nn.sigmoid(gate) * up).astype(BF16)
        y = jnp.dot(a_ref[sub], wd_ref[...], preferred_element_type=F32)
        out = x + FFN_RES_WEIGHT * y
        if final:
            out = _rms_norm(out, gf_ref[...])
        out_ref[rows, :] = out
        if n_proj:
            hp = _rms_norm(out, gm_ref[...]).astype(BF16)
            for n, p_ref in enumerate(proj_refs):
                y = jnp.dot(hp, wp_ref[:, n * D_MODEL:(n + 1) * D_MODEL],
                            preferred_element_type=F32)
                if n == 0:
                    y = y * Q_SCALE
                for p in range(N_PAIRS):
                    y_pair = y[:, p * PAIR_W:(p + 1) * PAIR_W]
                    if transposed[n]:
                        p_ref[p, :, rows] = y_pair.T.astype(BF16)
                    else:
                        p_ref[p, rows, :] = y_pair.astype(BF16)


def _ffn_sub_tiles(n_side):
    def estimate(n_sub):
        rows = n_sub * TM
        stream = 2 * 2 * rows * D_MODEL * 4
        stream += 2 * n_side * rows * D_MODEL * 2
        resident = (3 * D_FF + n_side * D_MODEL) * D_MODEL * 2
        scratch = rows * D_FF * 2
        live = 4 * TM * D_MODEL * 4
        return stream + resident + scratch + live

    return 2 if estimate(2) <= VMEM_LIMIT else 1


def _ffn(h, g, weights, gf, batch, seq, mixer=None, proj=None, final=False, cast_next=None):
    n_side = (0 if proj is None else len(proj[2])) + (mixer is not None)
    n_sub = _ffn_sub_tiles(n_side)
    tm = n_sub * TM
    ns = seq // tm

    def rows(b, i):
        return (b * ns + i, 0)

    def fixed(b, i):
        return (0, 0)

    pair_spec = pl.BlockSpec((None, N_PAIRS, tm, PAIR_W), lambda b, i: (b, 0, i, 0))
    inputs = [h, g, *weights, gf]
    in_specs = [
        pl.BlockSpec((tm, D_MODEL), rows),
        _resident((1, D_MODEL), fixed),
        _resident((D_MODEL, D_FF), fixed),
        _resident((D_MODEL, D_FF), fixed),
        _resident((D_FF, D_MODEL), fixed),
        _resident((1, D_MODEL), fixed),
    ]
    if mixer is not None:
        inputs += list(mixer)
        in_specs += [pair_spec, _resident((D_MODEL, D_MODEL), fixed)]
    out_shape = [jax.ShapeDtypeStruct(h.shape, F32)]
    out_specs = [pl.BlockSpec((tm, D_MODEL), rows)]
    transposed = ()
    if proj is not None:
        g_mix, w_proj, transposed = proj
        assert w_proj.shape[1] == len(transposed) * D_MODEL
        inputs += [g_mix, w_proj]
        in_specs += [_resident((1, D_MODEL), fixed), _resident(w_proj.shape, fixed)]
        for flag in transposed:
            if flag:
                out_shape.append(jax.ShapeDtypeStruct((batch, N_PAIRS, PAIR_W, seq), BF16))
                out_specs.append(
                    pl.BlockSpec((None, N_PAIRS, PAIR_W, tm), lambda b, i: (b, 0, 0, i)))
            else:
                out_shape.append(jax.ShapeDtypeStruct((batch, N_PAIRS, seq, PAIR_W), BF16))
                out_specs.append(pair_spec)
    n_cast = 0
    cast_blocks = min(CAST_BLOCKS, batch * ns)
    if cast_next is not None:
        stacks, layer, pos = cast_next
        n_cast = len(stacks)
        inputs += list(stacks)
        for stack in stacks:
            n_rows, n_cols = stack.shape[2:]
            cast_rows = n_rows // cast_blocks
            assert cast_rows * cast_blocks == n_rows and cast_rows % 16 == 0
            in_specs.append(pl.BlockSpec(
                (None, None, cast_rows, n_cols),
                lambda b, i: (layer, pos, jnp.minimum(b * ns + i, cast_blocks - 1), 0)))
            out_shape.append(jax.ShapeDtypeStruct((n_rows, n_cols), BF16))
            out_specs.append(pl.BlockSpec(
                (cast_rows, n_cols), lambda b, i: (jnp.minimum(b * ns + i, cast_blocks - 1), 0)))
    return pl.pallas_call(
        functools.partial(_ffn_kernel, has_mixer=mixer is not None, transposed=tuple(transposed),
                          final=final, n_cast=n_cast, cast_blocks=cast_blocks),
        out_shape=out_shape,
        grid=(batch, ns),
        in_specs=in_specs,
        out_specs=out_specs,
        scratch_shapes=[pltpu.VMEM((n_sub, TM, D_FF), BF16)],
        compiler_params=_params("arbitrary", "arbitrary"),
        name="ffn",
    )(*inputs)


def _keep_head_rows(x_t, head_in_pair):
    row = lax.broadcasted_iota(jnp.int32, x_t.shape, 0)
    own = (row >= HEAD_DIM * head_in_pair) & (row < HEAD_DIM * (head_in_pair + 1))
    return jnp.where(own, x_t, jnp.zeros_like(x_t))


def _attn_a_kernel(q_ref, *refs):
    n_kv = A_KBLOCKS + A_QSUB - 1
    k_refs = refs[:n_kv]
    vt_refs = refs[n_kv:2 * n_kv]
    bias_ref, zero_ref, o_ref = refs[2 * n_kv:2 * n_kv + 3]
    s_refs = refs[2 * n_kv + 3:-1]
    m_ref = refs[-1]
    dz = zero_ref[0]
    i = pl.program_id(2)
    heads = [(pp, hh) for pp in range(A_PAIRS) for hh in range(2)]
    chunks = [slice(c * LANES, (c + 1) * LANES) for c in range(QA // LANES)]
    ones = jnp.ones((ONES_ROWS, QA), BF16)
    for sub in range(A_QSUB):
        q_cols = slice(sub * QA, (sub + 1) * QA)
        sub_refs = s_refs[sub * len(heads):(sub + 1) * len(heads)]
        pens = [jnp.where(i * A_QSUB + sub - (A_KBLOCKS - 1) + j >= 0, 0.0, NEG_INF).astype(F32)
                for j in range(A_KBLOCKS)]
        for (pp, hh), s_ref in zip(heads, sub_refs):
            qm_t = _keep_head_rows(q_ref[pp, :, q_cols], hh)
            for j in range(A_KBLOCKS):
                s_ref[j + dz] = jnp.dot(k_refs[sub + j][pp], qm_t, preferred_element_type=F32)
        for (pp, hh), s_ref in zip(heads, sub_refs):
            for cols in chunks:
                blk_max = []
                for j in range(A_KBLOCKS):
                    x = s_ref[j + dz, :, cols] + bias_ref[2 * pp + hh, j, :, cols]
                    s_ref[j + dz, :, cols] = x
                    blk_max.append(x.max(axis=0, keepdims=True) + pens[j])
                m_ref[sub, pp, hh, :, cols] = functools.reduce(jnp.maximum, blk_max)
        for pp in range(A_PAIRS):
            outs = []
            for hh in range(2):
                s_ref = sub_refs[2 * pp + hh]
                pv = None
                for j in range(A_KBLOCKS):
                    columns = []
                    for cols in chunks:
                        shift = m_ref[sub, pp, hh, :, cols] - pens[j]
                        pieces = [
                            jnp.exp2(s_ref[j + dz, r:r + KSTRIP, cols] - shift).astype(BF16)
                            for r in range(0, QA, KSTRIP)]
                        columns.append(jnp.concatenate(pieces, axis=0))
                    p_tile = jnp.concatenate(columns, axis=1)
                    vt = vt_refs[sub + j][pp, hh * HEAD_DIM:(hh + 1) * HEAD_DIM, :]
                    part = jnp.dot(jnp.concatenate([vt, ones], axis=0), p_tile,
                                   preferred_element_type=F32)
                    pv = part if pv is None else pv + part
                outs.append(pv[:HEAD_DIM] / pv[HEAD_DIM:HEAD_DIM + 1])
            o_ref[pp, q_cols, :] = jnp.concatenate(outs, axis=0).T.astype(BF16)


def _attn_a(q, k, vt, bias_t):
    batch, _, seq, _ = k.shape
    n_steps = seq // (A_QSUB * QA)
    n_kv = A_KBLOCKS + A_QSUB - 1

    def k_spec(t):
        back = A_KBLOCKS - 1 - t
        return pl.BlockSpec((None, A_PAIRS, QA, PAIR_W),
                            lambda b, p, i: (b, p, jnp.maximum(i * A_QSUB - back, 0), 0))

    def vt_spec(t):
        back = A_KBLOCKS - 1 - t
        return pl.BlockSpec((None, A_PAIRS, PAIR_W, QA),
                            lambda b, p, i: (b, p, 0, jnp.maximum(i * A_QSUB - back, 0)))

    return pl.pallas_call(
        _attn_a_kernel,
        out_shape=jax.ShapeDtypeStruct(k.shape, BF16),
        grid=(batch, N_PAIRS // A_PAIRS, n_steps),
        in_specs=(
            [pl.BlockSpec((None, A_PAIRS, PAIR_W, A_QSUB * QA), lambda b, p, i: (b, p, 0, i))]
            + [k_spec(t) for t in range(n_kv)]
            + [vt_spec(t) for t in range(n_kv)]
            + [_resident((N_HEADS, A_KBLOCKS, QA, QA), lambda b, p, i: (0, 0, 0, 0))]
            + [pl.BlockSpec(memory_space=pltpu.SMEM)]
        ),
        out_specs=pl.BlockSpec((None, A_PAIRS, A_QSUB * QA, PAIR_W),
                               lambda b, p, i: (b, p, i, 0)),
        scratch_shapes=(
            [pltpu.VMEM((A_KBLOCKS, QA, QA), F32)] * (A_QSUB * 2 * A_PAIRS)
            + [pltpu.VMEM((A_QSUB, A_PAIRS, 2, 1, QA), F32)]
        ),
        compiler_params=_params("parallel", "parallel", "arbitrary"),
        name="attn_a",
    )(q, *([k] * n_kv), *([vt] * n_kv), bias_t, jnp.zeros((1,), jnp.int32))


def _bias_table_kernel(vec_ref, out_ref):
    back = (A_KBLOCKS - 1) * QA
    key = lax.broadcasted_iota(jnp.int32, (QA, QA), 0)
    qry = lax.broadcasted_iota(jnp.int32, (QA, QA), 1)
    chunk_shift = CHUNK.bit_length() - 1
    for j in range(A_KBLOCKS):
        rolled = pltpu.roll(jnp.broadcast_to(vec_ref[j], (QA, 2 * QA)), 0, 1,
                            stride=1, stride_axis=0)
        dchunk = (lax.shift_right_logical(qry + back, chunk_shift)
                  - lax.shift_right_logical(key + j * QA, chunk_shift))
        valid = (dchunk >= 0) & (dchunk <= LEFT_CHUNKS)
        out_ref[j] = jnp.where(valid, rolled[:, QA:], NEG_INF)


def _attn_a_bias_table(rel_bias):
    n_heads = rel_bias.shape[0]
    back = (A_KBLOCKS - 1) * QA
    d = (back - QA * jnp.arange(A_KBLOCKS))[:, None] + jnp.arange(2 * QA)[None, :] - QA
    vec = jnp.take(rel_bias.astype(F32) * LOG2E, jnp.clip(d, -REL_CLIP, REL_CLIP) + REL_CLIP,
                   axis=1)
    vec = vec.reshape(n_heads, A_KBLOCKS, 1, 2 * QA)
    return pl.pallas_call(
        _bias_table_kernel,
        out_shape=jax.ShapeDtypeStruct((n_heads, A_KBLOCKS, QA, QA), F32),
        grid=(n_heads,),
        in_specs=[pl.BlockSpec((None, A_KBLOCKS, 1, 2 * QA), lambda h: (h, 0, 0, 0))],
        out_specs=pl.BlockSpec((None, A_KBLOCKS, QA, QA), lambda h: (h, 0, 0, 0)),
        compiler_params=_params("parallel"),
        name="bias_table",
    )(vec)


def _split3(x):
    hi = x.astype(BF16)
    r1 = x - hi.astype(F32)
    mid = r1.astype(BF16)
    lo = (r1 - mid.astype(F32)).astype(BF16)
    return hi, mid, lo


def _kvf_kernel(x_ref, g_ref, w_ref, bf_ref, tri_ref, sel_ref, one_ref,
                k_ref, vt_ref, qe_ref, ke_ref, carry_ref):
    @pl.when(pl.program_id(1) == 0)
    def _():
        carry_ref[...] = jnp.zeros_like(carry_ref)

    for sub in range(KVF_SUB):
        rows = slice(sub * TM, (sub + 1) * TM)
        hn = _rms_norm(x_ref[rows, :], g_ref[...]).astype(BF16)
        y = jnp.dot(hn, w_ref[...], preferred_element_type=F32)
        for p in range(N_PAIRS):
            k_cols = slice(p * PAIR_W, (p + 1) * PAIR_W)
            v_cols = slice(D_MODEL + p * PAIR_W, D_MODEL + (p + 1) * PAIR_W)
            k_ref[p, rows, :] = y[:, k_cols].astype(BF16)
            vt_ref[p, sub] = y[:, v_cols].T.astype(BF16)
        z = y[:, 2 * D_MODEL:] + bf_ref[...]
        log_f = jnp.minimum(z, 0.0) - jnp.log1p(jnp.exp(-jnp.abs(z)))
        cum3 = jnp.dot(tri_ref[...], jnp.concatenate(_split3(log_f), axis=1),
                       preferred_element_type=F32)
        cum = (cum3[:, :EXT_W] + cum3[:, EXT_W:2 * EXT_W] + cum3[:, 2 * EXT_W:]) + carry_ref[...]
        carry_ref[...] = cum[TM - 1:TM, :]
        pieces = jnp.concatenate(_split3(cum * LOG2E), axis=1)
        ext = jnp.dot(pieces, sel_ref[...], preferred_element_type=F32) + one_ref[...]
        qe_ref[:, rows] = ext[:, :EXT_W].T.astype(BF16)
        ke_ref[rows, :] = ext[:, EXT_W:].astype(BF16)


def _kvf(h, g, w, bf, batch, seq):
    tm = KVF_SUB * TM
    ns = seq // tm
    tri = (jnp.arange(TM)[:, None] >= jnp.arange(TM)[None, :]).astype(BF16)
    half = EXT_PER_HEAD // 2
    src = jnp.arange(3 * EXT_W)[:, None]
    dst = jnp.arange(EXT_W)[None, :]
    src_head, src_piece = src % EXT_W, src // EXT_W
    dst_head, dst_slot = dst // EXT_PER_HEAD, dst % EXT_PER_HEAD
    same_head = (src_head == dst_head) & (src_head < N_HEADS)
    sel_q = (same_head & (dst_slot == half + src_piece)).astype(BF16)
    sel_k = -(same_head & (dst_slot == src_piece)).astype(BF16)
    live = jnp.arange(EXT_W) < N_HEADS * EXT_PER_HEAD
    slot = jnp.arange(EXT_W) % EXT_PER_HEAD
    one_q = (live & (slot < half)).astype(F32).reshape(1, EXT_W)
    one_k = (live & (slot >= half)).astype(F32).reshape(1, EXT_W)
    sel = jnp.concatenate([sel_q, sel_k], axis=1)
    one = jnp.concatenate([one_q, one_k], axis=1)

    k_sds = jax.ShapeDtypeStruct((batch, N_PAIRS, seq, PAIR_W), BF16)
    k_spec = pl.BlockSpec((None, N_PAIRS, tm, PAIR_W), lambda b, i: (b, 0, i, 0))
    vt_sds = jax.ShapeDtypeStruct((batch, N_PAIRS, seq // TM, PAIR_W, TM), BF16)
    vt_spec = pl.BlockSpec((None, N_PAIRS, KVF_SUB, PAIR_W, TM), lambda b, i: (b, 0, i, 0, 0))
    ext_sds = jax.ShapeDtypeStruct((batch, seq, EXT_W), BF16)
    ext_spec = pl.BlockSpec((None, tm, EXT_W), lambda b, i: (b, i, 0))
    ext_t_sds = jax.ShapeDtypeStruct((batch, EXT_W, seq), BF16)
    ext_t_spec = pl.BlockSpec((None, EXT_W, tm), lambda b, i: (b, 0, i))
    return pl.pallas_call(
        _kvf_kernel,
        out_shape=[k_sds, vt_sds, ext_t_sds, ext_sds],
        grid=(batch, ns),
        in_specs=[
            pl.BlockSpec((tm, D_MODEL), lambda b, i: (b * ns + i, 0)),
            _resident((1, D_MODEL), lambda b, i: (0, 0)),
            _resident((D_MODEL, 2 * D_MODEL + EXT_W), lambda b, i: (0, 0)),
            _resident((1, EXT_W), lambda b, i: (0, 0)),
            _resident((TM, TM), lambda b, i: (0, 0)),
            _resident((3 * EXT_W, 2 * EXT_W), lambda b, i: (0, 0)),
            _resident((1, 2 * EXT_W), lambda b, i: (0, 0)),
        ],
        out_specs=[k_spec, vt_spec, ext_t_spec, ext_spec],
        scratch_shapes=[pltpu.VMEM((1, EXT_W), F32)],
        compiler_params=_params("parallel", "arbitrary"),
        name="kvf",
    )(h, g, w, bf, tri, sel, one)


def _fox_kernel(q_ref, k_ref, vt_ref, qe_ref, ke_ref, zero_ref, o_ref, qa_ref, *scratch):
    s_refs = scratch[:-3]
    m_ref, alpha_ref, acc_ref = scratch[-3:]
    dz = zero_ref[0]
    group = pl.program_id(1)
    i = pl.program_id(2)
    heads = [(pp, hh) for pp in range(FOX_PAIRS) for hh in range(2)]
    qe_t = qe_ref[...]
    ext_row = lax.broadcasted_iota(jnp.int32, qe_t.shape, 0)
    for slot, (pp, hh) in enumerate(heads):
        head = 2 * (group * FOX_PAIRS + pp) + hh
        own = (ext_row >= EXT_PER_HEAD * head) & (ext_row < EXT_PER_HEAD * (head + 1))
        qa_ref[slot] = jnp.concatenate(
            [_keep_head_rows(q_ref[pp], hh), jnp.where(own, qe_t, jnp.zeros_like(qe_t))], axis=0)

    half = QB // 2

    n_chunks = QB // LANES

    def key_block(blk, diagonal):
        off = pl.multiple_of(blk * QB, QB)
        k_ext = ke_ref[pl.ds(off, QB), :]
        for slot, (pp, hh) in enumerate(heads):
            ka = jnp.concatenate([k_ref[pp, pl.ds(off, QB), :], k_ext], axis=1)
            if diagonal:
                s_refs[slot][dz, :half, :half] = jnp.dot(
                    ka[:half], qa_ref[slot, :, :half], preferred_element_type=F32)
                s_refs[slot][dz, :, half:] = jnp.dot(
                    ka, qa_ref[slot, :, half:], preferred_element_type=F32)
            else:
                s_refs[slot][dz] = jnp.dot(ka, qa_ref[slot], preferred_element_type=F32)

        def live_keys(c):
            return (c + 1) * LANES if diagonal else QB

        for slot in range(len(heads)):
            for c in range(n_chunks):
                cols = slice(c * LANES, (c + 1) * LANES)
                parts = []
                n_full = c * LANES if diagonal else QB
                if n_full:
                    parts.append(s_refs[slot][dz, :n_full, cols].max(axis=0, keepdims=True))
                if diagonal:
                    tile = slice(c * LANES, (c + 1) * LANES)
                    key_id = lax.broadcasted_iota(jnp.int32, (LANES, LANES), 0)
                    qry_id = lax.broadcasted_iota(jnp.int32, (LANES, LANES), 1)
                    x = jnp.where(key_id <= qry_id, s_refs[slot][dz, tile, cols], NEG_INF)
                    s_refs[slot][dz, tile, cols] = x
                    parts.append(x.max(axis=0, keepdims=True))
                blk_max = functools.reduce(jnp.maximum, parts)
                if diagonal:
                    m_ref[slot, :, cols] = blk_max
                else:
                    m_old = m_ref[slot, :, cols]
                    m_new = jnp.maximum(m_old, blk_max)
                    m_ref[slot, :, cols] = m_new
                    alpha_ref[slot, :, cols] = jnp.exp2(m_old - m_new)
        chunks_per_tile = MXU_TILE // LANES
        ones = jnp.ones((ONES_ROWS, QB), BF16)
        for slot, (pp, hh) in enumerate(heads):
            vt = vt_ref[pp, blk, hh * HEAD_DIM:(hh + 1) * HEAD_DIM, :]
            vt = jnp.concatenate([vt, ones], axis=0)
            for nt in range(QB // MXU_TILE):
                tile_cols = slice(nt * MXU_TILE, (nt + 1) * MXU_TILE)
                n_key_tiles = (nt + 1) if diagonal else QB // MXU_TILE
                pv = None
                for kt in range(n_key_tiles):
                    columns = []
                    for c in range(nt * chunks_per_tile, (nt + 1) * chunks_per_tile):
                        cols = slice(c * LANES, (c + 1) * LANES)
                        m_new = m_ref[slot, :, cols]
                        pieces = []
                        for r in range(kt * MXU_TILE, (kt + 1) * MXU_TILE, KSTRIP):
                            if r < live_keys(c):
                                pr = jnp.exp2(s_refs[slot][dz, r:r + KSTRIP, cols] - m_new)
                                pieces.append(pr.astype(BF16))
                            else:
                                pieces.append(jnp.zeros((KSTRIP, LANES), BF16))
                        columns.append(jnp.concatenate(pieces, axis=0))
                    p_tile = jnp.concatenate(columns, axis=1)
                    part = jnp.dot(vt[:, kt * MXU_TILE:(kt + 1) * MXU_TILE], p_tile,
                                   preferred_element_type=F32)
                    pv = part if pv is None else pv + part
                if diagonal:
                    acc_ref[slot, :, tile_cols] = pv
                else:
                    acc_ref[slot, :, tile_cols] = (
                        alpha_ref[slot, :, tile_cols] * acc_ref[slot, :, tile_cols] + pv)

    def body(j, carry):
        key_block(j, False)
        return carry

    key_block(i, True)
    lax.fori_loop(0, i, body, 0)
    for pp in range(FOX_PAIRS):
        o_t = jnp.concatenate(
            [acc_ref[2 * pp + hh, :HEAD_DIM, :] / acc_ref[2 * pp + hh, HEAD_DIM:HEAD_DIM + 1, :]
             for hh in range(2)], axis=0)
        o_ref[pp] = o_t.T.astype(BF16)


def _fox(q, k, vt, q_ext, k_ext):
    batch, _, seq, _ = k.shape
    nq = seq // QB
    n_heads = 2 * FOX_PAIRS
    assert vt.shape == (batch, N_PAIRS, nq, PAIR_W, QB)
    return pl.pallas_call(
        _fox_kernel,
        out_shape=jax.ShapeDtypeStruct(k.shape, BF16),
        grid=(batch, N_PAIRS // FOX_PAIRS, nq),
        in_specs=[
            pl.BlockSpec((None, FOX_PAIRS, PAIR_W, QB), lambda b, p, i: (b, p, 0, i)),
            pl.BlockSpec((None, FOX_PAIRS, seq, PAIR_W), lambda b, p, i: (b, p, 0, 0)),
            pl.BlockSpec((None, FOX_PAIRS, nq, PAIR_W, QB), lambda b, p, i: (b, p, 0, 0, 0)),
            pl.BlockSpec((None, EXT_W, QB), lambda b, p, i: (b, 0, i)),
            pl.BlockSpec((None, seq, EXT_W), lambda b, p, i: (b, 0, 0)),
            pl.BlockSpec(memory_space=pltpu.SMEM),
        ],
        out_specs=pl.BlockSpec((None, FOX_PAIRS, QB, PAIR_W), lambda b, p, i: (b, p, i, 0)),
        scratch_shapes=(
            [pltpu.VMEM((n_heads, PAIR_W + EXT_W, QB), BF16)]
            + [pltpu.VMEM((1, QB, QB), F32)] * n_heads
            + [pltpu.VMEM((n_heads, 1, QB), F32),
               pltpu.VMEM((n_heads, 1, QB), F32),
               pltpu.VMEM((n_heads, HEAD_DIM + ONES_ROWS, QB), F32)]
        ),
        compiler_params=_params("parallel", "parallel", "arbitrary"),
        name="fox",
    )(q, k, vt, q_ext, k_ext, jnp.zeros((1,), jnp.int32))


def kernel(x, ffn_norm, ffn_w_gate, ffn_w_up, ffn_w_down, mix_norm, a_w_qkv, a_w_o, a_rel_bias,
           kv_norm, b_w_kvf, b_f_bias, b_w_q, b_w_o, final_norm):
    batch, seq, _ = x.shape
    depth = ffn_norm.shape[0]
    n_a = a_w_qkv.shape[0]
    n_b = b_w_q.shape[0]
    assert seq % (2 * TM) == 0 and seq % (A_QSUB * QA) == 0 and seq % QB == 0
    assert n_a + n_b == depth

    gf = final_norm.reshape(1, D_MODEL)
    stacks = (ffn_w_gate, ffn_w_up, ffn_w_down)
    ffn_weights = [ffn_w_gate[0, 0].astype(BF16), ffn_w_up[0, 0].astype(BF16),
                   ffn_w_down[0, 0].astype(BF16)]

    def half_ffn(h, layer, pos, **fused):
        nxt = 2 * layer + pos + 1
        cast_next = (stacks, nxt // 2, nxt % 2) if nxt < 2 * depth else None
        outs = _ffn(h, ffn_norm[layer, pos].reshape(1, D_MODEL), tuple(ffn_weights), gf,
                    batch, seq, cast_next=cast_next, **fused)
        if cast_next is not None:
            ffn_weights[:] = outs[-3:]
            outs = outs[:-3]
        return outs

    h = x.reshape(batch * seq, D_MODEL)
    for layer in range(n_a):
        g_mix = mix_norm[layer].reshape(1, D_MODEL)
        h, qt, k, vt = half_ffn(
            h, layer, 0, proj=(g_mix, a_w_qkv[layer].astype(BF16), (True, False, True)))
        o = _attn_a(qt, k, vt, _attn_a_bias_table(a_rel_bias[layer]))
        (h,) = half_ffn(h, layer, 1, mixer=(o, a_w_o[layer].astype(BF16)))

    w_kvf = jnp.pad(b_w_kvf, ((0, 0), (0, EXT_W - N_HEADS))).astype(BF16)
    bf = jnp.pad(b_f_bias, (0, EXT_W - N_HEADS)).reshape(1, EXT_W).astype(F32)
    k_sh, vt_sh, q_ext_t, k_ext = _kvf(h, kv_norm.reshape(1, D_MODEL), w_kvf, bf, batch, seq)

    for lb in range(n_b):
        layer = n_a + lb
        g_mix = mix_norm[layer].reshape(1, D_MODEL)
        h, qt = half_ffn(h, layer, 0, proj=(g_mix, b_w_q[lb].astype(BF16), (True,)))
        o = _fox(qt, k_sh, vt_sh, q_ext_t, k_ext)
        (h,) = half_ffn(h, layer, 1, mixer=(o, b_w_o[lb].astype(BF16)),
                        final=(layer == depth - 1))

    return h.reshape(batch, seq, D_MODEL)
```

```python
import functools

import jax
import jax.numpy as jnp
from jax import lax
from jax.experimental import pallas as pl
from jax.experimental.pallas import tpu as pltpu

D_MODEL = 1024
N_HEADS = 16
HEAD_DIM = 64
N_PAIRS = N_HEADS // 2
PAIR_W = 2 * HEAD_DIM
D_FF = 2816
CHUNK = 64
LEFT_CHUNKS = 8
REL_CLIP = 256
EPS = 1e-6
NEG_INF = -1e30
ATTN_SCALE = HEAD_DIM ** -0.5
LOG2E = 1.4426950408889634
Q_SCALE = ATTN_SCALE * LOG2E
FFN_RES_WEIGHT = 0.5

TM = 512
FF_CHUNK = 256
CAST_BLOCKS = 16
KVF_SUB = 2
QA = 256
A_KBLOCKS = LEFT_CHUNKS * CHUNK // QA + 1
A_PAIRS = N_PAIRS
A_QSUB = 2
QB = 512
FOX_PAIRS = N_PAIRS
EXT_W = 128
EXT_PER_HEAD = 6
LANES = 128
MXU_TILE = 256
KSTRIP = LANES
ONES_ROWS = 16
VMEM_LIMIT = 56 * 1024 * 1024

F32 = jnp.float32
BF16 = jnp.bfloat16


def _rms_norm(x, g):
    y = x * lax.rsqrt(jnp.mean(x * x, axis=-1, keepdims=True) + EPS)
    return y * g


def _resident(block_shape, index_map):
    return pl.BlockSpec(block_shape, index_map, pipeline_mode=pl.Buffered(1))


def _params(*semantics):
    return pltpu.CompilerParams(dimension_semantics=semantics, vmem_limit_bytes=VMEM_LIMIT)


def _ffn_kernel(*refs, has_mixer, transposed, final, n_cast, cast_blocks):
    n_proj = len(transposed)
    x_ref, g_ref, wg_ref, wu_ref, wd_ref, gf_ref = refs[:6]
    at = 6
    if has_mixer:
        o_ref, wo_ref = refs[at:at + 2]
        at += 2
    if n_proj:
        gm_ref, wp_ref = refs[at:at + 2]
        at += 2
    cast_in = refs[at:at + n_cast]
    at += n_cast
    out_ref = refs[at]
    proj_refs = refs[at + 1:at + 1 + n_proj]
    cast_out = refs[at + 1 + n_proj:at + 1 + n_proj + n_cast]
    a_ref = refs[-1]

    if n_cast:
        @pl.when(pl.program_id(0) * pl.num_programs(1) + pl.program_id(1) < cast_blocks)
        def _():
            for src, dst in zip(cast_in, cast_out):
                dst[...] = src[...].astype(BF16)

    for sub in range(a_ref.shape[0]):
        rows = slice(sub * TM, (sub + 1) * TM)
        x = x_ref[rows, :]
        if has_mixer:
            o = jnp.concatenate([o_ref[p, rows, :] for p in range(N_PAIRS)], axis=1)
            x = x + jnp.dot(o, wo_ref[...], preferred_element_type=F32)
        hn = _rms_norm(x, g_ref[...]).astype(BF16)
        for c in range(D_FF // FF_CHUNK):
            sl = slice(c * FF_CHUNK, (c + 1) * FF_CHUNK)
            gate = jnp.dot(hn, wg_ref[:, sl], preferred_element_type=F32)
            up = jnp.dot(hn, wu_ref[:, sl], preferred_element_type=F32)
            a_ref[sub, :, sl] = (gate * jax.nn.sigmoid(gate) * up).astype(BF16)
        y = jnp.dot(a_ref[sub], wd_ref[...], preferred_element_type=F32)
        out = x + FFN_RES_WEIGHT * y
        if final:
            out = _rms_norm(out, gf_ref[...])
        out_ref[rows, :] = out
        if n_proj:
            hp = _rms_norm(out, gm_ref[...]).astype(BF16)
            for n, p_ref in enumerate(proj_refs):
                y = jnp.dot(hp, wp_ref[:, n * D_MODEL:(n + 1) * D_MODEL],
                            preferred_element_type=F32)
                if n == 0:
                    y = y * Q_SCALE
                for p in range(N_PAIRS):
                    y_pair = y[:, p * PAIR_W:(p + 1) * PAIR_W]
                    if transposed[n]:
                        p_ref[p, :, rows] = y_pair.T.astype(BF16)
                    else:
                        p_ref[p, rows, :] = y_pair.astype(BF16)


def _ffn_sub_tiles(n_side):
    def estimate(n_sub):
        rows = n_sub * TM
        stream = 2 * 2 * rows * D_MODEL * 4
        stream += 2 * n_side * rows * D_MODEL * 2
        resident = (3 * D_FF + n_side * D_MODEL) * D_MODEL * 2
        scratch = rows * D_FF * 2
        live = 4 * TM * D_MODEL * 4
        return stream + resident + scratch + live

    return 2 if estimate(2) <= VMEM_LIMIT else 1


def _ffn(h, g, weights, gf, batch, seq, mixer=None, proj=None, final=False, cast_next=None):
    n_side = (0 if proj is None else len(proj[2])) + (mixer is not None)
    n_sub = _ffn_sub_tiles(n_side)
    tm = n_sub * TM
    ns = seq // tm

    def rows(b, i):
        return (b * ns + i, 0)

    def fixed(b, i):
        return (0, 0)

    pair_spec = pl.BlockSpec((None, N_PAIRS, tm, PAIR_W), lambda b, i: (b, 0, i, 0))
    inputs = [h, g, *weights, gf]
    in_specs = [
        pl.BlockSpec((tm, D_MODEL), rows),
        _resident((1, D_MODEL), fixed),
        _resident((D_MODEL, D_FF), fixed),
        _resident((D_MODEL, D_FF), fixed),
        _resident((D_FF, D_MODEL), fixed),
        _resident((1, D_MODEL), fixed),
    ]
    if mixer is not None:
        inputs += list(mixer)
        in_specs += [pair_spec, _resident((D_MODEL, D_MODEL), fixed)]
    out_shape = [jax.ShapeDtypeStruct(h.shape, F32)]
    out_specs = [pl.BlockSpec((tm, D_MODEL), rows)]
    transposed = ()
    if proj is not None:
        g_mix, w_proj, transposed = proj
        assert w_proj.shape[1] == len(transposed) * D_MODEL
        inputs += [g_mix, w_proj]
        in_specs += [_resident((1, D_MODEL), fixed), _resident(w_proj.shape, fixed)]
        for flag in transposed:
            if flag:
                out_shape.append(jax.ShapeDtypeStruct((batch, N_PAIRS, PAIR_W, seq), BF16))
                out_specs.append(
                    pl.BlockSpec((None, N_PAIRS, PAIR_W, tm), lambda b, i: (b, 0, 0, i)))
            else:
                out_shape.append(jax.ShapeDtypeStruct((batch, N_PAIRS, seq, PAIR_W), BF16))
                out_specs.append(pair_spec)
    n_cast = 0
    cast_blocks = min(CAST_BLOCKS, batch * ns)
    if cast_next is not None:
        stacks, layer, pos = cast_next
        n_cast = len(stacks)
        inputs += list(stacks)
        for stack in stacks:
            n_rows, n_cols = stack.shape[2:]
            cast_rows = n_rows // cast_blocks
            assert cast_rows * cast_blocks == n_rows and cast_rows % 16 == 0
            in_specs.append(pl.BlockSpec(
                (None, None, cast_rows, n_cols),
                lambda b, i: (layer, pos, jnp.minimum(b * ns + i, cast_blocks - 1), 0)))
            out_shape.append(jax.ShapeDtypeStruct((n_rows, n_cols), BF16))
            out_specs.append(pl.BlockSpec(
                (cast_rows, n_cols), lambda b, i: (jnp.minimum(b * ns + i, cast_blocks - 1), 0)))
    return pl.pallas_call(
        functools.partial(_ffn_kernel, has_mixer=mixer is not None, transposed=tuple(transposed),
                          final=final, n_cast=n_cast, cast_blocks=cast_blocks),
        out_shape=out_shape,
        grid=(batch, ns),
        in_specs=in_specs,
        out_specs=out_specs,
        scratch_shapes=[pltpu.VMEM((n_sub, TM, D_FF), BF16)],
        compiler_params=_params("arbitrary", "arbitrary"),
        name="ffn",
    )(*inputs)


def _keep_head_rows(x_t, head_in_pair):
    row = lax.broadcasted_iota(jnp.int32, x_t.shape, 0)
    own = (row >= HEAD_DIM * head_in_pair) & (row < HEAD_DIM * (head_in_pair + 1))
    return jnp.where(own, x_t, jnp.zeros_like(x_t))


def _band_empty(j, key0, qry0):
    back = (A_KBLOCKS - 1) * QA
    q_chunks = range((qry0 + back) // CHUNK, (qry0 + LANES - 1 + back) // CHUNK + 1)
    k_chunks = range((j * QA + key0) // CHUNK, (j * QA + key0 + KSTRIP - 1) // CHUNK + 1)
    return not any(0 <= qc - kc <= LEFT_CHUNKS for qc in q_chunks for kc in k_chunks)


def _attn_a_kernel(q_ref, *refs):
    n_kv = A_KBLOCKS + A_QSUB - 1
    k_refs = refs[:n_kv]
    vt_refs = refs[n_kv:2 * n_kv]
    bias_ref, zero_ref, o_ref = refs[2 * n_kv:2 * n_kv + 3]
    s_refs = refs[2 * n_kv + 3:-1]
    m_ref = refs[-1]
    dz = zero_ref[0]
    i = pl.program_id(2)
    heads = [(pp, hh) for pp in range(A_PAIRS) for hh in range(2)]
    chunks = [slice(c * LANES, (c + 1) * LANES) for c in range(QA // LANES)]
    ones = jnp.ones((ONES_ROWS, QA), BF16)
    for sub in range(A_QSUB):
        q_cols = slice(sub * QA, (sub + 1) * QA)
        sub_refs = s_refs[sub * len(heads):(sub + 1) * len(heads)]
        pens = [jnp.where(i * A_QSUB + sub - (A_KBLOCKS - 1) + j >= 0, 0.0, NEG_INF).astype(F32)
                for j in range(A_KBLOCKS)]
        for (pp, hh), s_ref in zip(heads, sub_refs):
            qm_t = _keep_head_rows(q_ref[pp, :, q_cols], hh)
            for j in range(A_KBLOCKS):
                s_ref[j + dz] = jnp.dot(k_refs[sub + j][pp], qm_t, preferred_element_type=F32)
        for (pp, hh), s_ref in zip(heads, sub_refs):
            for cols in chunks:
                blk_max = []
                for j in range(A_KBLOCKS):
                    x = s_ref[j + dz, :, cols] + bias_ref[2 * pp + hh, j, :, cols]
                    s_ref[j + dz, :, cols] = x
                    blk_max.append(x.max(axis=0, keepdims=True) + pens[j])
                m_ref[sub, pp, hh, :, cols] = functools.reduce(jnp.maximum, blk_max)
        for pp in range(A_PAIRS):
            outs = []
            for hh in range(2):
                s_ref = sub_refs[2 * pp + hh]
                pv = None
                for j in range(A_KBLOCKS):
                    columns = []
                    for cols in chunks:
                        shift = m_ref[sub, pp, hh, :, cols] - pens[j]
                        pieces = [
                            jnp.zeros((KSTRIP, LANES), BF16)
                            if _band_empty(j, r, cols.start) else
                            jnp.exp2(s_ref[j + dz, r:r + KSTRIP, cols] - shift).astype(BF16)
                            for r in range(0, QA, KSTRIP)]
                        columns.append(jnp.concatenate(pieces, axis=0))
                    p_tile = jnp.concatenate(columns, axis=1)
                    vt = vt_refs[sub + j][pp, hh * HEAD_DIM:(hh + 1) * HEAD_DIM, :]
                    part = jnp.dot(jnp.concatenate([vt, ones], axis=0), p_tile,
                                   preferred_element_type=F32)
                    pv = part if pv is None else pv + part
                outs.append(pv[:HEAD_DIM] / pv[HEAD_DIM:HEAD_DIM + 1])
            o_ref[pp, q_cols, :] = jnp.concatenate(outs, axis=0).T.astype(BF16)


def _attn_a(q, k, vt, bias_t):
    batch, _, seq, _ = k.shape
    n_steps = seq // (A_QSUB * QA)
    n_kv = A_KBLOCKS + A_QSUB - 1

    def k_spec(t):
        back = A_KBLOCKS - 1 - t
        return pl.BlockSpec((None, A_PAIRS, QA, PAIR_W),
                            lambda b, p, i: (b, p, jnp.maximum(i * A_QSUB - back, 0), 0))

    def vt_spec(t):
        back = A_KBLOCKS - 1 - t
        return pl.BlockSpec((None, A_PAIRS, PAIR_W, QA),
                            lambda b, p, i: (b, p, 0, jnp.maximum(i * A_QSUB - back, 0)))

    return pl.pallas_call(
        _attn_a_kernel,
        out_shape=jax.ShapeDtypeStruct(k.shape, BF16),
        grid=(batch, N_PAIRS // A_PAIRS, n_steps),
        in_specs=(
            [pl.BlockSpec((None, A_PAIRS, PAIR_W, A_QSUB * QA), lambda b, p, i: (b, p, 0, i))]
            + [k_spec(t) for t in range(n_kv)]
            + [vt_spec(t) for t in range(n_kv)]
            + [_resident((N_HEADS, A_KBLOCKS, QA, QA), lambda b, p, i: (0, 0, 0, 0))]
            + [pl.BlockSpec(memory_space=pltpu.SMEM)]
        ),
        out_specs=pl.BlockSpec((None, A_PAIRS, A_QSUB * QA, PAIR_W),
                               lambda b, p, i: (b, p, i, 0)),
        scratch_shapes=(
            [pltpu.VMEM((A_KBLOCKS, QA, QA), F32)] * (A_QSUB * 2 * A_PAIRS)
            + [pltpu.VMEM((A_QSUB, A_PAIRS, 2, 1, QA), F32)]
        ),
        compiler_params=_params("parallel", "parallel", "arbitrary"),
        name="attn_a",
    )(q, *([k] * n_kv), *([vt] * n_kv), bias_t, jnp.zeros((1,), jnp.int32))


def _bias_table_kernel(vec_ref, out_ref):
    back = (A_KBLOCKS - 1) * QA
    key = lax.broadcasted_iota(jnp.int32, (QA, QA), 0)
    qry = lax.broadcasted_iota(jnp.int32, (QA, QA), 1)
    chunk_shift = CHUNK.bit_length() - 1
    for j in range(A_KBLOCKS):
        rolled = pltpu.roll(jnp.broadcast_to(vec_ref[j], (QA, 2 * QA)), 0, 1,
                            stride=1, stride_axis=0)
        dchunk = (lax.shift_right_logical(qry + back, chunk_shift)
                  - lax.shift_right_logical(key + j * QA, chunk_shift))
        valid = (dchunk >= 0) & (dchunk <= LEFT_CHUNKS)
        out_ref[j] = jnp.where(valid, rolled[:, QA:], NEG_INF)


def _attn_a_bias_table(rel_bias):
    n_heads = rel_bias.shape[0]
    back = (A_KBLOCKS - 1) * QA
    d = (back - QA * jnp.arange(A_KBLOCKS))[:, None] + jnp.arange(2 * QA)[None, :] - QA
    vec = jnp.take(rel_bias.astype(F32) * LOG2E, jnp.clip(d, -REL_CLIP, REL_CLIP) + REL_CLIP,
                   axis=1)
    vec = vec.reshape(n_heads, A_KBLOCKS, 1, 2 * QA)
    return pl.pallas_call(
        _bias_table_kernel,
        out_shape=jax.ShapeDtypeStruct((n_heads, A_KBLOCKS, QA, QA), F32),
        grid=(n_heads,),
        in_specs=[pl.BlockSpec((None, A_KBLOCKS, 1, 2 * QA), lambda h: (h, 0, 0, 0))],
        out_specs=pl.BlockSpec((None, A_KBLOCKS, QA, QA), lambda h: (h, 0, 0, 0)),
        compiler_params=_params("parallel"),
        name="bias_table",
    )(vec)


def _split3(x):
    hi = x.astype(BF16)
    r1 = x - hi.astype(F32)
    mid = r1.astype(BF16)
    lo = (r1 - mid.astype(F32)).astype(BF16)
    return hi, mid, lo


def _kvf_kernel(x_ref, g_ref, w_ref, bf_ref, tri_ref, sel_ref, one_ref,
                k_ref, vt_ref, qe_ref, ke_ref, carry_ref):
    @pl.when(pl.program_id(1) == 0)
    def _():
        carry_ref[...] = jnp.zeros_like(carry_ref)

    for sub in range(KVF_SUB):
        rows = slice(sub * TM, (sub + 1) * TM)
        hn = _rms_norm(x_ref[rows, :], g_ref[...]).astype(BF16)
        y = jnp.dot(hn, w_ref[...], preferred_element_type=F32)
        for p in range(N_PAIRS):
            k_cols = slice(p * PAIR_W, (p + 1) * PAIR_W)
            v_cols = slice(D_MODEL + p * PAIR_W, D_MODEL + (p + 1) * PAIR_W)
            k_ref[p, rows, :] = y[:, k_cols].astype(BF16)
            vt_ref[p, sub] = y[:, v_cols].T.astype(BF16)
        z = y[:, 2 * D_MODEL:] + bf_ref[...]
        log_f = jnp.minimum(z, 0.0) - jnp.log1p(jnp.exp(-jnp.abs(z)))
        cum3 = jnp.dot(tri_ref[...], jnp.concatenate(_split3(log_f), axis=1),
                       preferred_element_type=F32)
        cum = (cum3[:, :EXT_W] + cum3[:, EXT_W:2 * EXT_W] + cum3[:, 2 * EXT_W:]) + carry_ref[...]
        carry_ref[...] = cum[TM - 1:TM, :]
        pieces = jnp.concatenate(_split3(cum * LOG2E), axis=1)
        ext = jnp.dot(pieces, sel_ref[...], preferred_element_type=F32) + one_ref[...]
        qe_ref[:, rows] = ext[:, :EXT_W].T.astype(BF16)
        ke_ref[rows, :] = ext[:, EXT_W:].astype(BF16)


def _kvf(h, g, w, bf, batch, seq):
    tm = KVF_SUB * TM
    ns = seq // tm
    tri = (jnp.arange(TM)[:, None] >= jnp.arange(TM)[None, :]).astype(BF16)
    half = EXT_PER_HEAD // 2
    src = jnp.arange(3 * EXT_W)[:, None]
    dst = jnp.arange(EXT_W)[None, :]
    src_head, src_piece = src % EXT_W, src // EXT_W
    dst_head, dst_slot = dst // EXT_PER_HEAD, dst % EXT_PER_HEAD
    same_head = (src_head == dst_head) & (src_head < N_HEADS)
    sel_q = (same_head & (dst_slot == half + src_piece)).astype(BF16)
    sel_k = -(same_head & (dst_slot == src_piece)).astype(BF16)
    live = jnp.arange(EXT_W) < N_HEADS * EXT_PER_HEAD
    slot = jnp.arange(EXT_W) % EXT_PER_HEAD
    one_q = (live & (slot < half)).astype(F32).reshape(1, EXT_W)
    one_k = (live & (slot >= half)).astype(F32).reshape(1, EXT_W)
    sel = jnp.concatenate([sel_q, sel_k], axis=1)
    one = jnp.concatenate([one_q, one_k], axis=1)

    k_sds = jax.ShapeDtypeStruct((batch, N_PAIRS, seq, PAIR_W), BF16)
    k_spec = pl.BlockSpec((None, N_PAIRS, tm, PAIR_W), lambda b, i: (b, 0, i, 0))
    vt_sds = jax.ShapeDtypeStruct((batch, N_PAIRS, seq // TM, PAIR_W, TM), BF16)
    vt_spec = pl.BlockSpec((None, N_PAIRS, KVF_SUB, PAIR_W, TM), lambda b, i: (b, 0, i, 0, 0))
    ext_sds = jax.ShapeDtypeStruct((batch, seq, EXT_W), BF16)
    ext_spec = pl.BlockSpec((None, tm, EXT_W), lambda b, i: (b, i, 0))
    ext_t_sds = jax.ShapeDtypeStruct((batch, EXT_W, seq), BF16)
    ext_t_spec = pl.BlockSpec((None, EXT_W, tm), lambda b, i: (b, 0, i))
    return pl.pallas_call(
        _kvf_kernel,
        out_shape=[k_sds, vt_sds, ext_t_sds, ext_sds],
        grid=(batch, ns),
        in_specs=[
            pl.BlockSpec((tm, D_MODEL), lambda b, i: (b * ns + i, 0)),
            _resident((1, D_MODEL), lambda b, i: (0, 0)),
            _resident((D_MODEL, 2 * D_MODEL + EXT_W), lambda b, i: (0, 0)),
            _resident((1, EXT_W), lambda b, i: (0, 0)),
            _resident((TM, TM), lambda b, i: (0, 0)),
            _resident((3 * EXT_W, 2 * EXT_W), lambda b, i: (0, 0)),
            _resident((1, 2 * EXT_W), lambda b, i: (0, 0)),
        ],
        out_specs=[k_spec, vt_spec, ext_t_spec, ext_spec],
        scratch_shapes=[pltpu.VMEM((1, EXT_W), F32)],
        compiler_params=_params("parallel", "arbitrary"),
        name="kvf",
    )(h, g, w, bf, tri, sel, one)


def _fox_kernel(q_ref, k_ref, vt_ref, qe_ref, ke_ref, zero_ref, o_ref, qa_ref, *scratch):
    s_refs = scratch[:-3]
    m_ref, alpha_ref, acc_ref = scratch[-3:]
    dz = zero_ref[0]
    group = pl.program_id(1)
    i = pl.program_id(2)
    heads = [(pp, hh) for pp in range(FOX_PAIRS) for hh in range(2)]
    qe_t = qe_ref[...]
    ext_row = lax.broadcasted_iota(jnp.int32, qe_t.shape, 0)
    for slot, (pp, hh) in enumerate(heads):
        head = 2 * (group * FOX_PAIRS + pp) + hh
        own = (ext_row >= EXT_PER_HEAD * head) & (ext_row < EXT_PER_HEAD * (head + 1))
        qa_ref[slot] = jnp.concatenate(
            [_keep_head_rows(q_ref[pp], hh), jnp.where(own, qe_t, jnp.zeros_like(qe_t))], axis=0)

    half = QB // 2

    n_chunks = QB // LANES

    def key_block(blk, diagonal):
        off = pl.multiple_of(blk * QB, QB)
        k_ext = ke_ref[pl.ds(off, QB), :]
        for slot, (pp, hh) in enumerate(heads):
            ka = jnp.concatenate([k_ref[pp, pl.ds(off, QB), :], k_ext], axis=1)
            if diagonal:
                s_refs[slot][dz, :half, :half] = jnp.dot(
                    ka[:half], qa_ref[slot, :, :half], preferred_element_type=F32)
                s_refs[slot][dz, :, half:] = jnp.dot(
                    ka, qa_ref[slot, :, half:], preferred_element_type=F32)
            else:
                s_refs[slot][dz] = jnp.dot(ka, qa_ref[slot], preferred_element_type=F32)

        def live_keys(c):
            return (c + 1) * LANES if diagonal else QB

        for slot in range(len(heads)):
            for c in range(n_chunks):
                cols = slice(c * LANES, (c + 1) * LANES)
                parts = []
                n_full = c * LANES if diagonal else QB
                if n_full:
                    parts.append(s_refs[slot][dz, :n_full, cols].max(axis=0, keepdims=True))
                if diagonal:
                    tile = slice(c * LANES, (c + 1) * LANES)
                    key_id = lax.broadcasted_iota(jnp.int32, (LANES, LANES), 0)
                    qry_id = lax.broadcasted_iota(jnp.int32, (LANES, LANES), 1)
                    x = jnp.where(key_id <= qry_id, s_refs[slot][dz, tile, cols], NEG_INF)
                    s_refs[slot][dz, tile, cols] = x
                    parts.append(x.max(axis=0, keepdims=True))
                blk_max = functools.reduce(jnp.maximum, parts)
                if diagonal:
                    m_ref[slot, :, cols] = blk_max
                else:
                    m_old = m_ref[slot, :, cols]
                    m_new = jnp.maximum(m_old, blk_max)
                    m_ref[slot, :, cols] = m_new
                    alpha_ref[slot, :, cols] = jnp.exp2(m_old - m_new)
        chunks_per_tile = MXU_TILE // LANES
        ones = jnp.ones((ONES_ROWS, QB), BF16)
        for slot, (pp, hh) in enumerate(heads):
            vt = vt_ref[pp, blk, hh * HEAD_DIM:(hh + 1) * HEAD_DIM, :]
            vt = jnp.concatenate([vt, ones], axis=0)
            for nt in range(QB // MXU_TILE):
                tile_cols = slice(nt * MXU_TILE, (nt + 1) * MXU_TILE)
                n_key_tiles = (nt + 1) if diagonal else QB // MXU_TILE
                pv = None
                for kt in range(n_key_tiles):
                    columns = []
                    for c in range(nt * chunks_per_tile, (nt + 1) * chunks_per_tile):
                        cols = slice(c * LANES, (c + 1) * LANES)
                        m_new = m_ref[slot, :, cols]
                        pieces = []
                        for r in range(kt * MXU_TILE, (kt + 1) * MXU_TILE, KSTRIP):
                            if r < live_keys(c):
                                pr = jnp.exp2(s_refs[slot][dz, r:r + KSTRIP, cols] - m_new)
                                pieces.append(pr.astype(BF16))
                            else:
                                pieces.append(jnp.zeros((KSTRIP, LANES), BF16))
                        columns.append(jnp.concatenate(pieces, axis=0))
                    p_tile = jnp.concatenate(columns, axis=1)
                    part = jnp.dot(vt[:, kt * MXU_TILE:(kt + 1) * MXU_TILE], p_tile,
                                   preferred_element_type=F32)
                    pv = part if pv is None else pv + part
                if diagonal:
                    acc_ref[slot, :, tile_cols] = pv
                else:
                    acc_ref[slot, :, tile_cols] = (
                        alpha_ref[slot, :, tile_cols] * acc_ref[slot, :, tile_cols] + pv)

    def body(j, carry):
        key_block(j, False)
        return carry

    key_block(i, True)
    lax.fori_loop(0, i, body, 0)
    for pp in range(FOX_PAIRS):
        o_t = jnp.concatenate(
            [acc_ref[2 * pp + hh, :HEAD_DIM, :] / acc_ref[2 * pp + hh, HEAD_DIM:HEAD_DIM + 1, :]
             for hh in range(2)], axis=0)
        o_ref[pp] = o_t.T.astype(BF16)


def _fox(q, k, vt, q_ext, k_ext):
    batch, _, seq, _ = k.shape
    nq = seq // QB
    n_heads = 2 * FOX_PAIRS
    assert vt.shape == (batch, N_PAIRS, nq, PAIR_W, QB)
    return pl.pallas_call(
        _fox_kernel,
        out_shape=jax.ShapeDtypeStruct(k.shape, BF16),
        grid=(batch, N_PAIRS // FOX_PAIRS, nq),
        in_specs=[
            pl.BlockSpec((None, FOX_PAIRS, PAIR_W, QB), lambda b, p, i: (b, p, 0, i)),
            _resident((None, FOX_PAIRS, seq, PAIR_W), lambda b, p, i: (b, p, 0, 0)),
            _resident((None, FOX_PAIRS, nq, PAIR_W, QB), lambda b, p, i: (b, p, 0, 0, 0)),
            pl.BlockSpec((None, EXT_W, QB), lambda b, p, i: (b, 0, i)),
            _resident((None, seq, EXT_W), lambda b, p, i: (b, 0, 0)),
            pl.BlockSpec(memory_space=pltpu.SMEM),
        ],
        out_specs=pl.BlockSpec((None, FOX_PAIRS, QB, PAIR_W), lambda b, p, i: (b, p, i, 0)),
        scratch_shapes=(
            [pltpu.VMEM((n_heads, PAIR_W + EXT_W, QB), BF16)]
            + [pltpu.VMEM((1, QB, QB), F32)] * n_heads
            + [pltpu.VMEM((n_heads, 1, QB), F32),
               pltpu.VMEM((n_heads, 1, QB), F32),
               pltpu.VMEM((n_heads, HEAD_DIM + ONES_ROWS, QB), F32)]
        ),
        compiler_params=_params("parallel", "parallel", "arbitrary"),
        name="fox",
    )(q, k, vt, q_ext, k_ext, jnp.zeros((1,), jnp.int32))


def kernel(x, ffn_norm, ffn_w_gate, ffn_w_up, ffn_w_down, mix_norm, a_w_qkv, a_w_o, a_rel_bias,
           kv_norm, b_w_kvf, b_f_bias, b_w_q, b_w_o, final_norm):
    batch, seq, _ = x.shape
    depth = ffn_norm.shape[0]
    n_a = a_w_qkv.shape[0]
    n_b = b_w_q.shape[0]
    assert seq % (2 * TM) == 0 and seq % (A_QSUB * QA) == 0 and seq % QB == 0
    assert n_a + n_b == depth

    gf = final_norm.reshape(1, D_MODEL)
    stacks = (ffn_w_gate, ffn_w_up, ffn_w_down)
    ffn_weights = [ffn_w_gate[0, 0].astype(BF16), ffn_w_up[0, 0].astype(BF16),
                   ffn_w_down[0, 0].astype(BF16)]

    def half_ffn(h, layer, pos, **fused):
        nxt = 2 * layer + pos + 1
        cast_next = (stacks, nxt // 2, nxt % 2) if nxt < 2 * depth else None
        outs = _ffn(h, ffn_norm[layer, pos].reshape(1, D_MODEL), tuple(ffn_weights), gf,
                    batch, seq, cast_next=cast_next, **fused)
        if cast_next is not None:
            ffn_weights[:] = outs[-3:]
            outs = outs[:-3]
        return outs

    h = x.reshape(batch * seq, D_MODEL)
    for layer in range(n_a):
        g_mix = mix_norm[layer].reshape(1, D_MODEL)
        h, qt, k, vt = half_ffn(
            h, layer, 0, proj=(g_mix, a_w_qkv[layer].astype(BF16), (True, False, True)))
        o = _attn_a(qt, k, vt, _attn_a_bias_table(a_rel_bias[layer]))
        (h,) = half_ffn(h, layer, 1, mixer=(o, a_w_o[layer].astype(BF16)))

    w_kvf = jnp.pad(b_w_kvf, ((0, 0), (0, EXT_W - N_HEADS))).astype(BF16)
    bf = jnp.pad(b_f_bias, (0, EXT_W - N_HEADS)).reshape(1, EXT_W).astype(F32)
    k_sh, vt_sh, q_ext_t, k_ext = _kvf(h, kv_norm.reshape(1, D_MODEL), w_kvf, bf, batch, seq)

    for lb in range(n_b):
        layer = n_a + lb
        g_mix = mix_norm[layer].reshape(1, D_MODEL)
        h, qt = half_ffn(h, layer, 0, proj=(g_mix, b_w_q[lb].astype(BF16), (True,)))
        o = _fox(qt, k_sh, vt_sh, q_ext_t, k_ext)
        (h,) = half_ffn(h, layer, 1, mixer=(o, b_w_o[lb].astype(BF16)),
                        final=(layer == depth - 1))

    return h.reshape(batch, seq, D_MODEL)
```
